```python
import math
import jax, jax.numpy as jnp
from jax import lax
import numpy as np

D_MODEL = 2048
BATCH = 4
SEQ = 4096
DEPTH = 4

CHUNK = 64
Q_BLOCK = 128
EPS = 1e-6
GROUP_WIDTH = D_MODEL // 4

MLA_HEADS = 4
MLA_NOPE = 128
MLA_ROPE = 64
MLA_V = GROUP_WIDTH // MLA_HEADS
MLA_Q_RANK = D_MODEL // 4
MLA_KV_RANK = D_MODEL // 8
ROPE_THETA = 10000.0
SSM_HEAD_DIM = 64
SSM_HEADS = GROUP_WIDTH // SSM_HEAD_DIM
SSM_GROUPS = 2
SSM_STATE = 128
SSM_CONV = 4
SSM_CHUNK = CHUNK
SSM_CONV_DIM = GROUP_WIDTH + 2 * SSM_GROUPS * SSM_STATE
SB_HEADS = 4
SB_HEAD_DIM = GROUP_WIDTH // SB_HEADS
CA_HEADS = 4
CA_HEAD_DIM = GROUP_WIDTH // CA_HEADS
CA_PAST_CHUNKS = 8
CA_BAND = (CA_PAST_CHUNKS + 1) * CHUNK
CA_REL_PAST = 256
CA_REL_SIZE = CA_REL_PAST + CHUNK
N_EXPERTS = 32
TOP_K = 4
D_EXPERT = 768
SWIGLU_ALPHA = 1.702
SWIGLU_LIMIT = 7.0
MOE_BLOCK = 256

A_COLS = MLA_Q_RANK + MLA_KV_RANK + MLA_ROPE
B_COLS = GROUP_WIDTH + SSM_CONV_DIM + SSM_HEADS
C_COLS = 3 * GROUP_WIDTH
D_COLS = 3 * GROUP_WIDTH
IN_COLS = A_COLS + B_COLS + C_COLS + D_COLS

kernel_name = 'hymba_style_streaming_hybrid_moe'


def rms_norm(x, g):
    xf = x.astype(jnp.float32)
    y = xf * lax.rsqrt(jnp.mean(xf * xf, axis=-1, keepdims=True) + EPS)
    return (y * g.astype(jnp.float32)).astype(x.dtype)


def modulate(h, shift, scale):
    return h * (1 + scale[:, None, :]) + shift[:, None, :]


def rope_tables(seq):
    inv = 1.0 / (ROPE_THETA ** (jnp.arange(0, MLA_ROPE, 2, dtype=jnp.float32) / MLA_ROPE))
    ang = jnp.arange(seq, dtype=jnp.float32)[:, None] * inv[None, :]
    return jnp.cos(ang), jnp.sin(ang)


def apply_rope(x, cos, sin):
    x1, x2 = jnp.split(x.astype(jnp.float32), 2, axis=-1)
    return jnp.concatenate([x1 * cos - x2 * sin, x1 * sin + x2 * cos], axis=-1).astype(x.dtype)


def to_qblocks(t, nq):
    return jnp.moveaxis(t.reshape(t.shape[0], nq, Q_BLOCK, *t.shape[2:]), 1, 0)


def mla_attention(q_nope, q_rope, k_nope, k_rope, v):
    b, s, h, _ = q_nope.shape
    nq = s // Q_BLOCK
    scale = (MLA_NOPE + MLA_ROPE) ** -0.5
    key_chunk = jnp.arange(s) // CHUNK

    def one_block(args):
        qn, qr, blk = args
        sc = jnp.einsum('bqhd,bkhd->bhqk', qn, k_nope) + jnp.einsum('bqhr,bkr->bhqk', qr, k_rope)
        sc = sc.astype(jnp.float32) * scale
        q_chunk = (blk * Q_BLOCK + jnp.arange(Q_BLOCK)) // CHUNK
        mask = key_chunk[None, :] <= q_chunk[:, None]
        p = jax.nn.softmax(jnp.where(mask, sc, -jnp.inf), axis=-1).astype(v.dtype)
        return jnp.einsum('bhqk,bkhd->bqhd', p, v)

    out = lax.map(one_block, (to_qblocks(q_nope, nq), to_qblocks(q_rope, nq), jnp.arange(nq)))
    return jnp.moveaxis(out, 0, 1).reshape(b, s, h, -1)


def ssd_scan(x, dt, a, bm, cm):
    b, s, h, p = x.shape
    g, n = bm.shape[2], bm.shape[3]
    r = h // g
    nc = s // SSM_CHUNK
    q = SSM_CHUNK
    xq = x.astype(jnp.float32).reshape(b, nc, q, g, r, p)
    dtq = dt.reshape(b, nc, q, g, r)
    bq = bm.astype(jnp.float32).reshape(b, nc, q, g, n)
    cq = cm.astype(jnp.float32).reshape(b, nc, q, g, n)
    da_cs = jnp.cumsum(dtq * a.reshape(g, r), axis=2)
    cs_h = jnp.moveaxis(da_cs, 2, -1)
    tril = jnp.tril(jnp.ones((q, q), dtype=bool))
    diff = cs_h[..., :, None] - cs_h[..., None, :]
    decay = jnp.where(tril, jnp.exp(jnp.where(tril, diff, 0.0)), 0.0)
    cb = jnp.einsum('bctgn,bcsgn->bcgts', cq, bq)
    weights = cb[:, :, :, None] * decay * jnp.moveaxis(dtq, 2, -1)[..., None, :]
    y_diag = jnp.einsum('bcgrts,bcsgrp->bctgrp', weights, xq)
    decay_to_end = jnp.exp(da_cs[:, :, -1:] - da_cs)
    states = jnp.einsum('bcsgn,bcsgr,bcsgrp->bcgrpn', bq, decay_to_end * dtq, xq)
    chunk_decay = jnp.exp(da_cs[:, :, -1])

    def step(state, inp):
        st, dec = inp
        return state * dec[..., None, None] + st, state

    h0 = jnp.zeros((b, g, r, p, n), jnp.float32)
    _, h_in = lax.scan(step, h0, (jnp.moveaxis(states, 1, 0), jnp.moveaxis(chunk_decay, 1, 0)))
    h_in = jnp.moveaxis(h_in, 0, 1)
    y_off = jnp.einsum('bctgn,bcgrpn,bctgr->bctgrp', cq, h_in, jnp.exp(da_cs))
    return (y_diag + y_off).reshape(b, s, h, p)


def mamba2_mixer(z, xbc, dt_raw, conv_w, conv_b, dt_bias, a_log, d_skip, norm_g):
    b, s, _ = z.shape
    xbc = lax.conv_general_dilated(
        xbc, conv_w.astype(xbc.dtype)[:, None, :], window_strides=(1,),
        padding=((SSM_CONV - 1, 0),), dimension_numbers=('NWC', 'WIO', 'NWC'),
        feature_group_count=SSM_CONV_DIM) + conv_b
    xbc = jax.nn.silu(xbc)
    xs, bm, cm = jnp.split(xbc, [GROUP_WIDTH, GROUP_WIDTH + SSM_GROUPS * SSM_STATE], axis=-1)
    x = xs.reshape(b, s, SSM_HEADS, SSM_HEAD_DIM)
    bm = bm.reshape(b, s, SSM_GROUPS, SSM_STATE)
    cm = cm.reshape(b, s, SSM_GROUPS, SSM_STATE)
    dt = jax.nn.softplus(dt_raw.astype(jnp.float32) + dt_bias.astype(jnp.float32))
    a = -jnp.exp(a_log.astype(jnp.float32))
    y = ssd_scan(x, dt, a, bm, cm) + x.astype(jnp.float32) * d_skip.astype(jnp.float32)[:, None]
    y = y.reshape(b, s, GROUP_WIDTH) * jax.nn.silu(z.astype(jnp.float32))
    yg = y.reshape(b, s, SSM_GROUPS, GROUP_WIDTH // SSM_GROUPS)
    yg = yg * lax.rsqrt(jnp.mean(yg * yg, axis=-1, keepdims=True) + EPS)
    return (yg.reshape(b, s, GROUP_WIDTH) * norm_g.astype(jnp.float32)).astype(z.dtype)


def stick_breaking_attention(q, k, v):
    b, s, h, d = q.shape
    nq = s // Q_BLOCK
    scale = d ** -0.5
    key_pos = jnp.arange(s)

    def one_block(args):
        qb, blk = args
        zt = jnp.einsum('bqhd,bkhd->bhqk', qb, k).astype(jnp.float32) * scale
        q_pos = blk * Q_BLOCK + jnp.arange(Q_BLOCK)
        mask = key_pos[None, :] < q_pos[:, None]
        log_beta = jax.nn.log_sigmoid(zt)
        log_keep = jnp.where(mask, jax.nn.log_sigmoid(-zt), 0.0)
        later = lax.cumsum(log_keep, axis=3, reverse=True) - log_keep
        att = jnp.where(mask, jnp.exp(log_beta + later), 0.0).astype(v.dtype)
        return jnp.einsum('bhqk,bkhd->bqhd', att, v)

    out = lax.map(one_block, (to_qblocks(q, nq), jnp.arange(nq)))
    return jnp.moveaxis(out, 0, 1).reshape(b, s, h, d)


def chunked_relpos_attention(q, k, v, rel_bias):
    b, s, h, d = q.shape
    nc = s // CHUNK
    qc = q.reshape(b, nc, CHUNK, h, d)
    pad = ((0, 0), (CA_PAST_CHUNKS, 0), (0, 0), (0, 0), (0, 0))
    kp = jnp.pad(k.reshape(b, nc, CHUNK, h, d), pad)
    vp = jnp.pad(v.reshape(b, nc, CHUNK, h, d), pad)
    kb = jnp.concatenate([kp[:, j:j + nc] for j in range(CA_PAST_CHUNKS + 1)], axis=2)
    vb = jnp.concatenate([vp[:, j:j + nc] for j in range(CA_PAST_CHUNKS + 1)], axis=2)
    sc = jnp.einsum('bcqhd,bckhd->bchqk', qc, kb).astype(jnp.float32) * d ** -0.5
    dist = CA_PAST_CHUNKS * CHUNK + jnp.arange(CHUNK)[:, None] - jnp.arange(CA_BAND)[None, :]
    idx = jnp.clip(dist, -(CHUNK - 1), CA_REL_PAST) + (CHUNK - 1)
    sc = sc + rel_bias[:, idx].astype(jnp.float32)[None, None]
    key_chunk = jnp.arange(nc)[:, None] - CA_PAST_CHUNKS + jnp.arange(CA_BAND)[None, :] // CHUNK
    valid = (key_chunk >= 0)[None, :, None, None, :]
    p = jax.nn.softmax(jnp.where(valid, sc, -jnp.inf), axis=-1).astype(v.dtype)
    return jnp.einsum('bchqk,bckhd->bcqhd', p, vb).reshape(b, s, h, d)


def clamped_swiglu(gate, up):
    gate = jnp.minimum(gate, SWIGLU_LIMIT)
    up = jnp.clip(up, -SWIGLU_LIMIT, SWIGLU_LIMIT)
    return gate * jax.nn.sigmoid(SWIGLU_ALPHA * gate) * (up + 1.0)


def moe_ffn(h, router_w, router_b, w1, b1, w2, b2):
    b, s, d = h.shape
    n_tok = b * s
    t = h.reshape(n_tok, d)
    logits = (t @ router_w).astype(jnp.float32) + router_b.astype(jnp.float32)
    top_v, top_e = lax.top_k(logits, TOP_K)
    gates = jax.nn.softmax(top_v, axis=-1)
    flat_e = top_e.reshape(-1)
    flat_g = gates.reshape(-1)
    n_assign = n_tok * TOP_K
    order = jnp.argsort(flat_e)
    sorted_e = flat_e[order]
    counts = jnp.bincount(flat_e, length=N_EXPERTS)
    padded = (counts + MOE_BLOCK - 1) // MOE_BLOCK * MOE_BLOCK
    start = jnp.cumsum(counts) - counts
    padded_end = jnp.cumsum(padded)
    padded_start = padded_end - padded
    dest = padded_start[sorted_e] + jnp.arange(n_assign) - start[sorted_e]
    n_blocks = -(-(n_assign + N_EXPERTS * (MOE_BLOCK - 1)) // MOE_BLOCK)
    n_rows = n_blocks * MOE_BLOCK
    row_tok = jnp.zeros((n_rows,), jnp.int32).at[dest].set((order // TOP_K).astype(jnp.int32))
    row_gate = jnp.zeros((n_rows,), jnp.float32).at[dest].set(flat_g[order])
    block_e = jnp.minimum(jnp.searchsorted(padded_end, jnp.arange(n_blocks) * MOE_BLOCK, side='right'),
                          N_EXPERTS - 1)

    def run_block(args):
        tok, e = args
        xb = t[tok]
        hu = (xb @ w1[e] + b1[e]).astype(jnp.float32)
        act = clamped_swiglu(hu[:, :D_EXPERT], hu[:, D_EXPERT:]).astype(t.dtype)
        return act @ w2[e] + b2[e]

    y = lax.map(run_block, (row_tok.reshape(n_blocks, MOE_BLOCK), block_e))
    y = y.reshape(n_rows, d) * row_gate[:, None].astype(y.dtype)
    return jax.ops.segment_sum(y, row_tok, num_segments=n_tok).reshape(b, s, d)


def hybrid_layer(x, cond, cos, sin, attn_norm, ffn_norm, mod_w, mod_b, w_in,
                 mla_q_norm, mla_w_q_up, mla_kv_norm, mla_w_kv_up,
                 ssm_conv_w, ssm_conv_b, ssm_dt_bias, ssm_a_log, ssm_d, ssm_norm,
                 ca_rel_bias, mix_out_norm, w_out,
                 router_w, router_b, moe_w1, moe_b1, moe_w2, moe_b2):
    b, s, _ = x.shape
    mod = cond @ mod_w + mod_b
    shift_m, scale_m, gate_m, shift_f, scale_f, gate_f = jnp.split(mod, 6, axis=-1)

    h = modulate(rms_norm(x, attn_norm), shift_m, scale_m)
    proj = h @ w_in
    p_a, p_b, p_c, p_d = jnp.split(proj, [A_COLS, A_COLS + B_COLS, A_COLS + B_COLS + C_COLS], axis=-1)

    q_lat, kv_lat, k_rope = jnp.split(p_a, [MLA_Q_RANK, MLA_Q_RANK + MLA_KV_RANK], axis=-1)
    q = (rms_norm(q_lat, mla_q_norm) @ mla_w_q_up).reshape(b, s, MLA_HEADS, MLA_NOPE + MLA_ROPE)
    kv = (rms_norm(kv_lat, mla_kv_norm) @ mla_w_kv_up).reshape(b, s, MLA_HEADS, MLA_NOPE + MLA_V)
    q_rope = apply_rope(q[..., MLA_NOPE:], cos[:, None, :], sin[:, None, :])
    k_rope = apply_rope(k_rope, cos, sin)
    out_a = mla_attention(q[..., :MLA_NOPE], q_rope, kv[..., :MLA_NOPE], k_rope, kv[..., MLA_NOPE:])
    out_a = rms_norm(out_a.reshape(b, s, GROUP_WIDTH), mix_out_norm[0])

    z, xbc, dt_raw = jnp.split(p_b, [GROUP_WIDTH, GROUP_WIDTH + SSM_CONV_DIM], axis=-1)
    out_b = mamba2_mixer(z, xbc, dt_raw, ssm_conv_w, ssm_conv_b, ssm_dt_bias, ssm_a_log, ssm_d, ssm_norm)

    q_c, k_c, v_c = [u.reshape(b, s, SB_HEADS, SB_HEAD_DIM) for u in jnp.split(p_c, 3, axis=-1)]
    out_c = rms_norm(stick_breaking_attention(q_c, k_c, v_c).reshape(b, s, GROUP_WIDTH), mix_out_norm[1])

    q_d, k_d, v_d = [u.reshape(b, s, CA_HEADS, CA_HEAD_DIM) for u in jnp.split(p_d, 3, axis=-1)]
    out_d = chunked_relpos_attention(q_d, k_d, v_d, ca_rel_bias).reshape(b, s, GROUP_WIDTH)
    out_d = rms_norm(out_d, mix_out_norm[2])

    mixed = jnp.concatenate([out_a, out_b, out_c, out_d], axis=-1) @ w_out
    x = x + gate_m[:, None, :] * mixed

    h = modulate(rms_norm(x, ffn_norm), shift_f, scale_f)
    x = x + gate_f[:, None, :] * moe_ffn(h, router_w, router_b, moe_w1, moe_b1, moe_w2, moe_b2)
    return x


def setup_inputs(seed: int = 0) -> dict:
    key = jax.random.key(seed)
    ks = jax.random.split(key, 32)
    counter = iter(range(32))

    def nk():
        return ks[next(counter)]

    def nrm(shape, scale):
        return jax.random.normal(nk(), shape, jnp.float32) * scale

    def gain(shape):
        return 1.0 + nrm(shape, 0.02)

    L = DEPTH
    x = nrm((BATCH, SEQ, D_MODEL), 1.0)
    c = nrm((BATCH, D_MODEL), 1.0)
    attn_norm = gain((L, D_MODEL))
    ffn_norm = gain((L, D_MODEL))
    mod_w = nrm((L, D_MODEL, 6 * D_MODEL), 0.5 * D_MODEL ** -0.5)
    mod_b = nrm((L, 6 * D_MODEL), 0.02)
    w_in = nrm((L, D_MODEL, IN_COLS), D_MODEL ** -0.5)
    mla_q_norm = gain((L, MLA_Q_RANK))
    mla_w_q_up = nrm((L, MLA_Q_RANK, MLA_HEADS * (MLA_NOPE + MLA_ROPE)), MLA_Q_RANK ** -0.5)
    mla_kv_norm = gain((L, MLA_KV_RANK))
    mla_w_kv_up = nrm((L, MLA_KV_RANK, MLA_HEADS * (MLA_NOPE + MLA_V)), MLA_KV_RANK ** -0.5)
    ssm_conv_w = nrm((L, SSM_CONV, SSM_CONV_DIM), SSM_CONV ** -0.5)
    ssm_conv_b = nrm((L, SSM_CONV_DIM), 0.02)
    u = jax.random.uniform(nk(), (L, SSM_HEADS), jnp.float32)
    dt0 = jnp.exp(u * (math.log(0.1) - math.log(0.001)) + math.log(0.001))
    ssm_dt_bias = dt0 + jnp.log(-jnp.expm1(-dt0))
    ssm_a_log = jnp.log(jax.random.uniform(nk(), (L, SSM_HEADS), jnp.float32, minval=1.0, maxval=16.0))
    ssm_d = gain((L, SSM_HEADS))
    ssm_norm = gain((L, GROUP_WIDTH))
    ca_rel_bias = nrm((L, CA_HEADS, CA_REL_SIZE), 0.1)
    mix_out_norm = gain((L, 3, GROUP_WIDTH))
    w_out = nrm((L, D_MODEL, D_MODEL), D_MODEL ** -0.5)
    router_w = nrm((L, D_MODEL, N_EXPERTS), D_MODEL ** -0.5)
    router_b = nrm((L, N_EXPERTS), 0.01)
    moe_w1 = nrm((L, N_EXPERTS, D_MODEL, 2 * D_EXPERT), D_MODEL ** -0.5)
    moe_b1 = nrm((L, N_EXPERTS, 2 * D_EXPERT), 0.02)
    moe_w2 = nrm((L, N_EXPERTS, D_EXPERT, D_MODEL), D_EXPERT ** -0.5)
    moe_b2 = nrm((L, N_EXPERTS, D_MODEL), 0.02)
    final_norm = gain((D_MODEL,))
    return {'x': x, 'c': c, 'attn_norm': attn_norm, 'ffn_norm': ffn_norm, 'mod_w': mod_w, 'mod_b': mod_b,
            'w_in': w_in, 'mla_q_norm': mla_q_norm, 'mla_w_q_up': mla_w_q_up, 'mla_kv_norm': mla_kv_norm,
            'mla_w_kv_up': mla_w_kv_up, 'ssm_conv_w': ssm_conv_w, 'ssm_conv_b': ssm_conv_b,
            'ssm_dt_bias': ssm_dt_bias, 'ssm_a_log': ssm_a_log, 'ssm_d': ssm_d, 'ssm_norm': ssm_norm,
            'ca_rel_bias': ca_rel_bias, 'mix_out_norm': mix_out_norm, 'w_out': w_out,
            'router_w': router_w, 'router_b': router_b, 'moe_w1': moe_w1, 'moe_b1': moe_b1,
            'moe_w2': moe_w2, 'moe_b2': moe_b2, 'final_norm': final_norm}


def reference(x, c, attn_norm, ffn_norm, mod_w, mod_b, w_in, mla_q_norm, mla_w_q_up, mla_kv_norm,
              mla_w_kv_up, ssm_conv_w, ssm_conv_b, ssm_dt_bias, ssm_a_log, ssm_d, ssm_norm,
              ca_rel_bias, mix_out_norm, w_out, router_w, router_b, moe_w1, moe_b1, moe_w2, moe_b2,
              final_norm):
    cond = jax.nn.silu(c)
    cos, sin = rope_tables(x.shape[1])
    for i in range(DEPTH):
        x = hybrid_layer(x, cond, cos, sin, attn_norm[i], ffn_norm[i], mod_w[i], mod_b[i], w_in[i],
                         mla_q_norm[i], mla_w_q_up[i], mla_kv_norm[i], mla_w_kv_up[i],
                         ssm_conv_w[i], ssm_conv_b[i], ssm_dt_bias[i], ssm_a_log[i], ssm_d[i], ssm_norm[i],
                         ca_rel_bias[i], mix_out_norm[i], w_out[i],
                         router_w[i], router_b[i], moe_w1[i], moe_b1[i], moe_w2[i], moe_b2[i])
    return rms_norm(x, final_norm)
```

```python
import functools
import math

import jax
import jax.numpy as jnp
from jax import lax
from jax.experimental import pallas as pl
from jax.experimental.pallas import tpu as pltpu

F32 = jnp.float32
BF16 = jnp.bfloat16

D_MODEL = 2048
DEPTH = 4
CHUNK = 64
EPS = 1e-6
GROUP_WIDTH = 512
MLA_HEADS = 4
MLA_NOPE = 128
MLA_ROPE = 64
MLA_V = 128
MLA_Q_RANK = 512
MLA_KV_RANK = 256
ROPE_THETA = 10000.0
SSM_HEAD_DIM = 64
SSM_HEADS = 8
SSM_GROUPS = 2
SSM_STATE = 128
SSM_CONV = 4
SSM_CONV_DIM = 1024
SB_HEADS = 4
CA_HEADS = 4
CA_PAST_CHUNKS = 8
CA_REL_PAST = 256
N_EXPERTS = 32
TOP_K = 4
D_EXPERT = 768
SWIGLU_ALPHA = 1.702
SWIGLU_LIMIT = 7.0

A_COLS = MLA_Q_RANK + MLA_KV_RANK + MLA_ROPE
B_COLS = GROUP_WIDTH + SSM_CONV_DIM + SSM_HEADS
C_COLS = 3 * GROUP_WIDTH

LANES = 128
SUBLANES = 8
VMEM_BYTES_V7X = 64 * 1024 * 1024
NEG_BIG = -1e30

P_QLAT = 0
P_Z = 512
P_XBC = 1024
P_KVLAT = 2048
P_KROPE = 2304
P_DT = 2432
P_C = 2560
P_D = 4096
P_COLS = 5632

MOE_BLOCK = 256
ATT_TQ = 256
CA_TQ = 256
CA_WIN = CA_TQ + CA_PAST_CHUNKS * CHUNK
SSD_LC = 256


def _vmem_limit(nbytes):
    return int(min(max(2 * nbytes, 16 * 1024 * 1024), VMEM_BYTES_V7X - 8 * 1024 * 1024))


def _cparams(sem, nbytes):
    return pltpu.CompilerParams(dimension_semantics=sem, vmem_limit_bytes=_vmem_limit(nbytes))


def _rms(x, g):
    return x * lax.rsqrt(jnp.mean(x * x, axis=-1, keepdims=True) + EPS) * g


def _split3(x):
    hi = x.astype(BF16)
    r1 = x - hi.astype(F32)
    mid = r1.astype(BF16)
    lo = (r1 - mid.astype(F32)).astype(BF16)
    return hi, mid, lo


def _dot(a, b):
    return jnp.dot(a, b, preferred_element_type=F32)


def _dot_nt(a, b):
    return lax.dot_general(a, b, (((1,), (1,)), ((), ())), preferred_element_type=F32)


def _dot_tn(a, b):
    return lax.dot_general(a, b, (((0,), (0,)), ((), ())), preferred_element_type=F32)


def _dot3_left01(t01, x):
    hi, mid, lo = _split3(x)
    return _dot(t01, hi) + _dot(t01, mid) + _dot(t01, lo)


def _dot3_right01(x, t01):
    hi, mid, lo = _split3(x)
    return _dot(hi, t01) + _dot(mid, t01) + _dot(lo, t01)


def _mod_kernel(ct_ref, w_ref, b_ref, o_ref):
    k = pl.program_id(2)
    nb = o_ref.shape[1]

    @pl.when(k == 0)
    def _():
        o_ref[0] = jnp.broadcast_to(b_ref[0], o_ref.shape[1:])

    ct = ct_ref[...]
    cond = ct * jax.nn.sigmoid(ct)
    w = w_ref[0]
    for b in range(nb):
        o_ref[0, b:b + 1, :] += jnp.sum(w * cond[:, b:b + 1], axis=0, keepdims=True)


def _mod_all(c, mod_w, mod_b):
    nb, d = c.shape
    nl, _, nout = mod_w.shape
    tk, tn = 512, 2048
    ct = c.T
    return pl.pallas_call(
        _mod_kernel,
        grid=(nl, nout // tn, d // tk),
        in_specs=[pl.BlockSpec((tk, nb), lambda l, j, k: (k, 0)),
                  pl.BlockSpec((1, tk, tn), lambda l, j, k: (l, k, j)),
                  pl.BlockSpec((1, 1, tn), lambda l, j, k: (l, 0, j))],
        out_specs=pl.BlockSpec((1, nb, tn), lambda l, j, k: (l, 0, j)),
        out_shape=jax.ShapeDtypeStruct((nl, nb, nout), F32),
        compiler_params=_cparams(("parallel", "parallel", "arbitrary"), 2 * tk * tn * 4),
    )(ct, mod_w, mod_b.reshape(nl, 1, nout))


def _inproj_kernel(x_ref, g_ref, mod_ref, w_ref, o_ref, h_ref):
    @pl.when(pl.program_id(1) == 0)
    def _():
        x = x_ref[...]
        h = _rms(x, g_ref[...]) * (1.0 + mod_ref[0, 1:2, :]) + mod_ref[0, 0:1, :]
        h_ref[...] = h.astype(BF16)

    o_ref[...] = _dot(h_ref[...], w_ref[...]).astype(o_ref.dtype)


def _inproj(x2, g, mod, w, seq):
    n, d = x2.shape
    ncol = w.shape[1]
    tm, tn = 512, 512
    per_b = seq // tm
    est = 2 * tm * d * 4 + tm * d * 2 + 2 * d * tn * 2 + 2 * tm * tn * 2
    return pl.pallas_call(
        _inproj_kernel,
        grid=(n // tm, ncol // tn),
        in_specs=[pl.BlockSpec((tm, d), lambda i, j: (i, 0)),
                  pl.BlockSpec((1, d), lambda i, j: (0, 0)),
                  pl.BlockSpec((1, 6, d), lambda i, j: (i // per_b, 0, 0)),
                  pl.BlockSpec((d, tn), lambda i, j: (0, j))],
        out_specs=pl.BlockSpec((tm, tn), lambda i, j: (i, j)),
        out_shape=jax.ShapeDtypeStruct((n, ncol), BF16),
        scratch_shapes=[pltpu.VMEM((tm, d), BF16)],
        compiler_params=_cparams(("parallel", "arbitrary"), est),
    )(x2, g.reshape(1, d), mod, w)


def _rope128(y, tc, ts):
    return y * tc + pltpu.roll(y, 64, 1) * ts


def _mla_prep_kernel(ql_ref, kvl_ref, kr_ref, gq_ref, gkv_ref, wq_ref, wkv_ref, tc_ref, ts_ref,
                     q_ref, k_ref, v_ref):
    scale = (MLA_NOPE + MLA_ROPE) ** -0.5
    tc = tc_ref[...]
    ts = ts_ref[...]
    qn = _rms(ql_ref[...].astype(F32), gq_ref[...]).astype(BF16)
    yq = _dot(qn, wq_ref[...])
    kvn = _rms(kvl_ref[...].astype(F32), gkv_ref[...]).astype(BF16)
    ykv = _dot(kvn, wkv_ref[...])
    k_roped = _rope128(kr_ref[...].astype(F32), tc, ts).astype(BF16)
    for h in range(MLA_HEADS):
        o = 2 * LANES * h
        q_ref[:, o:o + LANES] = (yq[:, o:o + LANES] * scale).astype(BF16)
        q_ref[:, o + LANES:o + 2 * LANES] = (_rope128(yq[:, o + LANES:o + 2 * LANES], tc, ts) * scale).astype(BF16)
        k_ref[:, o:o + LANES] = ykv[:, LANES * h:LANES * (h + 1)].astype(BF16)
        k_ref[:, o + LANES:o + 2 * LANES] = k_roped
    v_ref[...] = ykv[:, MLA_HEADS * LANES:].astype(BF16)


def _mla_prep(proj, gq, gkv, wq, wkv, tc, ts, seq):
    n = proj.shape[0]
    tm = 512
    per_b = seq // tm
    hq = MLA_HEADS * 2 * LANES
    return pl.pallas_call(
        _mla_prep_kernel,
        grid=(n // tm,),
        in_specs=[pl.BlockSpec((tm, MLA_Q_RANK), lambda i: (i, P_QLAT // MLA_Q_RANK)),
                  pl.BlockSpec((tm, MLA_KV_RANK), lambda i: (i, P_KVLAT // MLA_KV_RANK)),
                  pl.BlockSpec((tm, LANES), lambda i: (i, P_KROPE // LANES)),
                  pl.BlockSpec((1, MLA_Q_RANK), lambda i: (0, 0)),
                  pl.BlockSpec((1, MLA_KV_RANK), lambda i: (0, 0)),
                  pl.BlockSpec((MLA_Q_RANK, hq), lambda i: (0, 0)),
                  pl.BlockSpec((MLA_KV_RANK, 2 * GROUP_WIDTH), lambda i: (0, 0)),
                  pl.BlockSpec((tm, LANES), lambda i: (i % per_b, 0)),
                  pl.BlockSpec((tm, LANES), lambda i: (i % per_b, 0))],
        out_specs=[pl.BlockSpec((tm, hq), lambda i: (i, 0)),
                   pl.BlockSpec((tm, hq), lambda i: (i, 0)),
                   pl.BlockSpec((tm, GROUP_WIDTH), lambda i: (i, 0))],
        out_shape=[jax.ShapeDtypeStruct((n, hq), BF16),
                   jax.ShapeDtypeStruct((n, hq), BF16),
                   jax.ShapeDtypeStruct((n, GROUP_WIDTH), BF16)],
        compiler_params=_cparams(("parallel",), 16 * 1024 * 1024),
    )(proj, proj, proj, gq.reshape(1, -1), gkv.reshape(1, -1), wq, wkv, tc, ts)


def _mla_flash_kernel(q_ref, k_ref, v_ref, o_ref):
    i = pl.program_id(2)
    tq = q_ref.shape[0]
    tk = tq
    q = q_ref[...]

    def step(j, carry, masked):
        m, l, acc = carry
        k = k_ref[pl.ds(pl.multiple_of(j * tk, tk), tk), :]
        v = v_ref[pl.ds(pl.multiple_of(j * tk, tk), tk), :]
        s = _dot_nt(q, k)
        if masked:
            qc = lax.broadcasted_iota(jnp.int32, (tq, tk), 0) // CHUNK
            kc = lax.broadcasted_iota(jnp.int32, (tq, tk), 1) // CHUNK
            s = jnp.where(kc <= qc, s, NEG_BIG)
        m_new = jnp.maximum(m, jnp.max(s, axis=-1, keepdims=True))
        alpha = jnp.exp(m - m_new)
        p = jnp.exp(s - m_new)
        l = alpha * l + jnp.sum(p, axis=-1, keepdims=True)
        acc = alpha * acc + _dot(p.astype(BF16), v)
        return m_new, l, acc

    init = (jnp.full((tq, 1), NEG_BIG, F32), jnp.zeros((tq, 1), F32), jnp.zeros((tq, v_ref.shape[1]), F32))
    carry = lax.fori_loop(0, i, lambda j, c: step(j, c, False), init)
    m, l, acc = step(i, carry, True)
    o_ref[...] = (acc / l).astype(o_ref.dtype)


def _mla_flash(q, k, v, batch, seq):
    n = q.shape[0]
    tq = ATT_TQ
    nq = seq // tq
    est = 2 * (seq * 2 * LANES * 2 + seq * LANES * 2) + 8 * tq * tq * 4
    return pl.pallas_call(
        _mla_flash_kernel,
        grid=(batch, MLA_HEADS, nq),
        in_specs=[pl.BlockSpec((tq, 2 * LANES), lambda b, h, i: (b * nq + i, h)),
                  pl.BlockSpec((seq, 2 * LANES), lambda b, h, i: (b, h)),
                  pl.BlockSpec((seq, LANES), lambda b, h, i: (b, h))],
        out_specs=pl.BlockSpec((tq, LANES), lambda b, h, i: (b * nq + i, h)),
        out_shape=jax.ShapeDtypeStruct((n, GROUP_WIDTH), BF16),
        compiler_params=_cparams(("parallel", "parallel", "arbitrary"), est),
    )(q, k, v)


def _sb_kernel(q_ref, k_ref, v_ref, o_ref):
    i = pl.program_id(2)
    tq = q_ref.shape[0]
    tk = tq
    scale = q_ref.shape[1] ** -0.5
    q = q_ref[...]
    row = lax.broadcasted_iota(jnp.int32, (tk, tk), 0)
    col = lax.broadcasted_iota(jnp.int32, (tk, tk), 1)
    tri = (row > col).astype(BF16)

    def step(j, carry, masked):
        run, acc = carry
        k = k_ref[pl.ds(pl.multiple_of(j * tk, tk), tk), :]
        v = v_ref[pl.ds(pl.multiple_of(j * tk, tk), tk), :]
        z = _dot_nt(q, k) * scale
        log_beta = jnp.minimum(z, 0.0) - jnp.log1p(jnp.exp(-jnp.abs(z)))
        log_keep = log_beta - z
        if masked:
            vis = col < row
            log_keep = jnp.where(vis, log_keep, 0.0)
        hi = log_keep.astype(BF16)
        lo = (log_keep - hi.astype(F32)).astype(BF16)
        later = _dot(hi, tri) + _dot(lo, tri) + run
        att = jnp.exp(log_beta + later)
        if masked:
            att = jnp.where(vis, att, 0.0)
        acc = acc + _dot(att.astype(BF16), v)
        run = run + jnp.sum(log_keep, axis=-1, keepdims=True)
        return run, acc

    carry = step(i, (jnp.zeros((tq, 1), F32), jnp.zeros((tq, v_ref.shape[1]), F32)), True)
    run, acc = lax.fori_loop(0, i, lambda jj, c: step(i - 1 - jj, c, False), carry)
    o_ref[...] = acc.astype(o_ref.dtype)


def _sb_attention(proj, batch, seq):
    n = proj.shape[0]
    tq = ATT_TQ
    nq = seq // tq
    cq, ck, cv = P_C // LANES, (P_C + GROUP_WIDTH) // LANES, (P_C + 2 * GROUP_WIDTH) // LANES
    est = 4 * seq * LANES * 2 + 10 * tq * tq * 4
    return pl.pallas_call(
        _sb_kernel,
        grid=(batch, SB_HEADS, nq),
        in_specs=[pl.BlockSpec((tq, LANES), lambda b, h, i: (b * nq + i, cq + h)),
                  pl.BlockSpec((seq, LANES), lambda b, h, i: (b, ck + h)),
                  pl.BlockSpec((seq, LANES), lambda b, h, i: (b, cv + h))],
        out_specs=pl.BlockSpec((tq, LANES), lambda b, h, i: (b * nq + i, h)),
        out_shape=jax.ShapeDtypeStruct((n, GROUP_WIDTH), BF16),
        compiler_params=_cparams(("parallel", "parallel", "arbitrary"), est),
    )(proj, proj, proj)


def _ca_kernel(q_ref, k_ref, v_ref, tab_ref, o_ref, kpad_ref, vpad_ref):
    i = pl.program_id(2)
    tq = q_ref.shape[0]
    past = CA_PAST_CHUNKS * CHUNK
    win = tq + past
    scale = q_ref.shape[1] ** -0.5

    @pl.when(i == 0)
    def _():
        kpad_ref[0:past, :] = jnp.zeros((past, kpad_ref.shape[1]), kpad_ref.dtype)
        vpad_ref[0:past, :] = jnp.zeros((past, vpad_ref.shape[1]), vpad_ref.dtype)
        kpad_ref[past:, :] = k_ref[...]
        vpad_ref[past:, :] = v_ref[...]

    start = pl.multiple_of(i * tq, tq)
    kw = kpad_ref[pl.ds(start, win), :]
    vw = vpad_ref[pl.ds(start, win), :]
    s = _dot_nt(q_ref[...], kw) * scale + tab_ref[0, 0]
    m = jnp.max(s, axis=-1, keepdims=True)
    p = jnp.exp(s - m)
    l = jnp.sum(p, axis=-1, keepdims=True)
    o_ref[...] = (_dot(p.astype(BF16), vw) / l).astype(o_ref.dtype)


def _ca_table(rel_bias, tq):
    past = CA_PAST_CHUNKS * CHUNK
    win = tq + past
    r = jnp.arange(tq)[:, None]
    c = jnp.arange(win)[None, :]
    dist = r + past - c
    idx = jnp.clip(dist, -(CHUNK - 1), CA_REL_PAST) + (CHUNK - 1)
    bias = rel_bias[:, idx].astype(F32)
    qc = r // CHUNK
    kc = c // CHUNK
    vis = (kc >= qc) & (kc <= qc + CA_PAST_CHUNKS)
    tabs = []
    n_var = past // tq + 1
    for v in range(n_var):
        first = (past - v * tq) // CHUNK if v < n_var - 1 else 0
        ok = vis & (kc >= first)
        tabs.append(jnp.where(ok[None], bias, NEG_BIG))
    return jnp.stack(tabs)


def _ca_attention(proj, table, batch, seq):
    n = proj.shape[0]
    tq = CA_TQ
    nq = seq // tq
    past = CA_PAST_CHUNKS * CHUNK
    win = tq + past
    n_var = table.shape[0]
    cq, ck, cv = P_D // LANES, (P_D + GROUP_WIDTH) // LANES, (P_D + 2 * GROUP_WIDTH) // LANES
    est = 4 * seq * LANES * 2 + 2 * (seq + past) * LANES * 2 + 2 * tq * win * 4 + 6 * tq * win * 4
    return pl.pallas_call(
        _ca_kernel,
        grid=(batch, CA_HEADS, nq),
        in_specs=[pl.BlockSpec((tq, LANES), lambda b, h, i: (b * nq + i, cq + h)),
                  pl.BlockSpec((seq, LANES), lambda b, h, i: (b, ck + h)),
                  pl.BlockSpec((seq, LANES), lambda b, h, i: (b, cv + h)),
                  pl.BlockSpec((1, 1, tq, win), lambda b, h, i: (jnp.minimum(i, n_var - 1), h, 0, 0))],
        out_specs=pl.BlockSpec((tq, LANES), lambda b, h, i: (b * nq + i, h)),
        out_shape=jax.ShapeDtypeStruct((n, GROUP_WIDTH), BF16),
        scratch_shapes=[pltpu.VMEM((seq + past, LANES), BF16), pltpu.VMEM((seq + past, LANES), BF16)],
        compiler_params=_cparams(("parallel", "parallel", "arbitrary"), est),
    )(proj, proj, proj, table)


def _ssd_kernel(z_ref, xbc_ref, dt_ref, cw_ref, cb_ref, dtb_ref, alog_ref, dskip_ref, ng_ref, ex_ref,
                o_ref, xbuf_ref, state_ref, y_ref):
    c = pl.program_id(1)
    lc = z_ref.shape[0]
    halo = SUBLANES
    gw = GROUP_WIDTH // SSM_GROUPS
    hpg = SSM_HEADS // SSM_GROUPS

    @pl.when(c == 0)
    def _():
        xbuf_ref[0:halo, :] = jnp.zeros((halo, xbuf_ref.shape[1]), F32)
        state_ref[...] = jnp.zeros(state_ref.shape, F32)

    xbuf_ref[halo:halo + lc, :] = xbc_ref[...].astype(F32)
    acc = jnp.broadcast_to(cb_ref[...], (lc, SSM_CONV_DIM))
    for k in range(SSM_CONV):
        acc = acc + cw_ref[k:k + 1, :] * xbuf_ref[pl.ds(halo - (SSM_CONV - 1) + k, lc), :]
    xbuf_ref[0:halo, :] = xbuf_ref[lc:lc + halo, :]
    u = acc * jax.nn.sigmoid(acc)
    xs = u[:, :GROUP_WIDTH]
    bm = u[:, GROUP_WIDTH:GROUP_WIDTH + SSM_GROUPS * SSM_STATE].astype(BF16)
    cm = u[:, GROUP_WIDTH + SSM_GROUPS * SSM_STATE:].astype(BF16)

    dt = jax.nn.softplus(dt_ref[...].astype(F32) + dtb_ref[...])
    a = -jnp.exp(alog_ref[...])
    da = dt * a
    row = lax.broadcasted_iota(jnp.int32, (lc, lc), 0)
    col = lax.broadcasted_iota(jnp.int32, (lc, lc), 1)
    lower = col <= row
    cs = _dot3_left01(lower.astype(BF16), da)
    cs_last = cs[lc - 1:lc, :]
    cs_t = cs.T
    dt_t = dt.T
    ex = ex_ref[...]
    ecs_x = _dot3_right01(jnp.exp(cs), ex)
    wdec_x = _dot3_right01(jnp.exp(cs_last - cs) * dt, ex)
    xs_b = xs.astype(BF16)

    for g in range(SSM_GROUPS):
        bg = bm[:, g * SSM_STATE:(g + 1) * SSM_STATE]
        cg = cm[:, g * SSM_STATE:(g + 1) * SSM_STATE]
        cb = _dot_nt(cg, bg)
        for r in range(hpg):
            h = g * hpg + r
            diff = cs[:, h:h + 1] - cs_t[h:h + 1, :]
            wgt = cb * jnp.exp(jnp.where(lower, diff, NEG_BIG)) * dt_t[h:h + 1, :]
            y_ref[:, h * SSM_HEAD_DIM:(h + 1) * SSM_HEAD_DIM] = _dot(
                wgt.astype(BF16), xs_b[:, h * SSM_HEAD_DIM:(h + 1) * SSM_HEAD_DIM])
        lo, hi = g * gw, (g + 1) * gw
        st = state_ref[g]
        y_off = _dot(cg, st.astype(BF16)) * ecs_x[:, lo:hi]
        y_ref[:, lo:hi] += y_off
        xw = (xs[:, lo:hi] * wdec_x[:, lo:hi]).astype(BF16)
        state_ref[g] = st * ecs_x[lc - 1:lc, lo:hi] + _dot_tn(bg, xw)

    zf = z_ref[...].astype(F32)
    y = (y_ref[...] + xs * dskip_ref[...]) * (zf * jax.nn.sigmoid(zf))
    for g in range(SSM_GROUPS):
        lo, hi = g * gw, (g + 1) * gw
        yg = y[:, lo:hi]
        yg = yg * lax.rsqrt(jnp.mean(yg * yg, axis=-1, keepdims=True) + EPS)
        o_ref[:, lo:hi] = (yg * ng_ref[:, lo:hi]).astype(o_ref.dtype)


def _ssd(proj, conv_w, conv_b, dt_bias, a_log, d_skip, norm_g, batch, seq):
    n = proj.shape[0]
    lc = SSD_LC
    nc = seq // lc
    dtb = jnp.zeros((1, LANES), F32).at[0, :SSM_HEADS].set(dt_bias)
    alog = jnp.full((1, LANES), NEG_BIG, F32).at[0, :SSM_HEADS].set(a_log)
    dskip = jnp.repeat(d_skip, SSM_HEAD_DIM).reshape(1, GROUP_WIDTH)
    ex = (jnp.arange(LANES)[:, None] == jnp.arange(GROUP_WIDTH)[None, :] // SSM_HEAD_DIM).astype(BF16)
    full = lambda shape: pl.BlockSpec(shape, lambda b, c: tuple(0 for _ in shape))
    return pl.pallas_call(
        _ssd_kernel,
        grid=(batch, nc),
        in_specs=[pl.BlockSpec((lc, GROUP_WIDTH), lambda b, c: (b * nc + c, P_Z // GROUP_WIDTH)),
                  pl.BlockSpec((lc, SSM_CONV_DIM), lambda b, c: (b * nc + c, P_XBC // SSM_CONV_DIM)),
                  pl.BlockSpec((lc, LANES), lambda b, c: (b * nc + c, P_DT // LANES)),
                  full((SSM_CONV, SSM_CONV_DIM)), full((1, SSM_CONV_DIM)), full((1, LANES)), full((1, LANES)),
                  full((1, GROUP_WIDTH)), full((1, GROUP_WIDTH)), full((LANES, GROUP_WIDTH))],
        out_specs=pl.BlockSpec((lc, GROUP_WIDTH), lambda b, c: (b * nc + c, 0)),
        out_shape=jax.ShapeDtypeStruct((n, GROUP_WIDTH), BF16),
        scratch_shapes=[pltpu.VMEM((lc + SUBLANES, SSM_CONV_DIM), F32),
                        pltpu.VMEM((SSM_GROUPS, SSM_STATE, GROUP_WIDTH // SSM_GROUPS), F32),
                        pltpu.VMEM((lc, GROUP_WIDTH), F32)],
        compiler_params=_cparams(("parallel", "arbitrary"), 16 * 1024 * 1024),
    )(proj, proj, proj, conv_w, conv_b.reshape(1, -1), dtb, alog, dskip, norm_g.reshape(1, -1), ex)


def _outproj_kernel(a_ref, b_ref, c_ref, d_ref, x_ref, gmix_ref, w_ref, mod_ref, gf_ref, rw_ref, rb_ref,
                    xo_ref, h_ref, e_ref, gt_ref):
    gw = GROUP_WIDTH
    a = _rms(a_ref[...].astype(F32), gmix_ref[0:1, :]).astype(BF16)
    c = _rms(c_ref[...].astype(F32), gmix_ref[1:2, :]).astype(BF16)
    d = _rms(d_ref[...].astype(F32), gmix_ref[2:3, :]).astype(BF16)
    mixed = (_dot(a, w_ref[0:gw, :]) + _dot(b_ref[...], w_ref[gw:2 * gw, :])
             + _dot(c, w_ref[2 * gw:3 * gw, :]) + _dot(d, w_ref[3 * gw:, :]))
    x = x_ref[...] + mod_ref[0, 2:3, :] * mixed
    xo_ref[...] = x
    h = _rms(x, gf_ref[...]) * (1.0 + mod_ref[0, 4:5, :]) + mod_ref[0, 3:4, :]
    h_ref[...] = h.astype(BF16)

    hh = h.astype(BF16)
    hl = (h - hh.astype(F32)).astype(BF16)
    rw = rw_ref[...]
    wh = rw.astype(BF16)
    wl = (rw - wh.astype(F32)).astype(BF16)
    logits = _dot(hh, wh) + _dot(hh, wl) + _dot(hl, wh) + rb_ref[...]
    lane = lax.broadcasted_iota(jnp.int32, logits.shape, 1)
    vals, idxs = [], []
    for _ in range(TOP_K):
        m = jnp.max(logits, axis=-1, keepdims=True)
        idx = jnp.min(jnp.where(logits == m, lane, LANES), axis=-1, keepdims=True)
        vals.append(m)
        idxs.append(idx)
        logits = jnp.where(lane == idx, -jnp.inf, logits)
    ex = [jnp.exp(v - vals[0]) for v in vals]
    tot = ex[0] + ex[1] + ex[2] + ex[3]
    e_out = jnp.zeros(logits.shape, jnp.int32)
    g_out = jnp.zeros(logits.shape, F32)
    for k in range(TOP_K):
        e_out = jnp.where(lane == k, idxs[k], e_out)
        g_out = jnp.where(lane == k, ex[k] / tot, g_out)
    e_ref[...] = e_out
    gt_ref[...] = g_out


def _outproj(oa, ob, oc, od, x2, gmix, w_out, mod, gf, rw, rb, seq):
    n, d = x2.shape
    tm = 256
    per_b = seq // tm
    gw = GROUP_WIDTH
    row = lambda width: pl.BlockSpec((tm, width), lambda i: (i, 0))
    full = lambda shape: pl.BlockSpec(shape, lambda i: tuple(0 for _ in shape))
    est = 2 * d * d * 2 + 4 * tm * d * 4 + 2 * tm * d * 2 + 8 * tm * gw * 2 + 6 * tm * d * 4
    return pl.pallas_call(
        _outproj_kernel,
        grid=(n // tm,),
        in_specs=[row(gw), row(gw), row(gw), row(gw), row(d), full((3, gw)), full((d, d)),
                  pl.BlockSpec((1, 6, d), lambda i: (i // per_b, 0, 0)), full((1, d)),
                  full((d, LANES)), full((1, LANES))],
        out_specs=[row(d), row(d), row(LANES), row(LANES)],
        out_shape=[jax.ShapeDtypeStruct((n, d), F32), jax.ShapeDtypeStruct((n, d), BF16),
                   jax.ShapeDtypeStruct((n, LANES), jnp.int32), jax.ShapeDtypeStruct((n, LANES), F32)],
        compiler_params=_cparams(("parallel",), est),
    )(oa, ob, oc, od, x2, gmix, w_out, mod, gf.reshape(1, d), rw, rb)


def _expert_kernel(be_ref, nu_ref, x_ref, w1_ref, b1_ref, w2_ref, b2_ref, o_ref):
    @pl.when(pl.program_id(0) < nu_ref[0])
    def _():
        hu = _dot(x_ref[...], w1_ref[0]) + b1_ref[0]
        gate = jnp.minimum(hu[:, :D_EXPERT], SWIGLU_LIMIT)
        up = jnp.clip(hu[:, D_EXPERT:], -SWIGLU_LIMIT, SWIGLU_LIMIT)
        act = gate * jax.nn.sigmoid(SWIGLU_ALPHA * gate) * (up + 1.0)
        o_ref[...] = (_dot(act.astype(BF16), w2_ref[0]) + b2_ref[0]).astype(o_ref.dtype)


def _experts(xs, block_e, n_used, w1, b1, w2, b2):
    n_rows, d = xs.shape
    tm = MOE_BLOCK
    nblk = n_rows // tm
    de2 = w1.shape[2]
    de = w2.shape[1]
    last = lambda i, nu: jnp.minimum(i, nu[0] - 1)
    est = 2 * (d * de2 + de * d) * 2 + 4 * tm * d * 2 + 3 * tm * de2 * 4
    grid_spec = pltpu.PrefetchScalarGridSpec(
        num_scalar_prefetch=2,
        grid=(nblk,),
        in_specs=[pl.BlockSpec((tm, d), lambda i, be, nu: (last(i, nu), 0)),
                  pl.BlockSpec((1, d, de2), lambda i, be, nu: (be[last(i, nu)], 0, 0)),
                  pl.BlockSpec((1, 1, de2), lambda i, be, nu: (be[last(i, nu)], 0, 0)),
                  pl.BlockSpec((1, de, d), lambda i, be, nu: (be[last(i, nu)], 0, 0)),
                  pl.BlockSpec((1, 1, d), lambda i, be, nu: (be[last(i, nu)], 0, 0))],
        out_specs=pl.BlockSpec((tm, d), lambda i, be, nu: (last(i, nu), 0)),
    )
    return pl.pallas_call(
        _expert_kernel,
        grid_spec=grid_spec,
        out_shape=jax.ShapeDtypeStruct((n_rows, d), BF16),
        compiler_params=_cparams(("arbitrary",), est),
    )(block_e, n_used, xs, w1, b1.reshape(b1.shape[0], 1, de2), w2, b2.reshape(b2.shape[0], 1, d))


def _final_norm_kernel(x_ref, g_ref, o_ref):
    o_ref[...] = _rms(x_ref[...], g_ref[...])


def _final_norm(x2, g):
    n, d = x2.shape
    tm = 512
    return pl.pallas_call(
        _final_norm_kernel,
        grid=(n // tm,),
        in_specs=[pl.BlockSpec((tm, d), lambda i: (i, 0)), pl.BlockSpec((1, d), lambda i: (0, 0))],
        out_specs=pl.BlockSpec((tm, d), lambda i: (i, 0)),
        out_shape=jax.ShapeDtypeStruct((n, d), F32),
        compiler_params=_cparams(("parallel",), 4 * tm * d * 4),
    )(x2, g.reshape(1, d))


def _swap_halves(w):
    half = w.shape[-1] // 2
    return jnp.concatenate([w[..., half:], w[..., :half]], axis=-1)


def _pack_w_in(w_in):
    d = w_in.shape[0]
    a0, b0, c0 = 0, A_COLS, A_COLS + B_COLS
    q_lat = w_in[:, a0:a0 + MLA_Q_RANK]
    kv_lat = w_in[:, a0 + MLA_Q_RANK:a0 + MLA_Q_RANK + MLA_KV_RANK]
    k_rope = w_in[:, a0 + MLA_Q_RANK + MLA_KV_RANK:A_COLS]
    z = w_in[:, b0:b0 + GROUP_WIDTH]
    xbc = w_in[:, b0 + GROUP_WIDTH:b0 + GROUP_WIDTH + SSM_CONV_DIM]
    dt = w_in[:, b0 + GROUP_WIDTH + SSM_CONV_DIM:c0]
    rest = w_in[:, c0:]
    pad = jnp.zeros((d, LANES - SSM_HEADS), w_in.dtype)
    return jnp.concatenate([q_lat, z, xbc, kv_lat, k_rope, _swap_halves(k_rope), dt, pad, rest], axis=1).astype(BF16)


def _pack_wq(wq):
    r = wq.shape[0]
    w = wq.reshape(r, MLA_HEADS, MLA_NOPE + MLA_ROPE)
    rope = w[..., MLA_NOPE:]
    return jnp.concatenate([w[..., :MLA_NOPE], rope, _swap_halves(rope)], axis=-1).reshape(r, -1).astype(BF16)


def _pack_wkv(wkv):
    r = wkv.shape[0]
    w = wkv.reshape(r, MLA_HEADS, MLA_NOPE + MLA_V)
    return jnp.concatenate([w[..., :MLA_NOPE].reshape(r, -1), w[..., MLA_NOPE:].reshape(r, -1)], axis=-1).astype(BF16)


def _rope_tables(seq):
    inv = 1.0 / (ROPE_THETA ** (jnp.arange(0, MLA_ROPE, 2, dtype=F32) / MLA_ROPE))
    ang = jnp.arange(seq, dtype=F32)[:, None] * inv[None, :]
    cos, sin = jnp.cos(ang), jnp.sin(ang)
    zero = jnp.zeros((seq, LANES - MLA_ROPE), F32)
    return jnp.concatenate([cos, cos, zero], axis=1), jnp.concatenate([-sin, sin, zero], axis=1)


def _route(top_e, n_tok):
    flat_e = top_e.reshape(-1)
    onehot = (flat_e[:, None] == jnp.arange(N_EXPERTS)[None, :]).astype(jnp.int32)
    before = jnp.cumsum(onehot, axis=0) - onehot
    rank = jnp.sum(before * onehot, axis=1)
    counts = jnp.sum(onehot, axis=0)
    padded = (counts + MOE_BLOCK - 1) // MOE_BLOCK * MOE_BLOCK
    padded_end = jnp.cumsum(padded)
    padded_start = padded_end - padded
    dest = padded_start[flat_e] + rank
    n_assign = n_tok * TOP_K
    n_blocks = -(-(n_assign + N_EXPERTS * (MOE_BLOCK - 1)) // MOE_BLOCK)
    block_e = jnp.minimum(jnp.searchsorted(padded_end, jnp.arange(n_blocks) * MOE_BLOCK, side='right'),
                          N_EXPERTS - 1).astype(jnp.int32)
    n_used = (padded_end[-1] // MOE_BLOCK).astype(jnp.int32).reshape(1)
    return dest, block_e, n_used, n_blocks


def kernel(x, c, attn_norm, ffn_norm, mod_w, mod_b, w_in, mla_q_norm, mla_w_q_up, mla_kv_norm, mla_w_kv_up,
           ssm_conv_w, ssm_conv_b, ssm_dt_bias, ssm_a_log, ssm_d, ssm_norm, ca_rel_bias, mix_out_norm, w_out,
           router_w, router_b, moe_w1, moe_b1, moe_w2, moe_b2, final_norm):
    batch, seq, d = x.shape
    n = batch * seq
    depth = w_in.shape[0]
    x2 = x.reshape(n, d)
    mod_all = _mod_all(c, mod_w, mod_b).reshape(depth, batch, 6, d)
    tc, ts = _rope_tables(seq)
    for l in range(depth):
        mod = mod_all[l]
        proj = _inproj(x2, attn_norm[l], mod, _pack_w_in(w_in[l]), seq)
        q, k, v = _mla_prep(proj, mla_q_norm[l], mla_kv_norm[l], _pack_wq(mla_w_q_up[l]),
                            _pack_wkv(mla_w_kv_up[l]), tc, ts, seq)
        out_a = _mla_flash(q, k, v, batch, seq)
        out_b = _ssd(proj, ssm_conv_w[l], ssm_conv_b[l], ssm_dt_bias[l], ssm_a_log[l], ssm_d[l], ssm_norm[l],
                     batch, seq)
        out_c = _sb_attention(proj, batch, seq)
        out_d = _ca_attention(proj, _ca_table(ca_rel_bias[l], CA_TQ), batch, seq)
        rw = jnp.zeros((d, LANES), F32).at[:, :N_EXPERTS].set(router_w[l])
        rb = jnp.full((1, LANES), NEG_BIG, F32).at[0, :N_EXPERTS].set(router_b[l])
        x2, h2, top_e, gates = _outproj(out_a, out_b, out_c, out_d, x2, mix_out_norm[l], w_out[l].astype(BF16),
                                        mod, ffn_norm[l], rw, rb, seq)
        top_e = top_e[:, :TOP_K]
        gates = gates[:, :TOP_K]
        dest, block_e, n_used, n_blocks = _route(top_e, n)
        tok = jnp.arange(n * TOP_K) // TOP_K
        xs = jnp.zeros((n_blocks * MOE_BLOCK, d), BF16).at[dest].set(h2[tok])
        ys = _experts(xs, block_e, n_used, moe_w1[l].astype(BF16), moe_b1[l], moe_w2[l].astype(BF16), moe_b2[l])
        moe = jnp.sum(ys[dest].reshape(n, TOP_K, d).astype(F32) * gates[:, :, None], axis=1)
        gate_f = jnp.repeat(mod[:, 5, :], seq, axis=0)
        x2 = x2 + gate_f * moe
    return _final_norm(x2, final_norm).reshape(batch, seq, d)
```

```python
import functools
import math

import jax
import jax.numpy as jnp
from jax import lax
from jax.experimental import pallas as pl
from jax.experimental.pallas import tpu as pltpu

F32 = jnp.float32
BF16 = jnp.bfloat16
U32 = jnp.uint32

D_MODEL = 2048
DEPTH = 4
CHUNK = 64
EPS = 1e-6
GROUP_WIDTH = 512
MLA_HEADS = 4
MLA_NOPE = 128
MLA_ROPE = 64
MLA_V = 128
MLA_Q_RANK = 512
MLA_KV_RANK = 256
ROPE_THETA = 10000.0
SSM_HEAD_DIM = 64
SSM_HEADS = 8
SSM_GROUPS = 2
SSM_STATE = 128
SSM_CONV = 4
SSM_CONV_DIM = 1024
SB_HEADS = 4
CA_HEADS = 4
CA_PAST_CHUNKS = 8
CA_REL_PAST = 256
N_EXPERTS = 32
TOP_K = 4
D_EXPERT = 768
SWIGLU_ALPHA = 1.702
SWIGLU_LIMIT = 7.0

A_COLS = MLA_Q_RANK + MLA_KV_RANK + MLA_ROPE
B_COLS = GROUP_WIDTH + SSM_CONV_DIM + SSM_HEADS
C_COLS = 3 * GROUP_WIDTH

LANES = 128
SUBLANES = 8
VMEM_BYTES_V7X = 64 * 1024 * 1024
NEG_BIG = -1e30

P_QLAT = 0
P_Z = 512
P_XBC = 1024
P_KVLAT = 2048
P_KROPE = 2304
P_DT = 2432
P_C = 2560
P_D = 4096
P_COLS = 5632

MOE_BLOCK = 256
ATT_TQ = 512
SB_SUB = 256
ROW_SUBLANES = 8
CA_TQ = 256
CA_WIN = CA_TQ + CA_PAST_CHUNKS * CHUNK
SSD_LC = 256


def _vmem_limit(nbytes):
    return int(min(max(2 * nbytes, 16 * 1024 * 1024), VMEM_BYTES_V7X - 8 * 1024 * 1024))


def _cparams(sem, nbytes):
    return pltpu.CompilerParams(dimension_semantics=sem, vmem_limit_bytes=_vmem_limit(nbytes))


def _rms(x, g):
    return x * lax.rsqrt(jnp.mean(x * x, axis=-1, keepdims=True) + EPS) * g


def _split3(x):
    hi = x.astype(BF16)
    r1 = x - hi.astype(F32)
    mid = r1.astype(BF16)
    lo = (r1 - mid.astype(F32)).astype(BF16)
    return hi, mid, lo


def _dot(a, b):
    return jnp.dot(a, b, preferred_element_type=F32)


def _dot_nt(a, b):
    return lax.dot_general(a, b, (((1,), (1,)), ((), ())), preferred_element_type=F32)


def _dot_tn(a, b):
    return lax.dot_general(a, b, (((0,), (0,)), ((), ())), preferred_element_type=F32)


def _dot3_left01(t01, x):
    hi, mid, lo = _split3(x)
    return _dot(t01, hi) + _dot(t01, mid) + _dot(t01, lo)


def _dot3_right01(x, t01):
    hi, mid, lo = _split3(x)
    return _dot(hi, t01) + _dot(mid, t01) + _dot(lo, t01)


def _bf16_bits(x):
    u = lax.bitcast_convert_type(x, U32)
    return (u + jnp.uint32(0x7FFF) + ((u >> 16) & jnp.uint32(1))) >> 16


def _store_packed_rows(ref, y):
    tm, d = y.shape
    half = d // 2
    for s in range(ROW_SUBLANES):
        lo = _bf16_bits(y[:, LANES * s:LANES * (s + 1)])
        hi = _bf16_bits(y[:, half + LANES * s:half + LANES * (s + 1)])
        ref[pl.ds(s, tm, stride=ROW_SUBLANES), :] = (hi << 16) | lo


def _unpack_lo(u):
    return lax.bitcast_convert_type(u << 16, F32)


def _unpack_hi(u):
    return lax.bitcast_convert_type(u & jnp.uint32(0xFFFF0000), F32)


def _mod_kernel(ct_ref, w_ref, b_ref, o_ref):
    k = pl.program_id(2)
    nb = o_ref.shape[1]

    @pl.when(k == 0)
    def _():
        o_ref[0] = jnp.broadcast_to(b_ref[0], o_ref.shape[1:])

    ct = ct_ref[...]
    cond = ct * jax.nn.sigmoid(ct)
    w = w_ref[0]
    for b in range(nb):
        o_ref[0, b:b + 1, :] += jnp.sum(w * cond[:, b:b + 1], axis=0, keepdims=True)


def _mod_all(c, mod_w, mod_b):
    nb, d = c.shape
    nl, _, nout = mod_w.shape
    tk, tn = 512, 2048
    ct = c.T
    return pl.pallas_call(
        _mod_kernel,
        grid=(nl, nout // tn, d // tk),
        in_specs=[pl.BlockSpec((tk, nb), lambda l, j, k: (k, 0)),
                  pl.BlockSpec((1, tk, tn), lambda l, j, k: (l, k, j)),
                  pl.BlockSpec((1, 1, tn), lambda l, j, k: (l, 0, j))],
        out_specs=pl.BlockSpec((1, nb, tn), lambda l, j, k: (l, 0, j)),
        out_shape=jax.ShapeDtypeStruct((nl, nb, nout), F32),
        compiler_params=_cparams(("parallel", "parallel", "arbitrary"), 2 * tk * tn * 4),
    )(ct, mod_w, mod_b.reshape(nl, 1, nout))


def _inproj_kernel(x_ref, g_ref, mod_ref, w_ref, o_ref, h_ref):
    @pl.when(pl.program_id(1) == 0)
    def _():
        x = x_ref[...]
        h = _rms(x, g_ref[...]) * (1.0 + mod_ref[0, 1:2, :]) + mod_ref[0, 0:1, :]
        h_ref[...] = h.astype(BF16)

    o_ref[...] = _dot(h_ref[...], w_ref[...]).astype(o_ref.dtype)


def _inproj(x2, g, mod, w, seq):
    n, d = x2.shape
    ncol = w.shape[1]
    tm, tn = 512, ncol // 4
    per_b = seq // tm
    est = 2 * tm * d * 4 + tm * d * 2 + 2 * d * tn * 2 + 2 * tm * tn * 2 + tm * tn * 4
    return pl.pallas_call(
        _inproj_kernel,
        grid=(n // tm, ncol // tn),
        in_specs=[pl.BlockSpec((tm, d), lambda i, j: (i, 0)),
                  pl.BlockSpec((1, d), lambda i, j: (0, 0)),
                  pl.BlockSpec((1, 6, d), lambda i, j: (i // per_b, 0, 0)),
                  pl.BlockSpec((d, tn), lambda i, j: (0, j))],
        out_specs=pl.BlockSpec((tm, tn), lambda i, j: (i, j)),
        out_shape=jax.ShapeDtypeStruct((n, ncol), BF16),
        scratch_shapes=[pltpu.VMEM((tm, d), BF16)],
        compiler_params=_cparams(("parallel", "arbitrary"), est),
    )(x2, g.reshape(1, d), mod, w)


def _rope128(y, tc, ts):
    return y * tc + pltpu.roll(y, 64, 1) * ts


def _mla_prep_kernel(ql_ref, kvl_ref, kr_ref, gq_ref, gkv_ref, wq_ref, wkv_ref, tc_ref, ts_ref,
                     q_ref, k_ref, v_ref):
    scale = (MLA_NOPE + MLA_ROPE) ** -0.5
    tc = tc_ref[...]
    ts = ts_ref[...]
    qn = _rms(ql_ref[...].astype(F32), gq_ref[...]).astype(BF16)
    yq = _dot(qn, wq_ref[...])
    kvn = _rms(kvl_ref[...].astype(F32), gkv_ref[...]).astype(BF16)
    ykv = _dot(kvn, wkv_ref[...])
    k_roped = _rope128(kr_ref[...].astype(F32), tc, ts).astype(BF16)
    for h in range(MLA_HEADS):
        o = 2 * LANES * h
        q_ref[:, o:o + LANES] = (yq[:, o:o + LANES] * scale).astype(BF16)
        q_ref[:, o + LANES:o + 2 * LANES] = (_rope128(yq[:, o + LANES:o + 2 * LANES], tc, ts) * scale).astype(BF16)
        k_ref[:, o:o + LANES] = ykv[:, LANES * h:LANES * (h + 1)].astype(BF16)
        k_ref[:, o + LANES:o + 2 * LANES] = k_roped
    v_ref[...] = ykv[:, MLA_HEADS * LANES:].astype(BF16)


def _mla_prep(proj, gq, gkv, wq, wkv, tc, ts, seq):
    n = proj.shape[0]
    tm = 512
    per_b = seq // tm
    hq = MLA_HEADS * 2 * LANES
    return pl.pallas_call(
        _mla_prep_kernel,
        grid=(n // tm,),
        in_specs=[pl.BlockSpec((tm, MLA_Q_RANK), lambda i: (i, P_QLAT // MLA_Q_RANK)),
                  pl.BlockSpec((tm, MLA_KV_RANK), lambda i: (i, P_KVLAT // MLA_KV_RANK)),
                  pl.BlockSpec((tm, LANES), lambda i: (i, P_KROPE // LANES)),
                  pl.BlockSpec((1, MLA_Q_RANK), lambda i: (0, 0)),
                  pl.BlockSpec((1, MLA_KV_RANK), lambda i: (0, 0)),
                  pl.BlockSpec((MLA_Q_RANK, hq), lambda i: (0, 0)),
                  pl.BlockSpec((MLA_KV_RANK, 2 * GROUP_WIDTH), lambda i: (0, 0)),
                  pl.BlockSpec((tm, LANES), lambda i: (i % per_b, 0)),
                  pl.BlockSpec((tm, LANES), lambda i: (i % per_b, 0))],
        out_specs=[pl.BlockSpec((tm, hq), lambda i: (i, 0)),
                   pl.BlockSpec((tm, hq), lambda i: (i, 0)),
                   pl.BlockSpec((tm, GROUP_WIDTH), lambda i: (i, 0))],
        out_shape=[jax.ShapeDtypeStruct((n, hq), BF16),
                   jax.ShapeDtypeStruct((n, hq), BF16),
                   jax.ShapeDtypeStruct((n, GROUP_WIDTH), BF16)],
        compiler_params=_cparams(("parallel",), 16 * 1024 * 1024),
    )(proj, proj, proj, gq.reshape(1, -1), gkv.reshape(1, -1), wq, wkv, tc, ts)


def _mla_flash_kernel(q_ref, k_ref, v_ref, o_ref):
    i = pl.program_id(2)
    tq = q_ref.shape[0]
    tk = tq
    q = q_ref[...]

    def step(j, carry, masked):
        m, l, acc = carry
        k = k_ref[pl.ds(pl.multiple_of(j * tk, tk), tk), :]
        v = v_ref[pl.ds(pl.multiple_of(j * tk, tk), tk), :]
        s = _dot_nt(q, k)
        if masked:
            qc = lax.broadcasted_iota(jnp.int32, (tq, tk), 0) // CHUNK
            kc = lax.broadcasted_iota(jnp.int32, (tq, tk), 1) // CHUNK
            s = jnp.where(kc <= qc, s, NEG_BIG)
        m_new = jnp.maximum(m, jnp.max(s, axis=-1, keepdims=True))
        alpha = jnp.exp(m - m_new)
        p = jnp.exp(s - m_new)
        l = alpha * l + jnp.sum(p, axis=-1, keepdims=True)
        acc = alpha * acc + _dot(p.astype(BF16), v)
        return m_new, l, acc

    init = (jnp.full((tq, 1), NEG_BIG, F32), jnp.zeros((tq, 1), F32), jnp.zeros((tq, v_ref.shape[1]), F32))
    carry = lax.fori_loop(0, i, lambda j, c: step(j, c, False), init)
    m, l, acc = step(i, carry, True)
    o_ref[...] = (acc / l).astype(o_ref.dtype)


def _mla_flash(q, k, v, batch, seq):
    n = q.shape[0]
    tq = ATT_TQ
    nq = seq // tq
    est = 2 * (seq * 2 * LANES * 2 + seq * LANES * 2) + 8 * tq * tq * 4
    return pl.pallas_call(
        _mla_flash_kernel,
        grid=(batch, MLA_HEADS, nq),
        in_specs=[pl.BlockSpec((tq, 2 * LANES), lambda b, h, i: (b * nq + i, h)),
                  pl.BlockSpec((seq, 2 * LANES), lambda b, h, i: (b, h)),
                  pl.BlockSpec((seq, LANES), lambda b, h, i: (b, h))],
        out_specs=pl.BlockSpec((tq, LANES), lambda b, h, i: (b * nq + i, h)),
        out_shape=jax.ShapeDtypeStruct((n, GROUP_WIDTH), BF16),
        compiler_params=_cparams(("parallel", "parallel", "arbitrary"), est),
    )(q, k, v)


def _sb_kernel(q_ref, k_ref, v_ref, o_ref):
    i = pl.program_id(2)
    tq = q_ref.shape[0]
    tk = tq
    sub = SB_SUB
    scale = q_ref.shape[1] ** -0.5
    q = q_ref[...]
    tri = (lax.broadcasted_iota(jnp.int32, (sub, sub), 0)
           > lax.broadcasted_iota(jnp.int32, (sub, sub), 1)).astype(BF16)

    def step(j, carry, masked):
        run, acc = carry
        k = k_ref[pl.ds(pl.multiple_of(j * tk, tk), tk), :]
        v = v_ref[pl.ds(pl.multiple_of(j * tk, tk), tk), :]
        z = _dot_nt(q, k) * scale
        log_beta = jnp.minimum(z, 0.0) - jnp.log1p(jnp.exp(-jnp.abs(z)))
        log_keep = log_beta - z
        if masked:
            vis = (lax.broadcasted_iota(jnp.int32, (tq, tk), 1)
                   < lax.broadcasted_iota(jnp.int32, (tq, tk), 0))
            log_keep = jnp.where(vis, log_keep, 0.0)
        hi = log_keep.astype(BF16)
        lo = (log_keep - hi.astype(F32)).astype(BF16)
        pieces = []
        for blk in range(tk // sub - 1, -1, -1):
            sl = slice(blk * sub, (blk + 1) * sub)
            pieces.append(_dot(hi[:, sl], tri) + _dot(lo[:, sl], tri) + run)
            run = run + jnp.sum(log_keep[:, sl], axis=-1, keepdims=True)
        later = jnp.concatenate(pieces[::-1], axis=1)
        att = jnp.exp(log_beta + later)
        if masked:
            att = jnp.where(vis, att, 0.0)
        acc = acc + _dot(att.astype(BF16), v)
        return run, acc

    carry = step(i, (jnp.zeros((tq, 1), F32), jnp.zeros((tq, v_ref.shape[1]), F32)), True)
    run, acc = lax.fori_loop(0, i, lambda jj, c: step(i - 1 - jj, c, False), carry)
    o_ref[...] = acc.astype(o_ref.dtype)


def _sb_attention(proj, batch, seq):
    n = proj.shape[0]
    tq = ATT_TQ
    nq = seq // tq
    cq, ck, cv = P_C // LANES, (P_C + GROUP_WIDTH) // LANES, (P_C + 2 * GROUP_WIDTH) // LANES
    est = 4 * seq * LANES * 2 + 10 * tq * tq * 4
    return pl.pallas_call(
        _sb_kernel,
        grid=(batch, SB_HEADS, nq),
        in_specs=[pl.BlockSpec((tq, LANES), lambda b, h, i: (b * nq + i, cq + h)),
                  pl.BlockSpec((seq, LANES), lambda b, h, i: (b, ck + h)),
                  pl.BlockSpec((seq, LANES), lambda b, h, i: (b, cv + h))],
        out_specs=pl.BlockSpec((tq, LANES), lambda b, h, i: (b * nq + i, h)),
        out_shape=jax.ShapeDtypeStruct((n, GROUP_WIDTH), BF16),
        compiler_params=_cparams(("parallel", "parallel", "arbitrary"), est),
    )(proj, proj, proj)


def _ca_kernel(q_ref, k_ref, v_ref, tab_ref, o_ref, kpad_ref, vpad_ref):
    i = pl.program_id(2)
    tq = q_ref.shape[0]
    past = CA_PAST_CHUNKS * CHUNK
    win = tq + past
    scale = q_ref.shape[1] ** -0.5

    @pl.when(i == 0)
    def _():
        kpad_ref[0:past, :] = jnp.zeros((past, kpad_ref.shape[1]), kpad_ref.dtype)
        vpad_ref[0:past, :] = jnp.zeros((past, vpad_ref.shape[1]), vpad_ref.dtype)
        kpad_ref[past:, :] = k_ref[...]
        vpad_ref[past:, :] = v_ref[...]

    start = pl.multiple_of(i * tq, tq)
    kw = kpad_ref[pl.ds(start, win), :]
    vw = vpad_ref[pl.ds(start, win), :]
    s = _dot_nt(q_ref[...], kw) * scale + tab_ref[0, 0]
    m = jnp.max(s, axis=-1, keepdims=True)
    p = jnp.exp(s - m)
    l = jnp.sum(p, axis=-1, keepdims=True)
    o_ref[...] = (_dot(p.astype(BF16), vw) / l).astype(o_ref.dtype)


def _ca_table(rel_bias, tq):
    past = CA_PAST_CHUNKS * CHUNK
    win = tq + past
    r = jnp.arange(tq)[:, None]
    c = jnp.arange(win)[None, :]
    dist = r + past - c
    idx = jnp.clip(dist, -(CHUNK - 1), CA_REL_PAST) + (CHUNK - 1)
    bias = rel_bias[:, idx].astype(F32)
    qc = r // CHUNK
    kc = c // CHUNK
    vis = (kc >= qc) & (kc <= qc + CA_PAST_CHUNKS)
    tabs = []
    n_var = past // tq + 1
    for v in range(n_var):
        first = (past - v * tq) // CHUNK if v < n_var - 1 else 0
        ok = vis & (kc >= first)
        tabs.append(jnp.where(ok[None], bias, NEG_BIG))
    return jnp.stack(tabs)


def _ca_attention(proj, table, batch, seq):
    n = proj.shape[0]
    tq = CA_TQ
    nq = seq // tq
    past = CA_PAST_CHUNKS * CHUNK
    win = tq + past
    n_var = table.shape[0]
    cq, ck, cv = P_D // LANES, (P_D + GROUP_WIDTH) // LANES, (P_D + 2 * GROUP_WIDTH) // LANES
    est = 4 * seq * LANES * 2 + 2 * (seq + past) * LANES * 2 + 2 * tq * win * 4 + 6 * tq * win * 4
    return pl.pallas_call(
        _ca_kernel,
        grid=(batch, CA_HEADS, nq),
        in_specs=[pl.BlockSpec((tq, LANES), lambda b, h, i: (b * nq + i, cq + h)),
                  pl.BlockSpec((seq, LANES), lambda b, h, i: (b, ck + h)),
                  pl.BlockSpec((seq, LANES), lambda b, h, i: (b, cv + h)),
                  pl.BlockSpec((1, 1, tq, win), lambda b, h, i: (jnp.minimum(i, n_var - 1), h, 0, 0))],
        out_specs=pl.BlockSpec((tq, LANES), lambda b, h, i: (b * nq + i, h)),
        out_shape=jax.ShapeDtypeStruct((n, GROUP_WIDTH), BF16),
        scratch_shapes=[pltpu.VMEM((seq + past, LANES), BF16), pltpu.VMEM((seq + past, LANES), BF16)],
        compiler_params=_cparams(("parallel", "parallel", "arbitrary"), est),
    )(proj, proj, proj, table)


def _ssd_kernel(z_ref, xbc_ref, dt_ref, cw_ref, cb_ref, dtb_ref, alog_ref, dskip_ref, ng_ref, ex_ref,
                o_ref, xbuf_ref, state_ref, y_ref):
    c = pl.program_id(1)
    lc = z_ref.shape[0]
    halo = SUBLANES
    gw = GROUP_WIDTH // SSM_GROUPS
    hpg = SSM_HEADS // SSM_GROUPS

    @pl.when(c == 0)
    def _():
        xbuf_ref[0:halo, :] = jnp.zeros((halo, xbuf_ref.shape[1]), F32)
        state_ref[...] = jnp.zeros(state_ref.shape, F32)

    xbuf_ref[halo:halo + lc, :] = xbc_ref[...].astype(F32)
    acc = jnp.broadcast_to(cb_ref[...], (lc, SSM_CONV_DIM))
    for k in range(SSM_CONV):
        acc = acc + cw_ref[k:k + 1, :] * xbuf_ref[pl.ds(halo - (SSM_CONV - 1) + k, lc), :]
    xbuf_ref[0:halo, :] = xbuf_ref[lc:lc + halo, :]
    u = acc * jax.nn.sigmoid(acc)
    xs = u[:, :GROUP_WIDTH]
    bm = u[:, GROUP_WIDTH:GROUP_WIDTH + SSM_GROUPS * SSM_STATE].astype(BF16)
    cm = u[:, GROUP_WIDTH + SSM_GROUPS * SSM_STATE:].astype(BF16)

    dt = jax.nn.softplus(dt_ref[...].astype(F32) + dtb_ref[...])
    a = -jnp.exp(alog_ref[...])
    da = dt * a
    row = lax.broadcasted_iota(jnp.int32, (lc, lc), 0)
    col = lax.broadcasted_iota(jnp.int32, (lc, lc), 1)
    lower = col <= row
    cs = _dot3_left01(lower.astype(BF16), da)
    cs_last = cs[lc - 1:lc, :]
    cs_t = cs.T
    dt_t = dt.T
    ex = ex_ref[...]
    ecs_x = _dot3_right01(jnp.exp(cs), ex)
    wdec_x = _dot3_right01(jnp.exp(cs_last - cs) * dt, ex)
    xs_b = xs.astype(BF16)

    for g in range(SSM_GROUPS):
        bg = bm[:, g * SSM_STATE:(g + 1) * SSM_STATE]
        cg = cm[:, g * SSM_STATE:(g + 1) * SSM_STATE]
        cb = _dot_nt(cg, bg)
        for r in range(hpg):
            h = g * hpg + r
            diff = cs[:, h:h + 1] - cs_t[h:h + 1, :]
            wgt = cb * jnp.exp(jnp.where(lower, diff, NEG_BIG)) * dt_t[h:h + 1, :]
            y_ref[:, h * SSM_HEAD_DIM:(h + 1) * SSM_HEAD_DIM] = _dot(
                wgt.astype(BF16), xs_b[:, h * SSM_HEAD_DIM:(h + 1) * SSM_HEAD_DIM])
        lo, hi = g * gw, (g + 1) * gw
        st = state_ref[g]
        y_off = _dot(cg, st.astype(BF16)) * ecs_x[:, lo:hi]
        y_ref[:, lo:hi] += y_off
        xw = (xs[:, lo:hi] * wdec_x[:, lo:hi]).astype(BF16)
        state_ref[g] = st * ecs_x[lc - 1:lc, lo:hi] + _dot_tn(bg, xw)

    zf = z_ref[...].astype(F32)
    y = (y_ref[...] + xs * dskip_ref[...]) * (zf * jax.nn.sigmoid(zf))
    for g in range(SSM_GROUPS):
        lo, hi = g * gw, (g + 1) * gw
        yg = y[:, lo:hi]
        yg = yg * lax.rsqrt(jnp.mean(yg * yg, axis=-1, keepdims=True) + EPS)
        o_ref[:, lo:hi] = (yg * ng_ref[:, lo:hi]).astype(o_ref.dtype)


def _ssd(proj, conv_w, conv_b, dt_bias, a_log, d_skip, norm_g, batch, seq):
    n = proj.shape[0]
    lc = SSD_LC
    nc = seq // lc
    dtb = jnp.zeros((1, LANES), F32).at[0, :SSM_HEADS].set(dt_bias)
    alog = jnp.full((1, LANES), NEG_BIG, F32).at[0, :SSM_HEADS].set(a_log)
    dskip = jnp.repeat(d_skip, SSM_HEAD_DIM).reshape(1, GROUP_WIDTH)
    ex = (jnp.arange(LANES)[:, None] == jnp.arange(GROUP_WIDTH)[None, :] // SSM_HEAD_DIM).astype(BF16)
    full = lambda shape: pl.BlockSpec(shape, lambda b, c: tuple(0 for _ in shape))
    return pl.pallas_call(
        _ssd_kernel,
        grid=(batch, nc),
        in_specs=[pl.BlockSpec((lc, GROUP_WIDTH), lambda b, c: (b * nc + c, P_Z // GROUP_WIDTH)),
                  pl.BlockSpec((lc, SSM_CONV_DIM), lambda b, c: (b * nc + c, P_XBC // SSM_CONV_DIM)),
                  pl.BlockSpec((lc, LANES), lambda b, c: (b * nc + c, P_DT // LANES)),
                  full((SSM_CONV, SSM_CONV_DIM)), full((1, SSM_CONV_DIM)), full((1, LANES)), full((1, LANES)),
                  full((1, GROUP_WIDTH)), full((1, GROUP_WIDTH)), full((LANES, GROUP_WIDTH))],
        out_specs=pl.BlockSpec((lc, GROUP_WIDTH), lambda b, c: (b * nc + c, 0)),
        out_shape=jax.ShapeDtypeStruct((n, GROUP_WIDTH), BF16),
        scratch_shapes=[pltpu.VMEM((lc + SUBLANES, SSM_CONV_DIM), F32),
                        pltpu.VMEM((SSM_GROUPS, SSM_STATE, GROUP_WIDTH // SSM_GROUPS), F32),
                        pltpu.VMEM((lc, GROUP_WIDTH), F32)],
        compiler_params=_cparams(("parallel", "arbitrary"), 16 * 1024 * 1024),
    )(proj, proj, proj, conv_w, conv_b.reshape(1, -1), dtb, alog, dskip, norm_g.reshape(1, -1), ex)


def _outproj_kernel(a_ref, b_ref, c_ref, d_ref, x_ref, gmix_ref, w_ref, mod_ref, gf_ref, rw_ref, rb_ref,
                    xo_ref, h_ref, e_ref, gt_ref):
    gw = GROUP_WIDTH
    a = _rms(a_ref[...].astype(F32), gmix_ref[0:1, :]).astype(BF16)
    c = _rms(c_ref[...].astype(F32), gmix_ref[1:2, :]).astype(BF16)
    d = _rms(d_ref[...].astype(F32), gmix_ref[2:3, :]).astype(BF16)
    mixed = (_dot(a, w_ref[0:gw, :]) + _dot(b_ref[...], w_ref[gw:2 * gw, :])
             + _dot(c, w_ref[2 * gw:3 * gw, :]) + _dot(d, w_ref[3 * gw:, :]))
    x = x_ref[...] + mod_ref[0, 2:3, :] * mixed
    xo_ref[...] = x
    h = _rms(x, gf_ref[...]) * (1.0 + mod_ref[0, 4:5, :]) + mod_ref[0, 3:4, :]
    _store_packed_rows(h_ref, h)

    hh = h.astype(BF16)
    hl = (h - hh.astype(F32)).astype(BF16)
    rw = rw_ref[...]
    wh = rw.astype(BF16)
    wl = (rw - wh.astype(F32)).astype(BF16)
    logits = _dot(hh, wh) + _dot(hh, wl) + _dot(hl, wh) + rb_ref[...]
    lane = lax.broadcasted_iota(jnp.int32, logits.shape, 1)
    vals, idxs = [], []
    for _ in range(TOP_K):
        m = jnp.max(logits, axis=-1, keepdims=True)
        idx = jnp.min(jnp.where(logits == m, lane, LANES), axis=-1, keepdims=True)
        vals.append(m)
        idxs.append(idx)
        logits = jnp.where(lane == idx, -jnp.inf, logits)
    ex = [jnp.exp(v - vals[0]) for v in vals]
    tot = ex[0] + ex[1] + ex[2] + ex[3]
    e_out = jnp.zeros(logits.shape, jnp.int32)
    g_out = jnp.zeros(logits.shape, F32)
    for k in range(TOP_K):
        e_out = jnp.where(lane == k, idxs[k], e_out)
        g_out = jnp.where(lane == k, ex[k] / tot, g_out)
    e_ref[...] = e_out
    gt_ref[...] = g_out


def _outproj(oa, ob, oc, od, x2, gmix, w_out, mod, gf, rw, rb, seq):
    n, d = x2.shape
    tm = 256
    per_b = seq // tm
    gw = GROUP_WIDTH
    row = lambda width: pl.BlockSpec((tm, width), lambda i: (i, 0))
    full = lambda shape: pl.BlockSpec(shape, lambda i: tuple(0 for _ in shape))
    est = 2 * d * d * 2 + 4 * tm * d * 4 + 2 * tm * d * 2 + 8 * tm * gw * 2 + 6 * tm * d * 4
    return pl.pallas_call(
        _outproj_kernel,
        grid=(n // tm,),
        in_specs=[row(gw), row(gw), row(gw), row(gw), row(d), full((3, gw)), full((d, d)),
                  pl.BlockSpec((1, 6, d), lambda i: (i // per_b, 0, 0)), full((1, d)),
                  full((d, LANES)), full((1, LANES))],
        out_specs=[row(d), pl.BlockSpec((tm * ROW_SUBLANES, LANES), lambda i: (i, 0)), row(LANES), row(LANES)],
        out_shape=[jax.ShapeDtypeStruct((n, d), F32), jax.ShapeDtypeStruct((n * ROW_SUBLANES, LANES), U32),
                   jax.ShapeDtypeStruct((n, LANES), jnp.int32), jax.ShapeDtypeStruct((n, LANES), F32)],
        compiler_params=_cparams(("parallel",), est),
    )(oa, ob, oc, od, x2, gmix, w_out, mod, gf.reshape(1, d), rw, rb)


def _row_copy(src_ref, src_row, dst_ref, dst_row, sem):
    rs = ROW_SUBLANES
    return pltpu.make_async_copy(src_ref.at[pl.ds(pl.multiple_of(src_row * rs, rs), rs)],
                                 dst_ref.at[pl.ds(pl.multiple_of(dst_row * rs, rs), rs)], sem)


def _expert_kernel(be_ref, nu_ref, tok_ref, tok_next_ref, dst_ref, hp_ref, w1_ref, b1_ref, w2_ref, b2_ref,
                   y_ref, xbuf_ref, ybuf_ref, gsem, ssem):
    i = pl.program_id(0)
    nu = nu_ref[0]
    tm = MOE_BLOCK
    slot = lax.rem(i, 2)

    def start_gather(idx_ref, to_slot):
        def body(r, carry):
            _row_copy(hp_ref, idx_ref[0, 0, r], xbuf_ref.at[to_slot], r, gsem.at[to_slot]).start()
            return carry
        lax.fori_loop(0, tm, body, 0)

    def wait_rows(src_ref, dst_ref, sem):
        for _ in range(tm):
            _row_copy(src_ref, 0, dst_ref, 0, sem).wait()

    @pl.when(i == 0)
    def _():
        start_gather(tok_ref, 0)
        ybuf_ref[...] = jnp.zeros(ybuf_ref.shape, U32)
        spare = pltpu.make_async_copy(ybuf_ref, y_ref.at[pl.ds(y_ref.shape[0] - tm * ROW_SUBLANES, tm * ROW_SUBLANES)],
                                      ssem)
        spare.start()
        spare.wait()

    @pl.when(i + 1 < nu)
    def _():
        start_gather(tok_next_ref, 1 - slot)

    @pl.when(i < nu)
    def _():
        xb = xbuf_ref.at[slot]
        wait_rows(hp_ref, xb, gsem.at[slot])
        los, his = [], []
        for s in range(ROW_SUBLANES):
            u = xb[pl.ds(s, tm, stride=ROW_SUBLANES), :]
            los.append(_unpack_lo(u).astype(BF16))
            his.append(_unpack_hi(u).astype(BF16))
        x = jnp.concatenate(los + his, axis=1)
        hu = _dot(x, w1_ref[0]) + b1_ref[0]
        gate = jnp.minimum(hu[:, :D_EXPERT], SWIGLU_LIMIT)
        up = jnp.clip(hu[:, D_EXPERT:], -SWIGLU_LIMIT, SWIGLU_LIMIT)
        act = gate * jax.nn.sigmoid(SWIGLU_ALPHA * gate) * (up + 1.0)
        y = _dot(act.astype(BF16), w2_ref[0]) + b2_ref[0]

        @pl.when(i > 0)
        def _():
            wait_rows(ybuf_ref, y_ref, ssem)

        _store_packed_rows(ybuf_ref, y)

        def body(r, carry):
            _row_copy(ybuf_ref, r, y_ref, dst_ref[0, 0, r], ssem).start()
            return carry
        lax.fori_loop(0, tm, body, 0)

        @pl.when(i == nu - 1)
        def _():
            wait_rows(ybuf_ref, y_ref, ssem)


def _experts(hp, row_tok, row_dst, block_e, n_used, n_slots, w1, b1, w2, b2):
    tm = MOE_BLOCK
    nblk = row_tok.shape[0]
    d, de2 = w1.shape[1], w1.shape[2]
    de = w2.shape[1]
    rs = ROW_SUBLANES
    last = lambda i, nu: jnp.minimum(i, nu[0] - 1)
    smem_rows = lambda f: pl.BlockSpec((1, 1, tm), f, memory_space=pltpu.SMEM)
    est = 2 * (d * de2 + de * d) * 2 + 3 * tm * rs * LANES * 4 + 4 * tm * d * 4 + 3 * tm * de2 * 4
    grid_spec = pltpu.PrefetchScalarGridSpec(
        num_scalar_prefetch=2,
        grid=(nblk,),
        in_specs=[smem_rows(lambda i, be, nu: (last(i, nu), 0, 0)),
                  smem_rows(lambda i, be, nu: (last(i + 1, nu), 0, 0)),
                  smem_rows(lambda i, be, nu: (last(i, nu), 0, 0)),
                  pl.BlockSpec(memory_space=pl.ANY),
                  pl.BlockSpec((1, d, de2), lambda i, be, nu: (be[last(i, nu)], 0, 0)),
                  pl.BlockSpec((1, 1, de2), lambda i, be, nu: (be[last(i, nu)], 0, 0)),
                  pl.BlockSpec((1, de, d), lambda i, be, nu: (be[last(i, nu)], 0, 0)),
                  pl.BlockSpec((1, 1, d), lambda i, be, nu: (be[last(i, nu)], 0, 0))],
        out_specs=pl.BlockSpec(memory_space=pl.ANY),
        scratch_shapes=[pltpu.VMEM((2, tm * rs, LANES), U32), pltpu.VMEM((tm * rs, LANES), U32),
                        pltpu.SemaphoreType.DMA((2,)), pltpu.SemaphoreType.DMA(())],
    )
    return pl.pallas_call(
        _expert_kernel,
        grid_spec=grid_spec,
        out_shape=jax.ShapeDtypeStruct((n_slots * rs, LANES), U32),
        compiler_params=_cparams(("arbitrary",), est),
    )(block_e, n_used, row_tok, row_tok, row_dst, hp, w1, b1.reshape(b1.shape[0], 1, de2), w2,
      b2.reshape(b2.shape[0], 1, d))


def _combine_kernel(y_ref, g_ref, x_ref, mod_ref, o_ref):
    tm, d = x_ref.shape
    half = d // 2
    g = g_ref[...]
    gk = [g[:, k:k + 1] for k in range(TOP_K)]
    for s in range(ROW_SUBLANES):
        lo = jnp.zeros((tm, LANES), F32)
        hi = jnp.zeros((tm, LANES), F32)
        for k in range(TOP_K):
            u = y_ref[pl.ds(k * ROW_SUBLANES + s, tm, stride=TOP_K * ROW_SUBLANES), :]
            lo = lo + _unpack_lo(u) * gk[k]
            hi = hi + _unpack_hi(u) * gk[k]
        a, b = LANES * s, half + LANES * s
        o_ref[:, a:a + LANES] = x_ref[:, a:a + LANES] + mod_ref[0, 5:6, a:a + LANES] * lo
        o_ref[:, b:b + LANES] = x_ref[:, b:b + LANES] + mod_ref[0, 5:6, b:b + LANES] * hi


def _combine(y4, gates, x2, mod, seq):
    n, d = x2.shape
    tm = 256
    per_b = seq // tm
    rows = tm * TOP_K * ROW_SUBLANES
    est = 2 * rows * LANES * 4 + 4 * tm * d * 4 + 2 * tm * LANES * 4
    return pl.pallas_call(
        _combine_kernel,
        grid=(n // tm,),
        in_specs=[pl.BlockSpec((rows, LANES), lambda i: (i, 0)),
                  pl.BlockSpec((tm, LANES), lambda i: (i, 0)),
                  pl.BlockSpec((tm, d), lambda i: (i, 0)),
                  pl.BlockSpec((1, 6, d), lambda i: (i // per_b, 0, 0))],
        out_specs=pl.BlockSpec((tm, d), lambda i: (i, 0)),
        out_shape=jax.ShapeDtypeStruct((n, d), F32),
        compiler_params=_cparams(("parallel",), est),
    )(y4, gates, x2, mod)


def _final_norm_kernel(x_ref, g_ref, o_ref):
    o_ref[...] = _rms(x_ref[...], g_ref[...])


def _final_norm(x2, g):
    n, d = x2.shape
    tm = 512
    return pl.pallas_call(
        _final_norm_kernel,
        grid=(n // tm,),
        in_specs=[pl.BlockSpec((tm, d), lambda i: (i, 0)), pl.BlockSpec((1, d), lambda i: (0, 0))],
        out_specs=pl.BlockSpec((tm, d), lambda i: (i, 0)),
        out_shape=jax.ShapeDtypeStruct((n, d), F32),
        compiler_params=_cparams(("parallel",), 4 * tm * d * 4),
    )(x2, g.reshape(1, d))


def _swap_halves(w):
    half = w.shape[-1] // 2
    return jnp.concatenate([w[..., half:], w[..., :half]], axis=-1)


def _pack_w_in(w_in):
    d = w_in.shape[0]
    a0, b0, c0 = 0, A_COLS, A_COLS + B_COLS
    q_lat = w_in[:, a0:a0 + MLA_Q_RANK]
    kv_lat = w_in[:, a0 + MLA_Q_RANK:a0 + MLA_Q_RANK + MLA_KV_RANK]
    k_rope = w_in[:, a0 + MLA_Q_RANK + MLA_KV_RANK:A_COLS]
    z = w_in[:, b0:b0 + GROUP_WIDTH]
    xbc = w_in[:, b0 + GROUP_WIDTH:b0 + GROUP_WIDTH + SSM_CONV_DIM]
    dt = w_in[:, b0 + GROUP_WIDTH + SSM_CONV_DIM:c0]
    rest = w_in[:, c0:]
    pad = jnp.zeros((d, LANES - SSM_HEADS), w_in.dtype)
    return jnp.concatenate([q_lat, z, xbc, kv_lat, k_rope, _swap_halves(k_rope), dt, pad, rest], axis=1).astype(BF16)


def _pack_wq(wq):
    r = wq.shape[0]
    w = wq.reshape(r, MLA_HEADS, MLA_NOPE + MLA_ROPE)
    rope = w[..., MLA_NOPE:]
    return jnp.concatenate([w[..., :MLA_NOPE], rope, _swap_halves(rope)], axis=-1).reshape(r, -1).astype(BF16)


def _pack_wkv(wkv):
    r = wkv.shape[0]
    w = wkv.reshape(r, MLA_HEADS, MLA_NOPE + MLA_V)
    return jnp.concatenate([w[..., :MLA_NOPE].reshape(r, -1), w[..., MLA_NOPE:].reshape(r, -1)], axis=-1).astype(BF16)


def _rope_tables(seq):
    inv = 1.0 / (ROPE_THETA ** (jnp.arange(0, MLA_ROPE, 2, dtype=F32) / MLA_ROPE))
    ang = jnp.arange(seq, dtype=F32)[:, None] * inv[None, :]
    cos, sin = jnp.cos(ang), jnp.sin(ang)
    zero = jnp.zeros((seq, LANES - MLA_ROPE), F32)
    return jnp.concatenate([cos, cos, zero], axis=1), jnp.concatenate([-sin, sin, zero], axis=1)


def _route(top_e, n_tok):
    flat_e = top_e.reshape(-1)
    n_assign = n_tok * TOP_K
    onehot = (flat_e[:, None] == jnp.arange(N_EXPERTS)[None, :]).astype(jnp.int32)
    before = jnp.cumsum(onehot, axis=0) - onehot
    rank = jnp.sum(before * onehot, axis=1)
    counts = jnp.sum(onehot, axis=0)
    padded = (counts + MOE_BLOCK - 1) // MOE_BLOCK * MOE_BLOCK
    padded_end = jnp.cumsum(padded)
    padded_start = padded_end - padded
    dest = padded_start[flat_e] + rank
    n_blocks = -(-(n_assign + N_EXPERTS * (MOE_BLOCK - 1)) // MOE_BLOCK)
    n_rows = n_blocks * MOE_BLOCK
    block_e = jnp.minimum(jnp.searchsorted(padded_end, jnp.arange(n_blocks) * MOE_BLOCK, side='right'),
                          N_EXPERTS - 1).astype(jnp.int32)
    n_used = (padded_end[-1] // MOE_BLOCK).astype(jnp.int32).reshape(1)
    row_assign = jnp.full((n_rows,), -1, jnp.int32).at[dest].set(jnp.arange(n_assign, dtype=jnp.int32))
    real = row_assign >= 0
    row_tok = jnp.where(real, row_assign // TOP_K, 0)
    row_dst = jnp.where(real, row_assign, n_assign + jnp.arange(n_rows, dtype=jnp.int32) % MOE_BLOCK)
    shape = (n_blocks, 1, MOE_BLOCK)
    return row_tok.reshape(shape), row_dst.reshape(shape), block_e, n_used, n_assign + MOE_BLOCK


def kernel(x, c, attn_norm, ffn_norm, mod_w, mod_b, w_in, mla_q_norm, mla_w_q_up, mla_kv_norm, mla_w_kv_up,
           ssm_conv_w, ssm_conv_b, ssm_dt_bias, ssm_a_log, ssm_d, ssm_norm, ca_rel_bias, mix_out_norm, w_out,
           router_w, router_b, moe_w1, moe_b1, moe_w2, moe_b2, final_norm):
    batch, seq, d = x.shape
    n = batch * seq
    depth = w_in.shape[0]
    x2 = x.reshape(n, d)
    mod_all = _mod_all(c, mod_w, mod_b).reshape(depth, batch, 6, d)
    tc, ts = _rope_tables(seq)
    for l in range(depth):
        mod = mod_all[l]
        proj = _inproj(x2, attn_norm[l], mod, _pack_w_in(w_in[l]), seq)
        q, k, v = _mla_prep(proj, mla_q_norm[l], mla_kv_norm[l], _pack_wq(mla_w_q_up[l]),
                            _pack_wkv(mla_w_kv_up[l]), tc, ts, seq)
        out_a = _mla_flash(q, k, v, batch, seq)
        out_b = _ssd(proj, ssm_conv_w[l], ssm_conv_b[l], ssm_dt_bias[l], ssm_a_log[l], ssm_d[l], ssm_norm[l],
                     batch, seq)
        out_c = _sb_attention(proj, batch, seq)
        out_d = _ca_attention(proj, _ca_table(ca_rel_bias[l], CA_TQ), batch, seq)
        rw = jnp.zeros((d, LANES), F32).at[:, :N_EXPERTS].set(router_w[l])
        rb = jnp.full((1, LANES), NEG_BIG, F32).at[0, :N_EXPERTS].set(router_b[l])
        x2, hp, top_e, gates = _outproj(out_a, out_b, out_c, out_d, x2, mix_out_norm[l], w_out[l].astype(BF16),
                                        mod, ffn_norm[l], rw, rb, seq)
        row_tok, row_dst, block_e, n_used, n_slots = _route(top_e[:, :TOP_K], n)
        y4 = _experts(hp, row_tok, row_dst, block_e, n_used, n_slots,
                      moe_w1[l].astype(BF16), moe_b1[l], moe_w2[l].astype(BF16), moe_b2[l])
        x2 = _combine(y4, gates, x2, mod, seq)
    return _final_norm(x2, final_norm).reshape(batch, seq, d)
```

```python
import functools
import math

import jax
import jax.numpy as jnp
from jax import lax
from jax.experimental import pallas as pl
from jax.experimental.pallas import tpu as pltpu

F32 = jnp.float32
BF16 = jnp.bfloat16
U32 = jnp.uint32

D_MODEL = 2048
DEPTH = 4
CHUNK = 64
EPS = 1e-6
GROUP_WIDTH = 512
MLA_HEADS = 4
MLA_NOPE = 128
MLA_ROPE = 64
MLA_V = 128
MLA_Q_RANK = 512
MLA_KV_RANK = 256
ROPE_THETA = 10000.0
SSM_HEAD_DIM = 64
SSM_HEADS = 8
SSM_GROUPS = 2
SSM_STATE = 128
SSM_CONV = 4
SSM_CONV_DIM = 1024
SB_HEADS = 4
CA_HEADS = 4
CA_PAST_CHUNKS = 8
CA_REL_PAST = 256
N_EXPERTS = 32
TOP_K = 4
D_EXPERT = 768
SWIGLU_ALPHA = 1.702
SWIGLU_LIMIT = 7.0

A_COLS = MLA_Q_RANK + MLA_KV_RANK + MLA_ROPE
B_COLS = GROUP_WIDTH + SSM_CONV_DIM + SSM_HEADS
C_COLS = 3 * GROUP_WIDTH

LANES = 128
SUBLANES = 8
VMEM_BYTES_V7X = 64 * 1024 * 1024
NEG_BIG = -1e30
LOG2_E = math.log2(math.e)

P_QLAT = 0
P_Z = 512
P_XBC = 1024
P_KVLAT = 2048
P_KROPE = 2304
P_DT = 2432
P_C = 2560
P_D = 4096
P_COLS = 5632

MOE_BLOCK = 256
EXPERT_COL_CHUNK = 256
ATT_TQ = 512
SB_SUB = 256
ROW_SUBLANES = 8
CA_TQ = 256
CA_WIN = CA_TQ + CA_PAST_CHUNKS * CHUNK
SSD_LC = 256


def _vmem_limit(nbytes):
    return int(min(max(2 * nbytes, 16 * 1024 * 1024), VMEM_BYTES_V7X - 8 * 1024 * 1024))


def _cparams(sem, nbytes):
    return pltpu.CompilerParams(dimension_semantics=sem, vmem_limit_bytes=_vmem_limit(nbytes))


def _rms(x, g):
    return x * lax.rsqrt(jnp.mean(x * x, axis=-1, keepdims=True) + EPS) * g


def _split3(x):
    hi = x.astype(BF16)
    r1 = x - hi.astype(F32)
    mid = r1.astype(BF16)
    lo = (r1 - mid.astype(F32)).astype(BF16)
    return hi, mid, lo


def _dot(a, b):
    return jnp.dot(a, b, preferred_element_type=F32)


def _dot_nt(a, b):
    return lax.dot_general(a, b, (((1,), (1,)), ((), ())), preferred_element_type=F32)


def _dot_tn(a, b):
    return lax.dot_general(a, b, (((0,), (0,)), ((), ())), preferred_element_type=F32)


def _dot3_left01(t01, x):
    hi, mid, lo = _split3(x)
    return _dot(t01, hi) + _dot(t01, mid) + _dot(t01, lo)


def _dot3_right01(x, t01):
    hi, mid, lo = _split3(x)
    return _dot(hi, t01) + _dot(mid, t01) + _dot(lo, t01)


def _bf16_bits(x):
    u = lax.bitcast_convert_type(x, U32)
    return (u + jnp.uint32(0x7FFF) + ((u >> 16) & jnp.uint32(1))) >> 16


def _store_packed_rows(ref, y):
    tm, d = y.shape
    half = d // 2
    for s in range(ROW_SUBLANES):
        lo = _bf16_bits(y[:, LANES * s:LANES * (s + 1)])
        hi = _bf16_bits(y[:, half + LANES * s:half + LANES * (s + 1)])
        ref[pl.ds(s, tm, stride=ROW_SUBLANES), :] = (hi << 16) | lo


def _unpack_lo(u):
    return lax.bitcast_convert_type(u << 16, F32)


def _unpack_hi(u):
    return lax.bitcast_convert_type(u & jnp.uint32(0xFFFF0000), F32)


def _mod_kernel(ct_ref, w_ref, b_ref, o_ref):
    k = pl.program_id(2)
    nb = o_ref.shape[1]

    @pl.when(k == 0)
    def _():
        o_ref[0] = jnp.broadcast_to(b_ref[0], o_ref.shape[1:])

    ct = ct_ref[...]
    cond = ct * jax.nn.sigmoid(ct)
    w = w_ref[0]
    for b in range(nb):
        o_ref[0, b:b + 1, :] += jnp.sum(w * cond[:, b:b + 1], axis=0, keepdims=True)


def _mod_all(c, mod_w, mod_b):
    nb, d = c.shape
    nl, _, nout = mod_w.shape
    tk, tn = 512, 2048
    ct = c.T
    return pl.pallas_call(
        _mod_kernel,
        grid=(nl, nout // tn, d // tk),
        in_specs=[pl.BlockSpec((tk, nb), lambda l, j, k: (k, 0)),
                  pl.BlockSpec((1, tk, tn), lambda l, j, k: (l, k, j)),
                  pl.BlockSpec((1, 1, tn), lambda l, j, k: (l, 0, j))],
        out_specs=pl.BlockSpec((1, nb, tn), lambda l, j, k: (l, 0, j)),
        out_shape=jax.ShapeDtypeStruct((nl, nb, nout), F32),
        compiler_params=_cparams(("parallel", "parallel", "arbitrary"), 2 * tk * tn * 4),
    )(ct, mod_w, mod_b.reshape(nl, 1, nout))


def _inproj_kernel(x_ref, g_ref, mod_ref, w_ref, o_ref, h_ref):
    @pl.when(pl.program_id(1) == 0)
    def _():
        x = x_ref[...]
        h = _rms(x, g_ref[...]) * (1.0 + mod_ref[0, 1:2, :]) + mod_ref[0, 0:1, :]
        h_ref[...] = h.astype(BF16)

    o_ref[...] = _dot(h_ref[...], w_ref[0]).astype(o_ref.dtype)


def _inproj(x2, g, mod, w, layer, seq):
    n, d = x2.shape
    ncol = w.shape[2]
    tm, tn = 512, ncol // 4
    per_b = seq // tm
    est = 2 * tm * d * 4 + tm * d * 2 + 2 * d * tn * 2 + 2 * tm * tn * 2 + tm * tn * 4
    return pl.pallas_call(
        _inproj_kernel,
        grid=(n // tm, ncol // tn),
        in_specs=[pl.BlockSpec((tm, d), lambda i, j: (i, 0)),
                  pl.BlockSpec((1, d), lambda i, j: (0, 0)),
                  pl.BlockSpec((1, 6, d), lambda i, j: (i // per_b, 0, 0)),
                  pl.BlockSpec((1, d, tn), lambda i, j: (layer, 0, j))],
        out_specs=pl.BlockSpec((tm, tn), lambda i, j: (i, j)),
        out_shape=jax.ShapeDtypeStruct((n, ncol), BF16),
        scratch_shapes=[pltpu.VMEM((tm, d), BF16)],
        compiler_params=_cparams(("parallel", "arbitrary"), est),
    )(x2, g.reshape(1, d), mod, w)


def _rope128(y, tc, ts):
    return y * tc + pltpu.roll(y, 64, 1) * ts


def _mla_prep_kernel(ql_ref, kvl_ref, kr_ref, gq_ref, gkv_ref, wq_ref, wkv_ref, tc_ref, ts_ref,
                     q_ref, k_ref, v_ref):
    scale = (MLA_NOPE + MLA_ROPE) ** -0.5
    tc = tc_ref[...]
    ts = ts_ref[...]
    qn = _rms(ql_ref[...].astype(F32), gq_ref[...]).astype(BF16)
    yq = _dot(qn, wq_ref[...])
    kvn = _rms(kvl_ref[...].astype(F32), gkv_ref[...]).astype(BF16)
    ykv = _dot(kvn, wkv_ref[...])
    k_roped = _rope128(kr_ref[...].astype(F32), tc, ts).astype(BF16)
    for h in range(MLA_HEADS):
        o = 2 * LANES * h
        q_ref[:, o:o + LANES] = (yq[:, o:o + LANES] * scale).astype(BF16)
        q_ref[:, o + LANES:o + 2 * LANES] = (_rope128(yq[:, o + LANES:o + 2 * LANES], tc, ts) * scale).astype(BF16)
        k_ref[:, o:o + LANES] = ykv[:, LANES * h:LANES * (h + 1)].astype(BF16)
        k_ref[:, o + LANES:o + 2 * LANES] = k_roped
    v_ref[...] = ykv[:, MLA_HEADS * LANES:].astype(BF16)


def _mla_prep(proj, gq, gkv, wq, wkv, tc, ts, seq):
    n = proj.shape[0]
    tm = 512
    per_b = seq // tm
    hq = MLA_HEADS * 2 * LANES
    return pl.pallas_call(
        _mla_prep_kernel,
        grid=(n // tm,),
        in_specs=[pl.BlockSpec((tm, MLA_Q_RANK), lambda i: (i, P_QLAT // MLA_Q_RANK)),
                  pl.BlockSpec((tm, MLA_KV_RANK), lambda i: (i, P_KVLAT // MLA_KV_RANK)),
                  pl.BlockSpec((tm, LANES), lambda i: (i, P_KROPE // LANES)),
                  pl.BlockSpec((1, MLA_Q_RANK), lambda i: (0, 0)),
                  pl.BlockSpec((1, MLA_KV_RANK), lambda i: (0, 0)),
                  pl.BlockSpec((MLA_Q_RANK, hq), lambda i: (0, 0)),
                  pl.BlockSpec((MLA_KV_RANK, 2 * GROUP_WIDTH), lambda i: (0, 0)),
                  pl.BlockSpec((tm, LANES), lambda i: (i % per_b, 0)),
                  pl.BlockSpec((tm, LANES), lambda i: (i % per_b, 0))],
        out_specs=[pl.BlockSpec((tm, hq), lambda i: (i, 0)),
                   pl.BlockSpec((tm, hq), lambda i: (i, 0)),
                   pl.BlockSpec((tm, GROUP_WIDTH), lambda i: (i, 0))],
        out_shape=[jax.ShapeDtypeStruct((n, hq), BF16),
                   jax.ShapeDtypeStruct((n, hq), BF16),
                   jax.ShapeDtypeStruct((n, GROUP_WIDTH), BF16)],
        compiler_params=_cparams(("parallel",), 16 * 1024 * 1024),
    )(proj, proj, proj, gq.reshape(1, -1), gkv.reshape(1, -1), wq, wkv, tc, ts)


def _mla_flash_kernel(q_ref, k_ref, v_ref, o_ref):
    i = pl.program_id(2)
    tq = q_ref.shape[0]
    tk = tq
    q = q_ref[...]

    def step(j, carry, masked):
        m, l, acc = carry
        k = k_ref[pl.ds(pl.multiple_of(j * tk, tk), tk), :]
        v = v_ref[pl.ds(pl.multiple_of(j * tk, tk), tk), :]
        s = _dot_nt(q, k)
        if masked:
            qc = lax.broadcasted_iota(jnp.int32, (tq, tk), 0) // CHUNK
            kc = lax.broadcasted_iota(jnp.int32, (tq, tk), 1) // CHUNK
            s = jnp.where(kc <= qc, s, NEG_BIG)
        m_new = jnp.maximum(m, jnp.max(s, axis=-1, keepdims=True))
        alpha = jnp.exp(m - m_new)
        p = jnp.exp(s - m_new)
        l = alpha * l + jnp.sum(p, axis=-1, keepdims=True)
        acc = alpha * acc + _dot(p.astype(BF16), v)
        return m_new, l, acc

    init = (jnp.full((tq, 1), NEG_BIG, F32), jnp.zeros((tq, 1), F32), jnp.zeros((tq, v_ref.shape[1]), F32))
    carry = lax.fori_loop(0, i, lambda j, c: step(j, c, False), init)
    m, l, acc = step(i, carry, True)
    o_ref[...] = (acc / l).astype(o_ref.dtype)


def _mla_flash(q, k, v, batch, seq):
    n = q.shape[0]
    tq = ATT_TQ
    nq = seq // tq
    est = 2 * (seq * 2 * LANES * 2 + seq * LANES * 2) + 8 * tq * tq * 4
    return pl.pallas_call(
        _mla_flash_kernel,
        grid=(batch, MLA_HEADS, nq),
        in_specs=[pl.BlockSpec((tq, 2 * LANES), lambda b, h, i: (b * nq + i, h)),
                  pl.BlockSpec((seq, 2 * LANES), lambda b, h, i: (b, h)),
                  pl.BlockSpec((seq, LANES), lambda b, h, i: (b, h))],
        out_specs=pl.BlockSpec((tq, LANES), lambda b, h, i: (b * nq + i, h)),
        out_shape=jax.ShapeDtypeStruct((n, GROUP_WIDTH), BF16),
        compiler_params=_cparams(("parallel", "parallel", "arbitrary"), est),
    )(q, k, v)


def _sb_kernel(q_ref, k_ref, v_ref, o_ref):
    i = pl.program_id(2)
    tq = q_ref.shape[0]
    tk = tq
    sub = SB_SUB
    scale = q_ref.shape[1] ** -0.5
    q = q_ref[...]
    tri = (lax.broadcasted_iota(jnp.int32, (sub, sub), 0)
           > lax.broadcasted_iota(jnp.int32, (sub, sub), 1)).astype(BF16)

    def step(j, carry, masked):
        run, acc = carry
        k = k_ref[pl.ds(pl.multiple_of(j * tk, tk), tk), :]
        v = v_ref[pl.ds(pl.multiple_of(j * tk, tk), tk), :]
        u = _dot_nt(q, k) * (scale * LOG2_E)
        log_beta = jnp.minimum(u, 0.0) - jnp.log2(1.0 + jnp.exp2(-jnp.abs(u)))
        log_keep = log_beta - u
        if masked:
            vis = (lax.broadcasted_iota(jnp.int32, (tq, tk), 1)
                   < lax.broadcasted_iota(jnp.int32, (tq, tk), 0))
            log_keep = jnp.where(vis, log_keep, 0.0)
        keep_b = log_keep.astype(BF16)
        pieces = []
        for blk in range(tk // sub - 1, -1, -1):
            sl = slice(blk * sub, (blk + 1) * sub)
            pieces.append(_dot(keep_b[:, sl], tri) + run)
            run = run + jnp.sum(log_keep[:, sl], axis=-1, keepdims=True)
        later = jnp.concatenate(pieces[::-1], axis=1)
        att = jnp.exp2(log_beta + later)
        if masked:
            att = jnp.where(vis, att, 0.0)
        acc = acc + _dot(att.astype(BF16), v)
        return run, acc

    carry = step(i, (jnp.zeros((tq, 1), F32), jnp.zeros((tq, v_ref.shape[1]), F32)), True)
    run, acc = lax.fori_loop(0, i, lambda jj, c: step(i - 1 - jj, c, False), carry)
    o_ref[...] = acc.astype(o_ref.dtype)


def _sb_attention(proj, batch, seq):
    n = proj.shape[0]
    tq = ATT_TQ
    nq = seq // tq
    cq, ck, cv = P_C // LANES, (P_C + GROUP_WIDTH) // LANES, (P_C + 2 * GROUP_WIDTH) // LANES
    est = 4 * seq * LANES * 2 + 10 * tq * tq * 4
    return pl.pallas_call(
        _sb_kernel,
        grid=(batch, SB_HEADS, nq),
        in_specs=[pl.BlockSpec((tq, LANES), lambda b, h, i: (b * nq + i, cq + h)),
                  pl.BlockSpec((seq, LANES), lambda b, h, i: (b, ck + h)),
                  pl.BlockSpec((seq, LANES), lambda b, h, i: (b, cv + h))],
        out_specs=pl.BlockSpec((tq, LANES), lambda b, h, i: (b * nq + i, h)),
        out_shape=jax.ShapeDtypeStruct((n, GROUP_WIDTH), BF16),
        compiler_params=_cparams(("parallel", "parallel", "arbitrary"), est),
    )(proj, proj, proj)


def _ca_kernel(q_ref, k_ref, v_ref, tab_ref, o_ref, kpad_ref, vpad_ref):
    i = pl.program_id(2)
    tq = q_ref.shape[0]
    past = CA_PAST_CHUNKS * CHUNK
    win = tq + past
    scale = q_ref.shape[1] ** -0.5

    @pl.when(i == 0)
    def _():
        kpad_ref[0:past, :] = jnp.zeros((past, kpad_ref.shape[1]), kpad_ref.dtype)
        vpad_ref[0:past, :] = jnp.zeros((past, vpad_ref.shape[1]), vpad_ref.dtype)
        kpad_ref[past:, :] = k_ref[...]
        vpad_ref[past:, :] = v_ref[...]

    start = pl.multiple_of(i * tq, tq)
    kw = kpad_ref[pl.ds(start, win), :]
    vw = vpad_ref[pl.ds(start, win), :]
    s = _dot_nt(q_ref[...], kw) * scale + tab_ref[0, 0]
    m = jnp.max(s, axis=-1, keepdims=True)
    p = jnp.exp(s - m)
    l = jnp.sum(p, axis=-1, keepdims=True)
    o_ref[...] = (_dot(p.astype(BF16), vw) / l).astype(o_ref.dtype)


def _ca_table(rel_bias, tq):
    past = CA_PAST_CHUNKS * CHUNK
    win = tq + past
    r = jnp.arange(tq)[:, None]
    c = jnp.arange(win)[None, :]
    ring = 1024
    assert ring >= tq + win - 1
    m = jnp.arange(ring)
    m = jnp.where(m >= win, m - ring, m)
    idx = jnp.clip(past - m, -(CHUNK - 1), CA_REL_PAST) + (CHUNK - 1)
    ringvals = rel_bias[:, idx].astype(F32)
    nh = rel_bias.shape[0]
    bias = jnp.tile(ringvals, (1, tq))[:, :tq * (ring - 1)].reshape(nh, tq, ring - 1)[:, :, :win]
    qc = r // CHUNK
    kc = c // CHUNK
    vis = (kc >= qc) & (kc <= qc + CA_PAST_CHUNKS)
    tabs = []
    n_var = past // tq + 1
    for v in range(n_var):
        first = (past - v * tq) // CHUNK if v < n_var - 1 else 0
        ok = vis & (kc >= first)
        tabs.append(jnp.where(ok[None], bias, NEG_BIG))
    return jnp.stack(tabs)


def _ca_attention(proj, table, batch, seq):
    n = proj.shape[0]
    tq = CA_TQ
    nq = seq // tq
    past = CA_PAST_CHUNKS * CHUNK
    win = tq + past
    n_var = table.shape[0]
    cq, ck, cv = P_D // LANES, (P_D + GROUP_WIDTH) // LANES, (P_D + 2 * GROUP_WIDTH) // LANES
    est = 4 * seq * LANES * 2 + 2 * (seq + past) * LANES * 2 + 2 * tq * win * 4 + 6 * tq * win * 4
    return pl.pallas_call(
        _ca_kernel,
        grid=(batch, CA_HEADS, nq),
        in_specs=[pl.BlockSpec((tq, LANES), lambda b, h, i: (b * nq + i, cq + h)),
                  pl.BlockSpec((seq, LANES), lambda b, h, i: (b, ck + h)),
                  pl.BlockSpec((seq, LANES), lambda b, h, i: (b, cv + h)),
                  pl.BlockSpec((1, 1, tq, win), lambda b, h, i: (jnp.minimum(i, n_var - 1), h, 0, 0))],
        out_specs=pl.BlockSpec((tq, LANES), lambda b, h, i: (b * nq + i, h)),
        out_shape=jax.ShapeDtypeStruct((n, GROUP_WIDTH), BF16),
        scratch_shapes=[pltpu.VMEM((seq + past, LANES), BF16), pltpu.VMEM((seq + past, LANES), BF16)],
        compiler_params=_cparams(("parallel", "parallel", "arbitrary"), est),
    )(proj, proj, proj, table)


def _ssd_kernel(z_ref, xbc_ref, dt_ref, cw_ref, cb_ref, dtb_ref, alog_ref, dskip_ref, ng_ref, ex_ref,
                o_ref, xbuf_ref, state_ref, y_ref):
    c = pl.program_id(1)
    lc = z_ref.shape[0]
    halo = SUBLANES
    gw = GROUP_WIDTH // SSM_GROUPS
    hpg = SSM_HEADS // SSM_GROUPS

    @pl.when(c == 0)
    def _():
        xbuf_ref[0:halo, :] = jnp.zeros((halo, xbuf_ref.shape[1]), F32)
        state_ref[...] = jnp.zeros(state_ref.shape, F32)

    xbuf_ref[halo:halo + lc, :] = xbc_ref[...].astype(F32)
    acc = jnp.broadcast_to(cb_ref[...], (lc, SSM_CONV_DIM))
    for k in range(SSM_CONV):
        acc = acc + cw_ref[k:k + 1, :] * xbuf_ref[pl.ds(halo - (SSM_CONV - 1) + k, lc), :]
    xbuf_ref[0:halo, :] = xbuf_ref[lc:lc + halo, :]
    u = acc * jax.nn.sigmoid(acc)
    xs = u[:, :GROUP_WIDTH]
    bm = u[:, GROUP_WIDTH:GROUP_WIDTH + SSM_GROUPS * SSM_STATE].astype(BF16)
    cm = u[:, GROUP_WIDTH + SSM_GROUPS * SSM_STATE:].astype(BF16)

    dt = jax.nn.softplus(dt_ref[...].astype(F32) + dtb_ref[...])
    a = -jnp.exp(alog_ref[...])
    da = dt * a
    row = lax.broadcasted_iota(jnp.int32, (lc, lc), 0)
    col = lax.broadcasted_iota(jnp.int32, (lc, lc), 1)
    lower = col <= row
    cs = _dot3_left01(lower.astype(BF16), da)
    cs_last = cs[lc - 1:lc, :]
    cs_t = cs.T
    dt_t = dt.T
    ex = ex_ref[...]
    ecs_x = _dot3_right01(jnp.exp(cs), ex)
    wdec_x = _dot3_right01(jnp.exp(cs_last - cs) * dt, ex)
    xs_b = xs.astype(BF16)

    for g in range(SSM_GROUPS):
        bg = bm[:, g * SSM_STATE:(g + 1) * SSM_STATE]
        cg = cm[:, g * SSM_STATE:(g + 1) * SSM_STATE]
        cb = _dot_nt(cg, bg)
        for r in range(hpg):
            h = g * hpg + r
            diff = cs[:, h:h + 1] - cs_t[h:h + 1, :]
            wgt = cb * jnp.exp(jnp.where(lower, diff, NEG_BIG)) * dt_t[h:h + 1, :]
            y_ref[:, h * SSM_HEAD_DIM:(h + 1) * SSM_HEAD_DIM] = _dot(
                wgt.astype(BF16), xs_b[:, h * SSM_HEAD_DIM:(h + 1) * SSM_HEAD_DIM])
        lo, hi = g * gw, (g + 1) * gw
        st = state_ref[g]
        y_off = _dot(cg, st.astype(BF16)) * ecs_x[:, lo:hi]
        y_ref[:, lo:hi] += y_off
        xw = (xs[:, lo:hi] * wdec_x[:, lo:hi]).astype(BF16)
        state_ref[g] = st * ecs_x[lc - 1:lc, lo:hi] + _dot_tn(bg, xw)

    zf = z_ref[...].astype(F32)
    y = (y_ref[...] + xs * dskip_ref[...]) * (zf * jax.nn.sigmoid(zf))
    for g in range(SSM_GROUPS):
        lo, hi = g * gw, (g + 1) * gw
        yg = y[:, lo:hi]
        yg = yg * lax.rsqrt(jnp.mean(yg * yg, axis=-1, keepdims=True) + EPS)
        o_ref[:, lo:hi] = (yg * ng_ref[:, lo:hi]).astype(o_ref.dtype)


def _ssd(proj, conv_w, conv_b, dt_bias, a_log, d_skip, norm_g, batch, seq):
    n = proj.shape[0]
    lc = SSD_LC
    nc = seq // lc
    dtb = jnp.zeros((1, LANES), F32).at[0, :SSM_HEADS].set(dt_bias)
    alog = jnp.full((1, LANES), NEG_BIG, F32).at[0, :SSM_HEADS].set(a_log)
    dskip = jnp.repeat(d_skip, SSM_HEAD_DIM).reshape(1, GROUP_WIDTH)
    ex = (jnp.arange(LANES)[:, None] == jnp.arange(GROUP_WIDTH)[None, :] // SSM_HEAD_DIM).astype(BF16)
    full = lambda shape: pl.BlockSpec(shape, lambda b, c: tuple(0 for _ in shape))
    return pl.pallas_call(
        _ssd_kernel,
        grid=(batch, nc),
        in_specs=[pl.BlockSpec((lc, GROUP_WIDTH), lambda b, c: (b * nc + c, P_Z // GROUP_WIDTH)),
                  pl.BlockSpec((lc, SSM_CONV_DIM), lambda b, c: (b * nc + c, P_XBC // SSM_CONV_DIM)),
                  pl.BlockSpec((lc, LANES), lambda b, c: (b * nc + c, P_DT // LANES)),
                  full((SSM_CONV, SSM_CONV_DIM)), full((1, SSM_CONV_DIM)), full((1, LANES)), full((1, LANES)),
                  full((1, GROUP_WIDTH)), full((1, GROUP_WIDTH)), full((LANES, GROUP_WIDTH))],
        out_specs=pl.BlockSpec((lc, GROUP_WIDTH), lambda b, c: (b * nc + c, 0)),
        out_shape=jax.ShapeDtypeStruct((n, GROUP_WIDTH), BF16),
        scratch_shapes=[pltpu.VMEM((lc + SUBLANES, SSM_CONV_DIM), F32),
                        pltpu.VMEM((SSM_GROUPS, SSM_STATE, GROUP_WIDTH // SSM_GROUPS), F32),
                        pltpu.VMEM((lc, GROUP_WIDTH), F32)],
        compiler_params=_cparams(("parallel", "arbitrary"), 16 * 1024 * 1024),
    )(proj, proj, proj, conv_w, conv_b.reshape(1, -1), dtb, alog, dskip, norm_g.reshape(1, -1), ex)


def _outproj_kernel(a_ref, b_ref, c_ref, d_ref, x_ref, gmix_ref, w_ref, mod_ref, gf_ref, rw_ref, rb_ref,
                    xo_ref, h_ref, e_ref, gt_ref):
    gw = GROUP_WIDTH
    a = _rms(a_ref[...].astype(F32), gmix_ref[0:1, :]).astype(BF16)
    c = _rms(c_ref[...].astype(F32), gmix_ref[1:2, :]).astype(BF16)
    d = _rms(d_ref[...].astype(F32), gmix_ref[2:3, :]).astype(BF16)
    mixed = (_dot(a, w_ref[0:gw, :]) + _dot(b_ref[...], w_ref[gw:2 * gw, :])
             + _dot(c, w_ref[2 * gw:3 * gw, :]) + _dot(d, w_ref[3 * gw:, :]))
    x = x_ref[...] + mod_ref[0, 2:3, :] * mixed
    xo_ref[...] = x
    h = _rms(x, gf_ref[...]) * (1.0 + mod_ref[0, 4:5, :]) + mod_ref[0, 3:4, :]
    _store_packed_rows(h_ref, h)

    hh = h.astype(BF16)
    hl = (h - hh.astype(F32)).astype(BF16)
    rw = rw_ref[...]
    wh = rw.astype(BF16)
    wl = (rw - wh.astype(F32)).astype(BF16)
    logits = _dot(hh, wh) + _dot(hh, wl) + _dot(hl, wh) + rb_ref[...]
    lane = lax.broadcasted_iota(jnp.int32, logits.shape, 1)
    vals, idxs = [], []
    for _ in range(TOP_K):
        m = jnp.max(logits, axis=-1, keepdims=True)
        idx = jnp.min(jnp.where(logits == m, lane, LANES), axis=-1, keepdims=True)
        vals.append(m)
        idxs.append(idx)
        logits = jnp.where(lane == idx, -jnp.inf, logits)
    ex = [jnp.exp(v - vals[0]) for v in vals]
    tot = ex[0] + ex[1] + ex[2] + ex[3]
    e_out = jnp.zeros(logits.shape, jnp.int32)
    g_out = jnp.zeros(logits.shape, F32)
    for k in range(TOP_K):
        e_out = jnp.where(lane == k, idxs[k], e_out)
        g_out = jnp.where(lane == k, ex[k] / tot, g_out)
    e_ref[...] = e_out
    gt_ref[...] = g_out


def _outproj(oa, ob, oc, od, x2, gmix, w_out, mod, gf, rw, rb, seq):
    n, d = x2.shape
    tm = 256
    per_b = seq // tm
    gw = GROUP_WIDTH
    row = lambda width: pl.BlockSpec((tm, width), lambda i: (i, 0))
    full = lambda shape: pl.BlockSpec(shape, lambda i: tuple(0 for _ in shape))
    est = 2 * d * d * 2 + 4 * tm * d * 4 + 2 * tm * d * 2 + 8 * tm * gw * 2 + 6 * tm * d * 4
    return pl.pallas_call(
        _outproj_kernel,
        grid=(n // tm,),
        in_specs=[row(gw), row(gw), row(gw), row(gw), row(d), full((3, gw)), full((d, d)),
                  pl.BlockSpec((1, 6, d), lambda i: (i // per_b, 0, 0)), full((1, d)),
                  full((d, LANES)), full((1, LANES))],
        out_specs=[row(d), pl.BlockSpec((tm * ROW_SUBLANES, LANES), lambda i: (i, 0)), row(LANES), row(LANES)],
        out_shape=[jax.ShapeDtypeStruct((n, d), F32), jax.ShapeDtypeStruct((n * ROW_SUBLANES, LANES), U32),
                   jax.ShapeDtypeStruct((n, LANES), jnp.int32), jax.ShapeDtypeStruct((n, LANES), F32)],
        compiler_params=_cparams(("parallel",), est),
    )(oa, ob, oc, od, x2, gmix, w_out, mod, gf.reshape(1, d), rw, rb)


def _row_copy(src_ref, src_row, dst_ref, dst_row, sem):
    rs = ROW_SUBLANES
    return pltpu.make_async_copy(src_ref.at[pl.ds(pl.multiple_of(src_row * rs, rs), rs)],
                                 dst_ref.at[pl.ds(pl.multiple_of(dst_row * rs, rs), rs)], sem)


def _expert_kernel(be_ref, nu_ref, tok_ref, tok_next_ref, dst_prev_ref, dst_ref, hp_ref, w1_ref, b1_ref, w2_ref,
                   b2_ref, y_ref, xa_ref, xb_ref, ya_ref, yb_ref, gsem, ssem):
    i = pl.program_id(0)
    nu = nu_ref[0]
    tm = MOE_BLOCK
    rs = ROW_SUBLANES

    def start_rows(copy_of_row):
        for r in range(tm):
            copy_of_row(r).start(priority=r % 2)

    def wait_rows(src_ref, dst_ref, sem):
        for _ in range(tm):
            _row_copy(src_ref, 0, dst_ref, 0, sem).wait()

    @pl.when(i == 0)
    def _():
        start_rows(lambda r: _row_copy(hp_ref, tok_ref[0, 0, r], xa_ref, r, gsem.at[0]))
        yb_ref[...] = jnp.zeros(yb_ref.shape, U32)
        spare = pltpu.make_async_copy(yb_ref, y_ref.at[pl.ds(y_ref.shape[0] - 2 * tm * rs, tm * rs)], ssem.at[0])
        spare.start()
        spare.wait()

    def body(xcur, xnext, ycur, yprev, gcur, gnext, scur, sprev):
        wait_rows(hp_ref, xcur, gcur)
        start_rows(lambda r: _row_copy(hp_ref, tok_next_ref[0, 0, r], xnext, r, gnext))
        start_rows(lambda r: _row_copy(yprev, r, y_ref, dst_prev_ref[0, 0, r], sprev))
        los, his = [], []
        for s in range(rs):
            u = xcur[pl.ds(s, tm, stride=rs), :]
            los.append(_unpack_lo(u).astype(BF16))
            his.append(_unpack_hi(u).astype(BF16))
        x = jnp.concatenate(los + his, axis=1)
        cw = EXPERT_COL_CHUNK
        hu = jnp.concatenate([_dot(x, w1_ref[0, :, c:c + cw].astype(BF16)) for c in range(0, w1_ref.shape[2], cw)],
                             axis=1) + b1_ref[0]
        gate = jnp.minimum(hu[:, :D_EXPERT], SWIGLU_LIMIT)
        up = jnp.clip(hu[:, D_EXPERT:], -SWIGLU_LIMIT, SWIGLU_LIMIT)
        act = (gate * jax.nn.sigmoid(SWIGLU_ALPHA * gate) * (up + 1.0)).astype(BF16)
        y = jnp.concatenate([_dot(act, w2_ref[0, :, c:c + cw].astype(BF16)) for c in range(0, w2_ref.shape[2], cw)],
                            axis=1) + b2_ref[0]

        @pl.when(i > 0)
        def _():
            wait_rows(ycur, y_ref, scur)

        _store_packed_rows(ycur, y)

        @pl.when(i == nu - 1)
        def _():
            start_rows(lambda r: _row_copy(ycur, r, y_ref, dst_ref[0, 0, r], scur))
            wait_rows(yprev, y_ref, sprev)
            wait_rows(ycur, y_ref, scur)
            wait_rows(hp_ref, xnext, gnext)

    even = lax.rem(i, 2) == 0

    @pl.when(jnp.logical_and(i < nu, even))
    def _():
        body(xa_ref, xb_ref, ya_ref, yb_ref, gsem.at[0], gsem.at[1], ssem.at[0], ssem.at[1])

    @pl.when(jnp.logical_and(i < nu, jnp.logical_not(even)))
    def _():
        body(xb_ref, xa_ref, yb_ref, ya_ref, gsem.at[1], gsem.at[0], ssem.at[1], ssem.at[0])


def _experts(hp, row_tok, row_dst, block_e, n_used, n_slots, w1, b1, w2, b2, layer=0):
    tm = MOE_BLOCK
    nblk = row_tok.shape[0]
    d, de2 = w1.shape[1], w1.shape[2]
    de = w2.shape[1]
    rs = ROW_SUBLANES
    block_e = block_e + layer * N_EXPERTS
    last = lambda i, nu: jnp.minimum(i, nu[0] - 1)
    smem_rows = lambda f: pl.BlockSpec((1, 1, tm), f, memory_space=pltpu.SMEM)
    est = (2 * (d * de2 + de * d) * w1.dtype.itemsize + 4 * tm * rs * LANES * 4 + 4 * tm * d * 4
           + 3 * tm * de2 * 4)
    first_prev = (n_slots - tm + jnp.arange(tm, dtype=jnp.int32)).reshape(1, 1, tm)
    row_dst_prev = jnp.concatenate([first_prev, row_dst[:-1]], axis=0)
    grid_spec = pltpu.PrefetchScalarGridSpec(
        num_scalar_prefetch=2,
        grid=(nblk,),
        in_specs=[smem_rows(lambda i, be, nu: (last(i, nu), 0, 0)),
                  smem_rows(lambda i, be, nu: (last(i + 1, nu), 0, 0)),
                  smem_rows(lambda i, be, nu: (last(i, nu), 0, 0)),
                  smem_rows(lambda i, be, nu: (last(i, nu), 0, 0)),
                  pl.BlockSpec(memory_space=pl.ANY),
                  pl.BlockSpec((1, d, de2), lambda i, be, nu: (be[last(i, nu)], 0, 0)),
                  pl.BlockSpec((1, 1, de2), lambda i, be, nu: (be[last(i, nu)], 0, 0)),
                  pl.BlockSpec((1, de, d), lambda i, be, nu: (be[last(i, nu)], 0, 0)),
                  pl.BlockSpec((1, 1, d), lambda i, be, nu: (be[last(i, nu)], 0, 0))],
        out_specs=pl.BlockSpec(memory_space=pl.ANY),
        scratch_shapes=[pltpu.VMEM((tm * rs, LANES), U32) for _ in range(4)]
        + [pltpu.SemaphoreType.DMA((2,)), pltpu.SemaphoreType.DMA((2,))],
    )
    return pl.pallas_call(
        _expert_kernel,
        grid_spec=grid_spec,
        out_shape=jax.ShapeDtypeStruct((n_slots * rs, LANES), U32),
        compiler_params=_cparams(("arbitrary",), est),
    )(block_e, n_used, row_tok, row_tok, row_dst_prev, row_dst, hp, w1, b1.reshape(b1.shape[0], 1, de2), w2,
      b2.reshape(b2.shape[0], 1, d))


def _combine_kernel(y_ref, g_ref, x_ref, mod_ref, o_ref):
    tm, d = x_ref.shape
    half = d // 2
    g = g_ref[...]
    gk = [g[:, k:k + 1] for k in range(TOP_K)]
    for s in range(ROW_SUBLANES):
        lo = jnp.zeros((tm, LANES), F32)
        hi = jnp.zeros((tm, LANES), F32)
        for k in range(TOP_K):
            u = y_ref[pl.ds(k * ROW_SUBLANES + s, tm, stride=TOP_K * ROW_SUBLANES), :]
            lo = lo + _unpack_lo(u) * gk[k]
            hi = hi + _unpack_hi(u) * gk[k]
        a, b = LANES * s, half + LANES * s
        o_ref[:, a:a + LANES] = x_ref[:, a:a + LANES] + mod_ref[0, 5:6, a:a + LANES] * lo
        o_ref[:, b:b + LANES] = x_ref[:, b:b + LANES] + mod_ref[0, 5:6, b:b + LANES] * hi


def _combine(y4, gates, x2, mod, seq):
    n, d = x2.shape
    tm = 256
    per_b = seq // tm
    rows = tm * TOP_K * ROW_SUBLANES
    est = 2 * rows * LANES * 4 + 4 * tm * d * 4 + 2 * tm * LANES * 4
    return pl.pallas_call(
        _combine_kernel,
        grid=(n // tm,),
        in_specs=[pl.BlockSpec((rows, LANES), lambda i: (i, 0)),
                  pl.BlockSpec((tm, LANES), lambda i: (i, 0)),
                  pl.BlockSpec((tm, d), lambda i: (i, 0)),
                  pl.BlockSpec((1, 6, d), lambda i: (i // per_b, 0, 0))],
        out_specs=pl.BlockSpec((tm, d), lambda i: (i, 0)),
        out_shape=jax.ShapeDtypeStruct((n, d), F32),
        compiler_params=_cparams(("parallel",), est),
    )(y4, gates, x2, mod)


def _final_norm_kernel(x_ref, g_ref, o_ref):
    o_ref[...] = _rms(x_ref[...], g_ref[...])


def _final_norm(x2, g):
    n, d = x2.shape
    tm = 512
    return pl.pallas_call(
        _final_norm_kernel,
        grid=(n // tm,),
        in_specs=[pl.BlockSpec((tm, d), lambda i: (i, 0)), pl.BlockSpec((1, d), lambda i: (0, 0))],
        out_specs=pl.BlockSpec((tm, d), lambda i: (i, 0)),
        out_shape=jax.ShapeDtypeStruct((n, d), F32),
        compiler_params=_cparams(("parallel",), 4 * tm * d * 4),
    )(x2, g.reshape(1, d))


def _swap_halves(w):
    half = w.shape[-1] // 2
    return jnp.concatenate([w[..., half:], w[..., :half]], axis=-1)


def _pack_w_in_kernel(w_ref, o_ref):
    w = w_ref[0]
    b0, c0 = A_COLS, A_COLS + B_COLS
    kr0 = MLA_Q_RANK + MLA_KV_RANK
    half = MLA_ROPE // 2
    pieces = [w[:, :MLA_Q_RANK],
              w[:, b0:b0 + GROUP_WIDTH + SSM_CONV_DIM],
              w[:, MLA_Q_RANK:kr0 + MLA_ROPE],
              w[:, kr0 + half:kr0 + MLA_ROPE], w[:, kr0:kr0 + half],
              w[:, c0 - SSM_HEADS:c0], jnp.zeros((w.shape[0], LANES - SSM_HEADS), w.dtype),
              w[:, c0:]]
    o_ref[0] = jnp.concatenate(pieces, axis=1).astype(BF16)


def _pack_w_in(w_in):
    nl, d, nc = w_in.shape
    tr = 256
    est = 2 * tr * nc * 4 + 2 * tr * P_COLS * 2 + 2 * tr * P_COLS * 4
    return pl.pallas_call(
        _pack_w_in_kernel,
        grid=(nl, d // tr),
        in_specs=[pl.BlockSpec((1, tr, nc), lambda l, i: (l, i, 0))],
        out_specs=pl.BlockSpec((1, tr, P_COLS), lambda l, i: (l, i, 0)),
        out_shape=jax.ShapeDtypeStruct((nl, d, P_COLS), BF16),
        compiler_params=_cparams(("parallel", "parallel"), est),
    )(w_in)


def _pack_wq(wq):
    r = wq.shape[0]
    w = wq.reshape(r, MLA_HEADS, MLA_NOPE + MLA_ROPE)
    rope = w[..., MLA_NOPE:]
    return jnp.concatenate([w[..., :MLA_NOPE], rope, _swap_halves(rope)], axis=-1).reshape(r, -1).astype(BF16)


def _pack_wkv(wkv):
    r = wkv.shape[0]
    w = wkv.reshape(r, MLA_HEADS, MLA_NOPE + MLA_V)
    return jnp.concatenate([w[..., :MLA_NOPE].reshape(r, -1), w[..., MLA_NOPE:].reshape(r, -1)], axis=-1).astype(BF16)


def _rope_tables(seq):
    inv = 1.0 / (ROPE_THETA ** (jnp.arange(0, MLA_ROPE, 2, dtype=F32) / MLA_ROPE))
    ang = jnp.arange(seq, dtype=F32)[:, None] * inv[None, :]
    cos, sin = jnp.cos(ang), jnp.sin(ang)
    zero = jnp.zeros((seq, LANES - MLA_ROPE), F32)
    return jnp.concatenate([cos, cos, zero], axis=1), jnp.concatenate([-sin, sin, zero], axis=1)


def _route(top_e, n_tok):
    n_assign = n_tok * TOP_K
    n_blocks = -(-(n_assign + N_EXPERTS * (MOE_BLOCK - 1)) // MOE_BLOCK)
    n_rows = n_blocks * MOE_BLOCK
    dest, meta = _route_dest(top_e)
    padded_end = meta[2, :N_EXPERTS]
    block_start = jnp.arange(n_blocks, dtype=jnp.int32) * MOE_BLOCK
    block_e = jnp.minimum(jnp.sum((padded_end[None, :] <= block_start[:, None]).astype(jnp.int32), axis=1),
                          N_EXPERTS - 1)
    n_used = (padded_end[N_EXPERTS - 1] // MOE_BLOCK).reshape(1)
    row_assign = _route_invert(dest[:, :TOP_K].reshape(n_assign), n_rows)
    real = row_assign >= 0
    row = jnp.arange(n_rows, dtype=jnp.int32)
    row_tok = jnp.where(real, row_assign // TOP_K, 0)
    spare = n_assign + (row // MOE_BLOCK % 2) * MOE_BLOCK + row % MOE_BLOCK
    row_dst = jnp.where(real, row_assign, spare)
    shape = (n_blocks, 1, MOE_BLOCK)
    return row_tok.reshape(shape), row_dst.reshape(shape), block_e, n_used, n_assign + 2 * MOE_BLOCK


ROUTE_TM = 512


def _route_dest_kernel(e_ref, dest_ref, meta_ref, carry_ref, base_ref):
    p = pl.program_id(0)
    i = pl.program_id(1)
    tm = e_ref.shape[0]
    e = e_ref[...]
    lane = lax.broadcasted_iota(jnp.int32, (tm, LANES), 1)
    hit = [lane == e[:, k:k + 1] for k in range(TOP_K)]
    tot = sum(h.astype(F32) for h in hit)

    @pl.when(jnp.logical_and(p == 0, i == 0))
    def _():
        carry_ref[...] = jnp.zeros(carry_ref.shape, F32)
        meta_ref[...] = jnp.zeros(meta_ref.shape, jnp.int32)

    @pl.when(jnp.logical_and(p == 0, i == pl.num_programs(1) - 1))
    def _():
        counts = carry_ref[...] + jnp.sum(tot, axis=0, keepdims=True)
        padded = jnp.floor((counts + (MOE_BLOCK - 1)) * (1.0 / MOE_BLOCK)) * MOE_BLOCK
        upper = (lax.broadcasted_iota(jnp.int32, (LANES, LANES), 0)
                 <= lax.broadcasted_iota(jnp.int32, (LANES, LANES), 1)).astype(BF16)
        padded_end = _dot3_right01(jnp.broadcast_to(padded, (SUBLANES, LANES)), upper)[0:1]
        base_ref[...] = padded_end - padded
        meta_ref[0:1, :] = counts.astype(jnp.int32)
        meta_ref[1:2, :] = (padded_end - padded).astype(jnp.int32)
        meta_ref[2:3, :] = padded_end.astype(jnp.int32)
        carry_ref[...] = -jnp.sum(tot, axis=0, keepdims=True)

    @pl.when(p == 1)
    def _():
        earlier = (lax.broadcasted_iota(jnp.int32, (tm, tm), 1)
                   < lax.broadcasted_iota(jnp.int32, (tm, tm), 0)).astype(BF16)
        pos = _dot(earlier, tot.astype(BF16)) + carry_ref[...] + base_ref[...]
        dest = jnp.zeros((tm, LANES), jnp.int32)
        for k in range(TOP_K):
            dk = jnp.sum(jnp.where(hit[k], pos, 0.0), axis=-1, keepdims=True)
            dest = jnp.where(lane == k, dk.astype(jnp.int32), dest)
        dest_ref[...] = dest

    carry_ref[...] += jnp.sum(tot, axis=0, keepdims=True)


def _route_dest(top_e):
    n = top_e.shape[0]
    tm = ROUTE_TM
    return pl.pallas_call(
        _route_dest_kernel,
        grid=(2, n // tm),
        in_specs=[pl.BlockSpec((tm, LANES), lambda p, i: (i, 0))],
        out_specs=[pl.BlockSpec((tm, LANES), lambda p, i: (i * p, 0)),
                   pl.BlockSpec((SUBLANES, LANES), lambda p, i: (0, 0))],
        out_shape=[jax.ShapeDtypeStruct((n, LANES), jnp.int32), jax.ShapeDtypeStruct((SUBLANES, LANES), jnp.int32)],
        scratch_shapes=[pltpu.VMEM((1, LANES), F32), pltpu.VMEM((1, LANES), F32)],
        compiler_params=_cparams(("arbitrary", "arbitrary"), 8 * tm * tm * 4),
    )(top_e)


ROUTE_CHUNK = 2048


def _route_invert_kernel(dest_ref, out_ref):
    s = pl.program_id(0)
    ch = dest_ref.shape[2]
    n_clear = out_ref.shape[0] // ch
    per = SUBLANES

    @pl.when(s < n_clear)
    def _():
        def clear(j, carry):
            for u in range(per):
                out_ref[s * ch + j * per + u] = -1
            return carry
        lax.fori_loop(0, ch // per, clear, 0)

    @pl.when(s >= n_clear)
    def _():
        base = (s - n_clear) * ch

        def place(j, carry):
            for u in range(per):
                a = j * per + u
                out_ref[dest_ref[0, 0, a]] = base + a
            return carry
        lax.fori_loop(0, ch // per, place, 0)


def _route_invert(dest_flat, n_rows):
    ch = ROUTE_CHUNK
    n_clear, n_place = n_rows // ch, dest_flat.shape[0] // ch
    assert n_clear * ch == n_rows and n_place * ch == dest_flat.shape[0]
    return pl.pallas_call(
        _route_invert_kernel,
        grid=(n_clear + n_place,),
        in_specs=[pl.BlockSpec((1, 1, ch), lambda s: (jnp.maximum(s - n_clear, 0), 0, 0), memory_space=pltpu.SMEM)],
        out_specs=pl.BlockSpec(memory_space=pltpu.SMEM),
        out_shape=jax.ShapeDtypeStruct((n_rows,), jnp.int32),
        compiler_params=pltpu.CompilerParams(dimension_semantics=("arbitrary",)),
    )(dest_flat.reshape(n_place, 1, ch))


def kernel(x, c, attn_norm, ffn_norm, mod_w, mod_b, w_in, mla_q_norm, mla_w_q_up, mla_kv_norm, mla_w_kv_up,
           ssm_conv_w, ssm_conv_b, ssm_dt_bias, ssm_a_log, ssm_d, ssm_norm, ca_rel_bias, mix_out_norm, w_out,
           router_w, router_b, moe_w1, moe_b1, moe_w2, moe_b2, final_norm):
    batch, seq, d = x.shape
    n = batch * seq
    depth = w_in.shape[0]
    x2 = x.reshape(n, d)
    mod_all = _mod_all(c, mod_w, mod_b).reshape(depth, batch, 6, d)
    tc, ts = _rope_tables(seq)
    w_in_packed = _pack_w_in(w_in)
    for l in range(depth):
        mod = mod_all[l]
        proj = _inproj(x2, attn_norm[l], mod, w_in_packed, l, seq)
        q, k, v = _mla_prep(proj, mla_q_norm[l], mla_kv_norm[l], _pack_wq(mla_w_q_up[l]),
                            _pack_wkv(mla_w_kv_up[l]), tc, ts, seq)
        out_a = _mla_flash(q, k, v, batch, seq)
        out_b = _ssd(proj, ssm_conv_w[l], ssm_conv_b[l], ssm_dt_bias[l], ssm_a_log[l], ssm_d[l], ssm_norm[l],
                     batch, seq)
        out_c = _sb_attention(proj, batch, seq)
        out_d = _ca_attention(proj, _ca_table(ca_rel_bias[l], CA_TQ), batch, seq)
        rw = jnp.zeros((d, LANES), F32).at[:, :N_EXPERTS].set(router_w[l])
        rb = jnp.full((1, LANES), NEG_BIG, F32).at[0, :N_EXPERTS].set(router_b[l])
        x2, hp, top_e, gates = _outproj(out_a, out_b, out_c, out_d, x2, mix_out_norm[l], w_out[l].astype(BF16),
                                        mod, ffn_norm[l], rw, rb, seq)
        row_tok, row_dst, block_e, n_used, n_slots = _route(top_e, n)
        y4 = _experts(hp, row_tok, row_dst, block_e, n_used, n_slots,
                      moe_w1.reshape(-1, *moe_w1.shape[2:]), moe_b1.reshape(-1, moe_b1.shape[2]),
                      moe_w2.reshape(-1, *moe_w2.shape[2:]), moe_b2.reshape(-1, moe_b2.shape[2]), layer=l)
        x2 = _combine(y4, gates, x2, mod, seq)
    return _final_norm(x2, final_norm).reshape(batch, seq, d)
```

```python
import functools
import math

import jax
import jax.numpy as jnp
from jax import lax
from jax.experimental import pallas as pl
from jax.experimental.pallas import tpu as pltpu

F32 = jnp.float32
BF16 = jnp.bfloat16
U32 = jnp.uint32

D_MODEL = 2048
DEPTH = 4
CHUNK = 64
EPS = 1e-6
GROUP_WIDTH = 512
MLA_HEADS = 4
MLA_NOPE = 128
MLA_ROPE = 64
MLA_V = 128
MLA_Q_RANK = 512
MLA_KV_RANK = 256
ROPE_THETA = 10000.0
SSM_HEAD_DIM = 64
SSM_HEADS = 8
SSM_GROUPS = 2
SSM_STATE = 128
SSM_CONV = 4
SSM_CONV_DIM = 1024
SB_HEADS = 4
CA_HEADS = 4
CA_PAST_CHUNKS = 8
CA_REL_PAST = 256
N_EXPERTS = 32
TOP_K = 4
D_EXPERT = 768
SWIGLU_ALPHA = 1.702
SWIGLU_LIMIT = 7.0

A_COLS = MLA_Q_RANK + MLA_KV_RANK + MLA_ROPE
B_COLS = GROUP_WIDTH + SSM_CONV_DIM + SSM_HEADS
C_COLS = 3 * GROUP_WIDTH

LANES = 128
SUBLANES = 8
VMEM_BYTES_V7X = 64 * 1024 * 1024
NEG_BIG = -1e30
LOG2_E = math.log2(math.e)

P_QLAT = 0
P_Z = 512
P_XBC = 1024
P_KVLAT = 2048
P_KROPE = 2304
P_DT = 2432
P_C = 2560
P_D = 4096
P_COLS = 5632

MOE_BLOCK = 256
EXPERT_COL_CHUNK = 256
ATT_TQ = 512
SB_SUB = 256
ROW_SUBLANES = 8
CA_TQ = 256
CA_WIN = CA_TQ + CA_PAST_CHUNKS * CHUNK
SSD_LC = 256


def _vmem_limit(nbytes):
    return int(min(max(2 * nbytes, 16 * 1024 * 1024), VMEM_BYTES_V7X - 8 * 1024 * 1024))


def _cparams(sem, nbytes):
    return pltpu.CompilerParams(dimension_semantics=sem, vmem_limit_bytes=_vmem_limit(nbytes))


def _rms(x, g):
    return x * lax.rsqrt(jnp.mean(x * x, axis=-1, keepdims=True) + EPS) * g


def _split3(x):
    hi = x.astype(BF16)
    r1 = x - hi.astype(F32)
    mid = r1.astype(BF16)
    lo = (r1 - mid.astype(F32)).astype(BF16)
    return hi, mid, lo


def _dot(a, b):
    return jnp.dot(a, b, preferred_element_type=F32)


def _dot_nt(a, b):
    return lax.dot_general(a, b, (((1,), (1,)), ((), ())), preferred_element_type=F32)


def _dot_tn(a, b):
    return lax.dot_general(a, b, (((0,), (0,)), ((), ())), preferred_element_type=F32)


def _dot3_left01(t01, x):
    hi, mid, lo = _split3(x)
    return _dot(t01, hi) + _dot(t01, mid) + _dot(t01, lo)


def _dot3_right01(x, t01):
    hi, mid, lo = _split3(x)
    return _dot(hi, t01) + _dot(mid, t01) + _dot(lo, t01)


def _bf16_bits(x):
    u = lax.bitcast_convert_type(x, U32)
    return (u + jnp.uint32(0x7FFF) + ((u >> 16) & jnp.uint32(1))) >> 16


def _store_packed_rows(ref, y):
    tm, d = y.shape
    half = d // 2
    rs = ROW_SUBLANES
    words = []
    for s in range(rs):
        lo = _bf16_bits(y[:, LANES * s:LANES * (s + 1)])
        hi = _bf16_bits(y[:, half + LANES * s:half + LANES * (s + 1)])
        words.append(((hi << 16) | lo).reshape(tm // rs, rs, LANES))
    ref[...] = jnp.swapaxes(jnp.stack(words, axis=1), 1, 2).reshape(tm * rs, LANES)


def _load_packed_rows(ref):
    rs = ROW_SUBLANES
    tm = ref.shape[0] // rs
    w = jnp.swapaxes(ref[...].reshape(tm // rs, rs, rs, LANES), 1, 2)
    return [w[:, s].reshape(tm, LANES) for s in range(rs)]


def _unpack_lo(u):
    return lax.bitcast_convert_type(u << 16, F32)


def _unpack_hi(u):
    return lax.bitcast_convert_type(u & jnp.uint32(0xFFFF0000), F32)


def _mod_kernel(ct_ref, w_ref, b_ref, o_ref):
    k = pl.program_id(2)
    nb = o_ref.shape[1]

    @pl.when(k == 0)
    def _():
        o_ref[0] = jnp.broadcast_to(b_ref[0], o_ref.shape[1:])

    ct = ct_ref[...]
    cond = ct * jax.nn.sigmoid(ct)
    w = w_ref[0]
    for b in range(nb):
        o_ref[0, b:b + 1, :] += jnp.sum(w * cond[:, b:b + 1], axis=0, keepdims=True)


def _mod_all(c, mod_w, mod_b):
    nb, d = c.shape
    nl, _, nout = mod_w.shape
    tk, tn = 512, 2048
    ct = c.T
    return pl.pallas_call(
        _mod_kernel,
        grid=(nl, nout // tn, d // tk),
        in_specs=[pl.BlockSpec((tk, nb), lambda l, j, k: (k, 0)),
                  pl.BlockSpec((1, tk, tn), lambda l, j, k: (l, k, j)),
                  pl.BlockSpec((1, 1, tn), lambda l, j, k: (l, 0, j))],
        out_specs=pl.BlockSpec((1, nb, tn), lambda l, j, k: (l, 0, j)),
        out_shape=jax.ShapeDtypeStruct((nl, nb, nout), F32),
        compiler_params=_cparams(("parallel", "parallel", "arbitrary"), 2 * tk * tn * 4),
    )(ct, mod_w, mod_b.reshape(nl, 1, nout))


def _inproj_kernel(x_ref, g_ref, mod_ref, w_ref, o_ref, h_ref):
    @pl.when(pl.program_id(1) == 0)
    def _():
        x = x_ref[...]
        h = _rms(x, g_ref[...]) * (1.0 + mod_ref[0, 1:2, :]) + mod_ref[0, 0:1, :]
        h_ref[...] = h.astype(BF16)

    o_ref[...] = _dot(h_ref[...], w_ref[0]).astype(o_ref.dtype)


def _inproj(x2, g, mod, w, layer, seq):
    n, d = x2.shape
    ncol = w.shape[2]
    tm, tn = 512, ncol // 4
    per_b = seq // tm
    est = 2 * tm * d * 4 + tm * d * 2 + 2 * d * tn * 2 + 2 * tm * tn * 2 + tm * tn * 4
    return pl.pallas_call(
        _inproj_kernel,
        grid=(n // tm, ncol // tn),
        in_specs=[pl.BlockSpec((tm, d), lambda i, j: (i, 0)),
                  pl.BlockSpec((1, d), lambda i, j: (0, 0)),
                  pl.BlockSpec((1, 6, d), lambda i, j: (i // per_b, 0, 0)),
                  pl.BlockSpec((1, d, tn), lambda i, j: (layer, 0, j))],
        out_specs=pl.BlockSpec((tm, tn), lambda i, j: (i, j)),
        out_shape=jax.ShapeDtypeStruct((n, ncol), BF16),
        scratch_shapes=[pltpu.VMEM((tm, d), BF16)],
        compiler_params=_cparams(("parallel", "arbitrary"), est),
    )(x2, g.reshape(1, d), mod, w)


def _rope128(y, tc, ts):
    return y * tc + pltpu.roll(y, 64, 1) * ts


def _mla_prep_kernel(ql_ref, kvl_ref, kr_ref, gq_ref, gkv_ref, wq_ref, wkv_ref, tc_ref, ts_ref,
                     q_ref, k_ref, v_ref):
    scale = (MLA_NOPE + MLA_ROPE) ** -0.5
    tc = tc_ref[...]
    ts = ts_ref[...]
    qn = _rms(ql_ref[...].astype(F32), gq_ref[...]).astype(BF16)
    yq = _dot(qn, wq_ref[...])
    kvn = _rms(kvl_ref[...].astype(F32), gkv_ref[...]).astype(BF16)
    ykv = _dot(kvn, wkv_ref[...])
    k_roped = _rope128(kr_ref[...].astype(F32), tc, ts).astype(BF16)
    for h in range(MLA_HEADS):
        o = 2 * LANES * h
        q_ref[:, o:o + LANES] = (yq[:, o:o + LANES] * scale).astype(BF16)
        q_ref[:, o + LANES:o + 2 * LANES] = (_rope128(yq[:, o + LANES:o + 2 * LANES], tc, ts) * scale).astype(BF16)
        k_ref[:, o:o + LANES] = ykv[:, LANES * h:LANES * (h + 1)].astype(BF16)
        k_ref[:, o + LANES:o + 2 * LANES] = k_roped
    v_ref[...] = ykv[:, MLA_HEADS * LANES:].astype(BF16)


def _mla_prep(proj, gq, gkv, wq, wkv, tc, ts, seq):
    n = proj.shape[0]
    tm = 512
    per_b = seq // tm
    hq = MLA_HEADS * 2 * LANES
    return pl.pallas_call(
        _mla_prep_kernel,
        grid=(n // tm,),
        in_specs=[pl.BlockSpec((tm, MLA_Q_RANK), lambda i: (i, P_QLAT // MLA_Q_RANK)),
                  pl.BlockSpec((tm, MLA_KV_RANK), lambda i: (i, P_KVLAT // MLA_KV_RANK)),
                  pl.BlockSpec((tm, LANES), lambda i: (i, P_KROPE // LANES)),
                  pl.BlockSpec((1, MLA_Q_RANK), lambda i: (0, 0)),
                  pl.BlockSpec((1, MLA_KV_RANK), lambda i: (0, 0)),
                  pl.BlockSpec((MLA_Q_RANK, hq), lambda i: (0, 0)),
                  pl.BlockSpec((MLA_KV_RANK, 2 * GROUP_WIDTH), lambda i: (0, 0)),
                  pl.BlockSpec((tm, LANES), lambda i: (i % per_b, 0)),
                  pl.BlockSpec((tm, LANES), lambda i: (i % per_b, 0))],
        out_specs=[pl.BlockSpec((tm, hq), lambda i: (i, 0)),
                   pl.BlockSpec((tm, hq), lambda i: (i, 0)),
                   pl.BlockSpec((tm, GROUP_WIDTH), lambda i: (i, 0))],
        out_shape=[jax.ShapeDtypeStruct((n, hq), BF16),
                   jax.ShapeDtypeStruct((n, hq), BF16),
                   jax.ShapeDtypeStruct((n, GROUP_WIDTH), BF16)],
        compiler_params=_cparams(("parallel",), 16 * 1024 * 1024),
    )(proj, proj, proj, gq.reshape(1, -1), gkv.reshape(1, -1), wq, wkv, tc, ts)


def _mla_flash_kernel(q_ref, k_ref, v_ref, o_ref):
    i = pl.program_id(2)
    tq = q_ref.shape[0]
    tk = tq
    q = q_ref[...]

    def step(j, carry, masked):
        m, l, acc = carry
        k = k_ref[pl.ds(pl.multiple_of(j * tk, tk), tk), :]
        v = v_ref[pl.ds(pl.multiple_of(j * tk, tk), tk), :]
        s = _dot_nt(q, k)
        if masked:
            qc = lax.broadcasted_iota(jnp.int32, (tq, tk), 0) // CHUNK
            kc = lax.broadcasted_iota(jnp.int32, (tq, tk), 1) // CHUNK
            s = jnp.where(kc <= qc, s, NEG_BIG)
        m_new = jnp.maximum(m, jnp.max(s, axis=-1, keepdims=True))
        alpha = jnp.exp(m - m_new)
        p = jnp.exp(s - m_new)
        l = alpha * l + jnp.sum(p, axis=-1, keepdims=True)
        acc = alpha * acc + _dot(p.astype(BF16), v)
        return m_new, l, acc

    init = (jnp.full((tq, 1), NEG_BIG, F32), jnp.zeros((tq, 1), F32), jnp.zeros((tq, v_ref.shape[1]), F32))
    carry = lax.fori_loop(0, i // 2, lambda jj, c: step(2 * jj + 1, step(2 * jj, c, False), False), init)
    carry = lax.cond(lax.rem(i, 2) == 1, lambda c: step(i - 1, c, False), lambda c: c, carry)
    m, l, acc = step(i, carry, True)
    o_ref[...] = (acc / l).astype(o_ref.dtype)


def _mla_flash(q, k, v, batch, seq):
    n = q.shape[0]
    tq = ATT_TQ
    nq = seq // tq
    est = 2 * (seq * 2 * LANES * 2 + seq * LANES * 2) + 8 * tq * tq * 4
    return pl.pallas_call(
        _mla_flash_kernel,
        grid=(batch, MLA_HEADS, nq),
        in_specs=[pl.BlockSpec((tq, 2 * LANES), lambda b, h, i: (b * nq + i, h)),
                  pl.BlockSpec((seq, 2 * LANES), lambda b, h, i: (b, h)),
                  pl.BlockSpec((seq, LANES), lambda b, h, i: (b, h))],
        out_specs=pl.BlockSpec((tq, LANES), lambda b, h, i: (b * nq + i, h)),
        out_shape=jax.ShapeDtypeStruct((n, GROUP_WIDTH), BF16),
        compiler_params=_cparams(("parallel", "parallel", "arbitrary"), est),
    )(q, k, v)


def _sb_kernel(q_ref, k_ref, v_ref, o_ref):
    i = pl.program_id(2)
    tq = q_ref.shape[0]
    tk = tq
    sub = SB_SUB
    scale = q_ref.shape[1] ** -0.5
    q = q_ref[...]
    tri = (lax.broadcasted_iota(jnp.int32, (sub, sub), 0)
           > lax.broadcasted_iota(jnp.int32, (sub, sub), 1)).astype(BF16)

    def step(j, carry, masked):
        run, acc = carry
        k = k_ref[pl.ds(pl.multiple_of(j * tk, tk), tk), :]
        v = v_ref[pl.ds(pl.multiple_of(j * tk, tk), tk), :]
        u = _dot_nt(q, k) * (scale * LOG2_E)
        log_beta = jnp.minimum(u, 0.0) - jnp.log2(1.0 + jnp.exp2(-jnp.abs(u)))
        log_keep = log_beta - u
        if masked:
            vis = (lax.broadcasted_iota(jnp.int32, (tq, tk), 1)
                   < lax.broadcasted_iota(jnp.int32, (tq, tk), 0))
            log_keep = jnp.where(vis, log_keep, 0.0)
        keep_b = log_keep.astype(BF16)
        pieces = []
        for blk in range(tk // sub - 1, -1, -1):
            sl = slice(blk * sub, (blk + 1) * sub)
            pieces.append(_dot(keep_b[:, sl], tri) + run)
            run = run + jnp.sum(log_keep[:, sl], axis=-1, keepdims=True)
        later = jnp.concatenate(pieces[::-1], axis=1)
        att = jnp.exp2(log_beta + later)
        if masked:
            att = jnp.where(vis, att, 0.0)
        acc = acc + _dot(att.astype(BF16), v)
        return run, acc

    carry = step(i, (jnp.zeros((tq, 1), F32), jnp.zeros((tq, v_ref.shape[1]), F32)), True)
    carry = lax.fori_loop(0, i // 2, lambda jj, c: step(i - 2 - 2 * jj, step(i - 1 - 2 * jj, c, False), False), carry)
    run, acc = lax.cond(lax.rem(i, 2) == 1, lambda c: step(0, c, False), lambda c: c, carry)
    o_ref[...] = acc.astype(o_ref.dtype)


def _sb_attention(proj, batch, seq):
    n = proj.shape[0]
    tq = ATT_TQ
    nq = seq // tq
    cq, ck, cv = P_C // LANES, (P_C + GROUP_WIDTH) // LANES, (P_C + 2 * GROUP_WIDTH) // LANES
    est = 4 * seq * LANES * 2 + 10 * tq * tq * 4
    return pl.pallas_call(
        _sb_kernel,
        grid=(batch, SB_HEADS, nq),
        in_specs=[pl.BlockSpec((tq, LANES), lambda b, h, i: (b * nq + i, cq + h)),
                  pl.BlockSpec((seq, LANES), lambda b, h, i: (b, ck + h)),
                  pl.BlockSpec((seq, LANES), lambda b, h, i: (b, cv + h))],
        out_specs=pl.BlockSpec((tq, LANES), lambda b, h, i: (b * nq + i, h)),
        out_shape=jax.ShapeDtypeStruct((n, GROUP_WIDTH), BF16),
        compiler_params=_cparams(("parallel", "parallel", "arbitrary"), est),
    )(proj, proj, proj)


def _ca_kernel(q_ref, k_ref, v_ref, tab_ref, o_ref, kpad_ref, vpad_ref):
    i = pl.program_id(2)
    tq = q_ref.shape[0]
    past = CA_PAST_CHUNKS * CHUNK
    win = tq + past
    scale = q_ref.shape[1] ** -0.5

    @pl.when(i == 0)
    def _():
        kpad_ref[0:past, :] = jnp.zeros((past, kpad_ref.shape[1]), kpad_ref.dtype)
        vpad_ref[0:past, :] = jnp.zeros((past, vpad_ref.shape[1]), vpad_ref.dtype)
        kpad_ref[past:, :] = k_ref[...]
        vpad_ref[past:, :] = v_ref[...]

    start = pl.multiple_of(i * tq, tq)
    kw = kpad_ref[pl.ds(start, win), :]
    vw = vpad_ref[pl.ds(start, win), :]
    s = _dot_nt(q_ref[...], kw) * scale + tab_ref[0, 0]
    m = jnp.max(s, axis=-1, keepdims=True)
    p = jnp.exp(s - m)
    l = jnp.sum(p, axis=-1, keepdims=True)
    o_ref[...] = (_dot(p.astype(BF16), vw) / l).astype(o_ref.dtype)


def _ca_table(rel_bias, tq):
    past = CA_PAST_CHUNKS * CHUNK
    win = tq + past
    r = jnp.arange(tq)[:, None]
    c = jnp.arange(win)[None, :]
    ring = 1024
    assert ring >= tq + win - 1
    m = jnp.arange(ring)
    m = jnp.where(m >= win, m - ring, m)
    idx = jnp.clip(past - m, -(CHUNK - 1), CA_REL_PAST) + (CHUNK - 1)
    ringvals = rel_bias[:, idx].astype(F32)
    nh = rel_bias.shape[0]
    bias = jnp.tile(ringvals, (1, tq))[:, :tq * (ring - 1)].reshape(nh, tq, ring - 1)[:, :, :win]
    qc = r // CHUNK
    kc = c // CHUNK
    vis = (kc >= qc) & (kc <= qc + CA_PAST_CHUNKS)
    tabs = []
    n_var = past // tq + 1
    for v in range(n_var):
        first = (past - v * tq) // CHUNK if v < n_var - 1 else 0
        ok = vis & (kc >= first)
        tabs.append(jnp.where(ok[None], bias, NEG_BIG))
    return jnp.stack(tabs)


def _ca_attention(proj, table, batch, seq):
    n = proj.shape[0]
    tq = CA_TQ
    nq = seq // tq
    past = CA_PAST_CHUNKS * CHUNK
    win = tq + past
    n_var = table.shape[0]
    cq, ck, cv = P_D // LANES, (P_D + GROUP_WIDTH) // LANES, (P_D + 2 * GROUP_WIDTH) // LANES
    est = 4 * seq * LANES * 2 + 2 * (seq + past) * LANES * 2 + 2 * tq * win * 4 + 6 * tq * win * 4
    return pl.pallas_call(
        _ca_kernel,
        grid=(batch, CA_HEADS, nq),
        in_specs=[pl.BlockSpec((tq, LANES), lambda b, h, i: (b * nq + i, cq + h)),
                  pl.BlockSpec((seq, LANES), lambda b, h, i: (b, ck + h)),
                  pl.BlockSpec((seq, LANES), lambda b, h, i: (b, cv + h)),
                  pl.BlockSpec((1, 1, tq, win), lambda b, h, i: (jnp.minimum(i, n_var - 1), h, 0, 0))],
        out_specs=pl.BlockSpec((tq, LANES), lambda b, h, i: (b * nq + i, h)),
        out_shape=jax.ShapeDtypeStruct((n, GROUP_WIDTH), BF16),
        scratch_shapes=[pltpu.VMEM((seq + past, LANES), BF16), pltpu.VMEM((seq + past, LANES), BF16)],
        compiler_params=_cparams(("parallel", "parallel", "arbitrary"), est),
    )(proj, proj, proj, table)


def _ssd_kernel(z_ref, xbc_ref, dt_ref, cw_ref, cb_ref, dtb_ref, alog_ref, dskip_ref, ng_ref, ex_ref,
                o_ref, xbuf_ref, state_ref, y_ref):
    c = pl.program_id(1)
    lc = z_ref.shape[0]
    halo = SUBLANES
    gw = GROUP_WIDTH // SSM_GROUPS
    hpg = SSM_HEADS // SSM_GROUPS

    @pl.when(c == 0)
    def _():
        xbuf_ref[0:halo, :] = jnp.zeros((halo, xbuf_ref.shape[1]), F32)
        state_ref[...] = jnp.zeros(state_ref.shape, F32)

    xbuf_ref[halo:halo + lc, :] = xbc_ref[...].astype(F32)
    acc = jnp.broadcast_to(cb_ref[...], (lc, SSM_CONV_DIM))
    for k in range(SSM_CONV):
        acc = acc + cw_ref[k:k + 1, :] * xbuf_ref[pl.ds(halo - (SSM_CONV - 1) + k, lc), :]
    xbuf_ref[0:halo, :] = xbuf_ref[lc:lc + halo, :]
    u = acc * jax.nn.sigmoid(acc)
    xs = u[:, :GROUP_WIDTH]
    bm = u[:, GROUP_WIDTH:GROUP_WIDTH + SSM_GROUPS * SSM_STATE].astype(BF16)
    cm = u[:, GROUP_WIDTH + SSM_GROUPS * SSM_STATE:].astype(BF16)

    dt = jax.nn.softplus(dt_ref[...].astype(F32) + dtb_ref[...])
    a = -jnp.exp(alog_ref[...])
    da = dt * a
    row = lax.broadcasted_iota(jnp.int32, (lc, lc), 0)
    col = lax.broadcasted_iota(jnp.int32, (lc, lc), 1)
    lower = col <= row
    cs = _dot3_left01(lower.astype(BF16), da)
    cs_last = cs[lc - 1:lc, :]
    cs_t = cs.T
    dt_t = dt.T
    ex = ex_ref[...]
    ecs_x = _dot3_right01(jnp.exp(cs), ex)
    wdec_x = _dot3_right01(jnp.exp(cs_last - cs) * dt, ex)
    xs_b = xs.astype(BF16)

    for g in range(SSM_GROUPS):
        bg = bm[:, g * SSM_STATE:(g + 1) * SSM_STATE]
        cg = cm[:, g * SSM_STATE:(g + 1) * SSM_STATE]
        cb = _dot_nt(cg, bg)
        for r in range(hpg):
            h = g * hpg + r
            diff = cs[:, h:h + 1] - cs_t[h:h + 1, :]
            wgt = cb * jnp.exp(jnp.where(lower, diff, NEG_BIG)) * dt_t[h:h + 1, :]
            y_ref[:, h * SSM_HEAD_DIM:(h + 1) * SSM_HEAD_DIM] = _dot(
                wgt.astype(BF16), xs_b[:, h * SSM_HEAD_DIM:(h + 1) * SSM_HEAD_DIM])
        lo, hi = g * gw, (g + 1) * gw
        st = state_ref[g]
        y_off = _dot(cg, st.astype(BF16)) * ecs_x[:, lo:hi]
        y_ref[:, lo:hi] += y_off
        xw = (xs[:, lo:hi] * wdec_x[:, lo:hi]).astype(BF16)
        state_ref[g] = st * ecs_x[lc - 1:lc, lo:hi] + _dot_tn(bg, xw)

    zf = z_ref[...].astype(F32)
    y = (y_ref[...] + xs * dskip_ref[...]) * (zf * jax.nn.sigmoid(zf))
    for g in range(SSM_GROUPS):
        lo, hi = g * gw, (g + 1) * gw
        yg = y[:, lo:hi]
        yg = yg * lax.rsqrt(jnp.mean(yg * yg, axis=-1, keepdims=True) + EPS)
        o_ref[:, lo:hi] = (yg * ng_ref[:, lo:hi]).astype(o_ref.dtype)


def _ssd(proj, conv_w, conv_b, dt_bias, a_log, d_skip, norm_g, batch, seq):
    n = proj.shape[0]
    lc = SSD_LC
    nc = seq // lc
    dtb = jnp.zeros((1, LANES), F32).at[0, :SSM_HEADS].set(dt_bias)
    alog = jnp.full((1, LANES), NEG_BIG, F32).at[0, :SSM_HEADS].set(a_log)
    dskip = jnp.repeat(d_skip, SSM_HEAD_DIM).reshape(1, GROUP_WIDTH)
    ex = (jnp.arange(LANES)[:, None] == jnp.arange(GROUP_WIDTH)[None, :] // SSM_HEAD_DIM).astype(BF16)
    full = lambda shape: pl.BlockSpec(shape, lambda b, c: tuple(0 for _ in shape))
    return pl.pallas_call(
        _ssd_kernel,
        grid=(batch, nc),
        in_specs=[pl.BlockSpec((lc, GROUP_WIDTH), lambda b, c: (b * nc + c, P_Z // GROUP_WIDTH)),
                  pl.BlockSpec((lc, SSM_CONV_DIM), lambda b, c: (b * nc + c, P_XBC // SSM_CONV_DIM)),
                  pl.BlockSpec((lc, LANES), lambda b, c: (b * nc + c, P_DT // LANES)),
                  full((SSM_CONV, SSM_CONV_DIM)), full((1, SSM_CONV_DIM)), full((1, LANES)), full((1, LANES)),
                  full((1, GROUP_WIDTH)), full((1, GROUP_WIDTH)), full((LANES, GROUP_WIDTH))],
        out_specs=pl.BlockSpec((lc, GROUP_WIDTH), lambda b, c: (b * nc + c, 0)),
        out_shape=jax.ShapeDtypeStruct((n, GROUP_WIDTH), BF16),
        scratch_shapes=[pltpu.VMEM((lc + SUBLANES, SSM_CONV_DIM), F32),
                        pltpu.VMEM((SSM_GROUPS, SSM_STATE, GROUP_WIDTH // SSM_GROUPS), F32),
                        pltpu.VMEM((lc, GROUP_WIDTH), F32)],
        compiler_params=_cparams(("parallel", "arbitrary"), 16 * 1024 * 1024),
    )(proj, proj, proj, conv_w, conv_b.reshape(1, -1), dtb, alog, dskip, norm_g.reshape(1, -1), ex)


def _outproj_kernel(a_ref, b_ref, c_ref, d_ref, x_ref, gmix_ref, w_ref, mod_ref, gf_ref, rw_ref, rb_ref,
                    xo_ref, h_ref, e_ref, gt_ref):
    gw = GROUP_WIDTH
    a = _rms(a_ref[...].astype(F32), gmix_ref[0:1, :]).astype(BF16)
    c = _rms(c_ref[...].astype(F32), gmix_ref[1:2, :]).astype(BF16)
    d = _rms(d_ref[...].astype(F32), gmix_ref[2:3, :]).astype(BF16)
    mixed = (_dot(a, w_ref[0:gw, :]) + _dot(b_ref[...], w_ref[gw:2 * gw, :])
             + _dot(c, w_ref[2 * gw:3 * gw, :]) + _dot(d, w_ref[3 * gw:, :]))
    x = x_ref[...] + mod_ref[0, 2:3, :] * mixed
    xo_ref[...] = x
    h = _rms(x, gf_ref[...]) * (1.0 + mod_ref[0, 4:5, :]) + mod_ref[0, 3:4, :]
    _store_packed_rows(h_ref, h)

    hh = h.astype(BF16)
    hl = (h - hh.astype(F32)).astype(BF16)
    rw = rw_ref[...]
    wh = rw.astype(BF16)
    wl = (rw - wh.astype(F32)).astype(BF16)
    logits = _dot(hh, wh) + _dot(hh, wl) + _dot(hl, wh) + rb_ref[...]
    lane = lax.broadcasted_iota(jnp.int32, logits.shape, 1)
    vals, idxs = [], []
    for _ in range(TOP_K):
        m = jnp.max(logits, axis=-1, keepdims=True)
        idx = jnp.min(jnp.where(logits == m, lane, LANES), axis=-1, keepdims=True)
        vals.append(m)
        idxs.append(idx)
        logits = jnp.where(lane == idx, -jnp.inf, logits)
    ex = [jnp.exp(v - vals[0]) for v in vals]
    tot = ex[0] + ex[1] + ex[2] + ex[3]
    e_out = jnp.zeros(logits.shape, jnp.int32)
    g_out = jnp.zeros(logits.shape, F32)
    for k in range(TOP_K):
        e_out = jnp.where(lane == k, idxs[k], e_out)
        g_out = jnp.where(lane == k, ex[k] / tot, g_out)
    e_ref[...] = e_out
    gt_ref[...] = g_out


def _outproj(oa, ob, oc, od, x2, gmix, w_out, mod, gf, rw, rb, seq):
    n, d = x2.shape
    tm = 256
    per_b = seq // tm
    gw = GROUP_WIDTH
    row = lambda width: pl.BlockSpec((tm, width), lambda i: (i, 0))
    full = lambda shape: pl.BlockSpec(shape, lambda i: tuple(0 for _ in shape))
    est = 2 * d * d * 2 + 4 * tm * d * 4 + 2 * tm * d * 2 + 8 * tm * gw * 2 + 6 * tm * d * 4
    return pl.pallas_call(
        _outproj_kernel,
        grid=(n // tm,),
        in_specs=[row(gw), row(gw), row(gw), row(gw), row(d), full((3, gw)), full((d, d)),
                  pl.BlockSpec((1, 6, d), lambda i: (i // per_b, 0, 0)), full((1, d)),
                  full((d, LANES)), full((1, LANES))],
        out_specs=[row(d), pl.BlockSpec((tm * ROW_SUBLANES, LANES), lambda i: (i, 0)), row(LANES), row(LANES)],
        out_shape=[jax.ShapeDtypeStruct((n, d), F32), jax.ShapeDtypeStruct((n * ROW_SUBLANES, LANES), U32),
                   jax.ShapeDtypeStruct((n, LANES), jnp.int32), jax.ShapeDtypeStruct((n, LANES), F32)],
        compiler_params=_cparams(("parallel",), est),
    )(oa, ob, oc, od, x2, gmix, w_out, mod, gf.reshape(1, d), rw, rb)


def _row_copy(src_ref, src_row, dst_ref, dst_row, sem):
    rs = ROW_SUBLANES
    return pltpu.make_async_copy(src_ref.at[pl.ds(pl.multiple_of(src_row * rs, rs), rs)],
                                 dst_ref.at[pl.ds(pl.multiple_of(dst_row * rs, rs), rs)], sem)


def _expert_kernel(be_ref, nu_ref, tok_ref, tok_next_ref, dst_prev_ref, dst_ref, hp_ref, w1_ref, b1_ref, w2_ref,
                   b2_ref, y_ref, xa_ref, xb_ref, ya_ref, yb_ref, gsem, ssem):
    i = pl.program_id(0)
    nu = nu_ref[0]
    tm = MOE_BLOCK
    rs = ROW_SUBLANES

    def start_rows(copy_of_row):
        for r in range(tm):
            copy_of_row(r).start(priority=r % 2)

    def wait_rows(src_ref, dst_ref, sem):
        for _ in range(tm):
            _row_copy(src_ref, 0, dst_ref, 0, sem).wait()

    @pl.when(i == 0)
    def _():
        start_rows(lambda r: _row_copy(hp_ref, tok_ref[0, 0, r], xa_ref, r, gsem.at[0]))
        yb_ref[...] = jnp.zeros(yb_ref.shape, U32)
        spare = pltpu.make_async_copy(yb_ref, y_ref.at[pl.ds(y_ref.shape[0] - 2 * tm * rs, tm * rs)], ssem.at[0])
        spare.start()
        spare.wait()

    def body(xcur, xnext, ycur, yprev, gcur, gnext, scur, sprev):
        wait_rows(hp_ref, xcur, gcur)
        start_rows(lambda r: _row_copy(hp_ref, tok_next_ref[0, 0, r], xnext, r, gnext))
        start_rows(lambda r: _row_copy(yprev, r, y_ref, dst_prev_ref[0, 0, r], sprev))
        words = _load_packed_rows(xcur)
        x = jnp.concatenate([_unpack_lo(u).astype(BF16) for u in words]
                            + [_unpack_hi(u).astype(BF16) for u in words], axis=1)
        cw = EXPERT_COL_CHUNK
        hu = jnp.concatenate([_dot(x, w1_ref[0, :, c:c + cw].astype(BF16)) for c in range(0, w1_ref.shape[2], cw)],
                             axis=1) + b1_ref[0]
        gate = jnp.minimum(hu[:, :D_EXPERT], SWIGLU_LIMIT)
        up = jnp.clip(hu[:, D_EXPERT:], -SWIGLU_LIMIT, SWIGLU_LIMIT)
        act = (gate * jax.nn.sigmoid(SWIGLU_ALPHA * gate) * (up + 1.0)).astype(BF16)
        y = jnp.concatenate([_dot(act, w2_ref[0, :, c:c + cw].astype(BF16)) for c in range(0, w2_ref.shape[2], cw)],
                            axis=1) + b2_ref[0]

        @pl.when(i > 0)
        def _():
            wait_rows(ycur, y_ref, scur)

        _store_packed_rows(ycur, y)

        @pl.when(i == nu - 1)
        def _():
            start_rows(lambda r: _row_copy(ycur, r, y_ref, dst_ref[0, 0, r], scur))
            wait_rows(yprev, y_ref, sprev)
            wait_rows(ycur, y_ref, scur)
            wait_rows(hp_ref, xnext, gnext)

    even = lax.rem(i, 2) == 0

    @pl.when(jnp.logical_and(i < nu, even))
    def _():
        body(xa_ref, xb_ref, ya_ref, yb_ref, gsem.at[0], gsem.at[1], ssem.at[0], ssem.at[1])

    @pl.when(jnp.logical_and(i < nu, jnp.logical_not(even)))
    def _():
        body(xb_ref, xa_ref, yb_ref, ya_ref, gsem.at[1], gsem.at[0], ssem.at[1], ssem.at[0])


def _experts(hp, row_tok, row_dst, block_e, n_used, n_slots, w1, b1, w2, b2, layer=0):
    tm = MOE_BLOCK
    nblk = row_tok.shape[0]
    d, de2 = w1.shape[1], w1.shape[2]
    de = w2.shape[1]
    rs = ROW_SUBLANES
    block_e = block_e + layer * N_EXPERTS
    last = lambda i, nu: jnp.minimum(i, nu[0] - 1)
    smem_rows = lambda f: pl.BlockSpec((1, 1, tm), f, memory_space=pltpu.SMEM)
    est = (2 * (d * de2 + de * d) * w1.dtype.itemsize + 4 * tm * rs * LANES * 4 + 4 * tm * d * 4
           + 3 * tm * de2 * 4)
    first_prev = (n_slots - tm + jnp.arange(tm, dtype=jnp.int32)).reshape(1, 1, tm)
    row_dst_prev = jnp.concatenate([first_prev, row_dst[:-1]], axis=0)
    grid_spec = pltpu.PrefetchScalarGridSpec(
        num_scalar_prefetch=2,
        grid=(nblk,),
        in_specs=[smem_rows(lambda i, be, nu: (last(i, nu), 0, 0)),
                  smem_rows(lambda i, be, nu: (last(i + 1, nu), 0, 0)),
                  smem_rows(lambda i, be, nu: (last(i, nu), 0, 0)),
                  smem_rows(lambda i, be, nu: (last(i, nu), 0, 0)),
                  pl.BlockSpec(memory_space=pl.ANY),
                  pl.BlockSpec((1, d, de2), lambda i, be, nu: (be[last(i, nu)], 0, 0)),
                  pl.BlockSpec((1, 1, de2), lambda i, be, nu: (be[last(i, nu)], 0, 0)),
                  pl.BlockSpec((1, de, d), lambda i, be, nu: (be[last(i, nu)], 0, 0)),
                  pl.BlockSpec((1, 1, d), lambda i, be, nu: (be[last(i, nu)], 0, 0))],
        out_specs=pl.BlockSpec(memory_space=pl.ANY),
        scratch_shapes=[pltpu.VMEM((tm * rs, LANES), U32) for _ in range(4)]
        + [pltpu.SemaphoreType.DMA((2,)), pltpu.SemaphoreType.DMA((2,))],
    )
    return pl.pallas_call(
        _expert_kernel,
        grid_spec=grid_spec,
        out_shape=jax.ShapeDtypeStruct((n_slots * rs, LANES), U32),
        compiler_params=_cparams(("arbitrary",), est),
    )(block_e, n_used, row_tok, row_tok, row_dst_prev, row_dst, hp, w1, b1.reshape(b1.shape[0], 1, de2), w2,
      b2.reshape(b2.shape[0], 1, d))


def _combine_kernel(y0_ref, y1_ref, y2_ref, y3_ref, g_ref, x_ref, mod_ref, o_ref):
    tm, d = x_ref.shape
    half = d // 2
    rs = ROW_SUBLANES
    rc = COMBINE_ROWS

    def chunk(c, carry):
        r0 = pl.multiple_of(c * rc, rc)
        rows = pl.ds(r0, rc)
        g = g_ref[rows, :]
        lo = [jnp.zeros((rc, LANES), F32) for _ in range(rs)]
        hi = [jnp.zeros((rc, LANES), F32) for _ in range(rs)]
        for k, y_ref in enumerate((y0_ref, y1_ref, y2_ref, y3_ref)):
            blk = y_ref[pl.ds(pl.multiple_of(r0 * rs, rc * rs), rc * rs), :]
            w = jnp.swapaxes(blk.reshape(rc // rs, rs, rs, LANES), 1, 2)
            gk = g[:, k:k + 1]
            for s in range(rs):
                u = w[:, s].reshape(rc, LANES)
                lo[s] = lo[s] + _unpack_lo(u) * gk
                hi[s] = hi[s] + _unpack_hi(u) * gk
        for s in range(rs):
            a, b = LANES * s, half + LANES * s
            o_ref[rows, a:a + LANES] = x_ref[rows, a:a + LANES] + mod_ref[0, 5:6, a:a + LANES] * lo[s]
            o_ref[rows, b:b + LANES] = x_ref[rows, b:b + LANES] + mod_ref[0, 5:6, b:b + LANES] * hi[s]
        return carry
    lax.fori_loop(0, tm // rc, chunk, 0)


def _combine(y4, gates, x2, mod, seq):
    n, d = x2.shape
    tm = 256
    per_b = seq // tm
    nt = n // tm
    rows = tm * ROW_SUBLANES
    est = 2 * TOP_K * rows * LANES * 4 + 4 * tm * d * 4 + 2 * tm * LANES * 4
    yspec = lambda k: pl.BlockSpec((rows, LANES), lambda i: (k * nt + i, 0))
    return pl.pallas_call(
        _combine_kernel,
        grid=(nt,),
        in_specs=[yspec(0), yspec(1), yspec(2), yspec(3),
                  pl.BlockSpec((tm, LANES), lambda i: (i, 0)),
                  pl.BlockSpec((tm, d), lambda i: (i, 0)),
                  pl.BlockSpec((1, 6, d), lambda i: (i // per_b, 0, 0))],
        out_specs=pl.BlockSpec((tm, d), lambda i: (i, 0)),
        out_shape=jax.ShapeDtypeStruct((n, d), F32),
        compiler_params=_cparams(("parallel",), est),
    )(y4, y4, y4, y4, gates, x2, mod)


def _final_norm_kernel(x_ref, g_ref, o_ref):
    o_ref[...] = _rms(x_ref[...], g_ref[...])


def _final_norm(x2, g):
    n, d = x2.shape
    tm = 512
    return pl.pallas_call(
        _final_norm_kernel,
        grid=(n // tm,),
        in_specs=[pl.BlockSpec((tm, d), lambda i: (i, 0)), pl.BlockSpec((1, d), lambda i: (0, 0))],
        out_specs=pl.BlockSpec((tm, d), lambda i: (i, 0)),
        out_shape=jax.ShapeDtypeStruct((n, d), F32),
        compiler_params=_cparams(("parallel",), 4 * tm * d * 4),
    )(x2, g.reshape(1, d))


def _swap_halves(w):
    half = w.shape[-1] // 2
    return jnp.concatenate([w[..., half:], w[..., :half]], axis=-1)


def _pack_w_in_kernel(w_ref, o_ref):
    w = w_ref[0]
    b0, c0 = A_COLS, A_COLS + B_COLS
    kr0 = MLA_Q_RANK + MLA_KV_RANK
    half = MLA_ROPE // 2
    pieces = [w[:, :MLA_Q_RANK],
              w[:, b0:b0 + GROUP_WIDTH + SSM_CONV_DIM],
              w[:, MLA_Q_RANK:kr0 + MLA_ROPE],
              w[:, kr0 + half:kr0 + MLA_ROPE], w[:, kr0:kr0 + half],
              w[:, c0 - SSM_HEADS:c0], jnp.zeros((w.shape[0], LANES - SSM_HEADS), w.dtype),
              w[:, c0:]]
    o_ref[0] = jnp.concatenate(pieces, axis=1).astype(BF16)


def _pack_w_in(w_in):
    nl, d, nc = w_in.shape
    tr = 256
    est = 2 * tr * nc * 4 + 2 * tr * P_COLS * 2 + 2 * tr * P_COLS * 4
    return pl.pallas_call(
        _pack_w_in_kernel,
        grid=(nl, d // tr),
        in_specs=[pl.BlockSpec((1, tr, nc), lambda l, i: (l, i, 0))],
        out_specs=pl.BlockSpec((1, tr, P_COLS), lambda l, i: (l, i, 0)),
        out_shape=jax.ShapeDtypeStruct((nl, d, P_COLS), BF16),
        compiler_params=_cparams(("parallel", "parallel"), est),
    )(w_in)


def _pack_wq(wq):
    r = wq.shape[0]
    w = wq.reshape(r, MLA_HEADS, MLA_NOPE + MLA_ROPE)
    rope = w[..., MLA_NOPE:]
    return jnp.concatenate([w[..., :MLA_NOPE], rope, _swap_halves(rope)], axis=-1).reshape(r, -1).astype(BF16)


def _pack_wkv(wkv):
    r = wkv.shape[0]
    w = wkv.reshape(r, MLA_HEADS, MLA_NOPE + MLA_V)
    return jnp.concatenate([w[..., :MLA_NOPE].reshape(r, -1), w[..., MLA_NOPE:].reshape(r, -1)], axis=-1).astype(BF16)


def _rope_tables(seq):
    inv = 1.0 / (ROPE_THETA ** (jnp.arange(0, MLA_ROPE, 2, dtype=F32) / MLA_ROPE))
    ang = jnp.arange(seq, dtype=F32)[:, None] * inv[None, :]
    cos, sin = jnp.cos(ang), jnp.sin(ang)
    zero = jnp.zeros((seq, LANES - MLA_ROPE), F32)
    return jnp.concatenate([cos, cos, zero], axis=1), jnp.concatenate([-sin, sin, zero], axis=1)


def _route(top_e, n_tok):
    n_assign = n_tok * TOP_K
    n_blocks = -(-(n_assign + N_EXPERTS * (MOE_BLOCK - 1)) // MOE_BLOCK)
    n_rows = n_blocks * MOE_BLOCK
    dest, meta = _route_dest(top_e)
    padded_end = meta[2, :N_EXPERTS]
    block_start = jnp.arange(n_blocks, dtype=jnp.int32) * MOE_BLOCK
    block_e = jnp.minimum(jnp.sum((padded_end[None, :] <= block_start[:, None]).astype(jnp.int32), axis=1),
                          N_EXPERTS - 1)
    n_used = (padded_end[N_EXPERTS - 1] // MOE_BLOCK).reshape(1)
    row_assign = _route_invert(meta, dest[:, :TOP_K].reshape(n_assign), n_rows)
    real = row_assign >= 0
    row = jnp.arange(n_rows, dtype=jnp.int32)
    row_tok = jnp.where(real, row_assign // TOP_K, 0)
    spare = n_assign + (row // MOE_BLOCK % 2) * MOE_BLOCK + row % MOE_BLOCK
    row_dst = jnp.where(real, row_assign % TOP_K * n_tok + row_assign // TOP_K, spare)
    shape = (n_blocks, 1, MOE_BLOCK)
    return row_tok.reshape(shape), row_dst.reshape(shape), block_e, n_used, n_assign + 2 * MOE_BLOCK


ROUTE_TM = 512
COMBINE_ROWS = 16


def _route_dest_kernel(e_ref, dest_ref, meta_ref, carry_ref, base_ref):
    p = pl.program_id(0)
    i = pl.program_id(1)
    tm = e_ref.shape[0]
    e = e_ref[...]
    lane = lax.broadcasted_iota(jnp.int32, (tm, LANES), 1)
    hit = [lane == e[:, k:k + 1] for k in range(TOP_K)]
    tot = sum(h.astype(F32) for h in hit)

    @pl.when(jnp.logical_and(p == 0, i == 0))
    def _():
        carry_ref[...] = jnp.zeros(carry_ref.shape, F32)
        meta_ref[...] = jnp.zeros(meta_ref.shape, jnp.int32)

    @pl.when(jnp.logical_and(p == 0, i == pl.num_programs(1) - 1))
    def _():
        counts = carry_ref[...] + jnp.sum(tot, axis=0, keepdims=True)
        padded = jnp.floor((counts + (MOE_BLOCK - 1)) * (1.0 / MOE_BLOCK)) * MOE_BLOCK
        upper = (lax.broadcasted_iota(jnp.int32, (LANES, LANES), 0)
                 <= lax.broadcasted_iota(jnp.int32, (LANES, LANES), 1)).astype(BF16)
        padded_end = _dot3_right01(jnp.broadcast_to(padded, (SUBLANES, LANES)), upper)[0:1]
        base_ref[...] = padded_end - padded
        meta_ref[0:1, :] = counts.astype(jnp.int32)
        meta_ref[1:2, :] = (padded_end - padded).astype(jnp.int32)
        meta_ref[2:3, :] = padded_end.astype(jnp.int32)
        carry_ref[...] = -jnp.sum(tot, axis=0, keepdims=True)

    @pl.when(p == 1)
    def _():
        earlier = (lax.broadcasted_iota(jnp.int32, (tm, tm), 1)
                   < lax.broadcasted_iota(jnp.int32, (tm, tm), 0)).astype(BF16)
        pos = _dot(earlier, tot.astype(BF16)) + carry_ref[...] + base_ref[...]
        dest = jnp.zeros((tm, LANES), jnp.int32)
        for k in range(TOP_K):
            dk = jnp.sum(jnp.where(hit[k], pos, 0.0), axis=-1, keepdims=True)
            dest = jnp.where(lane == k, dk.astype(jnp.int32), dest)
        dest_ref[...] = dest

    carry_ref[...] += jnp.sum(tot, axis=0, keepdims=True)


def _route_dest(top_e):
    n = top_e.shape[0]
    tm = ROUTE_TM
    return pl.pallas_call(
        _route_dest_kernel,
        grid=(2, n // tm),
        in_specs=[pl.BlockSpec((tm, LANES), lambda p, i: (i, 0))],
        out_specs=[pl.BlockSpec((tm, LANES), lambda p, i: (i * p, 0)),
                   pl.BlockSpec((SUBLANES, LANES), lambda p, i: (0, 0))],
        out_shape=[jax.ShapeDtypeStruct((n, LANES), jnp.int32), jax.ShapeDtypeStruct((SUBLANES, LANES), jnp.int32)],
        scratch_shapes=[pltpu.VMEM((1, LANES), F32), pltpu.VMEM((1, LANES), F32)],
        compiler_params=_cparams(("arbitrary", "arbitrary"), 8 * tm * tm * 4),
    )(top_e)


ROUTE_CHUNK = 2048


def _route_invert_kernel(meta_ref, dest_ref, out_ref):
    s = pl.program_id(0)
    ch = dest_ref.shape[2]
    per = SUBLANES

    def clear(r, carry):
        out_ref[r] = -1
        return carry

    @pl.when(s == 0)
    def _():
        def expert_padding(e, carry):
            return lax.fori_loop(meta_ref[1, e] + meta_ref[0, e], meta_ref[2, e], clear, carry)
        lax.fori_loop(0, N_EXPERTS, expert_padding, 0)
        lax.fori_loop(meta_ref[2, N_EXPERTS - 1], out_ref.shape[0], clear, 0)

    def place(j, carry):
        for u in range(per):
            a = j * per + u
            out_ref[dest_ref[0, 0, a]] = s * ch + a
        return carry
    lax.fori_loop(0, ch // per, place, 0)


def _route_invert(meta, dest_flat, n_rows):
    ch = ROUTE_CHUNK
    n_place = dest_flat.shape[0] // ch
    assert n_place * ch == dest_flat.shape[0]
    return pl.pallas_call(
        _route_invert_kernel,
        grid=(n_place,),
        in_specs=[pl.BlockSpec(memory_space=pltpu.SMEM),
                  pl.BlockSpec((1, 1, ch), lambda s: (s, 0, 0), memory_space=pltpu.SMEM)],
        out_specs=pl.BlockSpec(memory_space=pltpu.SMEM),
        out_shape=jax.ShapeDtypeStruct((n_rows,), jnp.int32),
        compiler_params=pltpu.CompilerParams(dimension_semantics=("arbitrary",)),
    )(meta, dest_flat.reshape(n_place, 1, ch))


def kernel(x, c, attn_norm, ffn_norm, mod_w, mod_b, w_in, mla_q_norm, mla_w_q_up, mla_kv_norm, mla_w_kv_up,
           ssm_conv_w, ssm_conv_b, ssm_dt_bias, ssm_a_log, ssm_d, ssm_norm, ca_rel_bias, mix_out_norm, w_out,
           router_w, router_b, moe_w1, moe_b1, moe_w2, moe_b2, final_norm):
    batch, seq, d = x.shape
    n = batch * seq
    depth = w_in.shape[0]
    x2 = x.reshape(n, d)
    mod_all = _mod_all(c, mod_w, mod_b).reshape(depth, batch, 6, d)
    tc, ts = _rope_tables(seq)
    w_in_packed = _pack_w_in(w_in)
    for l in range(depth):
        mod = mod_all[l]
        proj = _inproj(x2, attn_norm[l], mod, w_in_packed, l, seq)
        q, k, v = _mla_prep(proj, mla_q_norm[l], mla_kv_norm[l], _pack_wq(mla_w_q_up[l]),
                            _pack_wkv(mla_w_kv_up[l]), tc, ts, seq)
        out_a = _mla_flash(q, k, v, batch, seq)
        out_b = _ssd(proj, ssm_conv_w[l], ssm_conv_b[l], ssm_dt_bias[l], ssm_a_log[l], ssm_d[l], ssm_norm[l],
                     batch, seq)
        out_c = _sb_attention(proj, batch, seq)
        out_d = _ca_attention(proj, _ca_table(ca_rel_bias[l], CA_TQ), batch, seq)
        rw = jnp.zeros((d, LANES), F32).at[:, :N_EXPERTS].set(router_w[l])
        rb = jnp.full((1, LANES), NEG_BIG, F32).at[0, :N_EXPERTS].set(router_b[l])
        x2, hp, top_e, gates = _outproj(out_a, out_b, out_c, out_d, x2, mix_out_norm[l], w_out[l].astype(BF16),
                                        mod, ffn_norm[l], rw, rb, seq)
        row_tok, row_dst, block_e, n_used, n_slots = _route(top_e, n)
        y4 = _experts(hp, row_tok, row_dst, block_e, n_used, n_slots,
                      moe_w1.reshape(-1, *moe_w1.shape[2:]), moe_b1.reshape(-1, moe_b1.shape[2]),
                      moe_w2.reshape(-1, *moe_w2.shape[2:]), moe_b2.reshape(-1, moe_b2.shape[2]), layer=l)
        x2 = _combine(y4, gates, x2, mod, seq)
    return _final_norm(x2, final_norm).reshape(batch, seq, d)
```

```python
import functools
import math

import jax
import jax.numpy as jnp
from jax import lax
from jax.experimental import pallas as pl
from jax.experimental.pallas import tpu as pltpu

F32 = jnp.float32
BF16 = jnp.bfloat16
U32 = jnp.uint32

D_MODEL = 2048
DEPTH = 4
CHUNK = 64
EPS = 1e-6
GROUP_WIDTH = 512
MLA_HEADS = 4
MLA_NOPE = 128
MLA_ROPE = 64
MLA_V = 128
MLA_Q_RANK = 512
MLA_KV_RANK = 256
ROPE_THETA = 10000.0
SSM_HEAD_DIM = 64
SSM_HEADS = 8
SSM_GROUPS = 2
SSM_STATE = 128
SSM_CONV = 4
SSM_CONV_DIM = 1024
SB_HEADS = 4
CA_HEADS = 4
CA_PAST_CHUNKS = 8
CA_REL_PAST = 256
N_EXPERTS = 32
TOP_K = 4
D_EXPERT = 768
SWIGLU_ALPHA = 1.702
SWIGLU_LIMIT = 7.0

A_COLS = MLA_Q_RANK + MLA_KV_RANK + MLA_ROPE
B_COLS = GROUP_WIDTH + SSM_CONV_DIM + SSM_HEADS
C_COLS = 3 * GROUP_WIDTH

LANES = 128
SUBLANES = 8
VMEM_BYTES_V7X = 64 * 1024 * 1024
NEG_BIG = -1e30
LOG2_E = math.log2(math.e)

P_QLAT = 0
P_Z = 512
P_XBC = 1024
P_KVLAT = 2048
P_KROPE = 2304
P_DT = 2432
P_C = 2560
P_D = 4096
P_COLS = 5632

MOE_BLOCK = 256
EXPERT_COL_CHUNK = 256
ATT_TQ = 512
SB_SUB = 256
ROW_SUBLANES = 8
CA_TQ = 256
CA_WIN = CA_TQ + CA_PAST_CHUNKS * CHUNK
SSD_LC = 256


def _vmem_limit(nbytes):
    return int(min(max(2 * nbytes, 16 * 1024 * 1024), VMEM_BYTES_V7X - 8 * 1024 * 1024))


def _cparams(sem, nbytes):
    return pltpu.CompilerParams(dimension_semantics=sem, vmem_limit_bytes=_vmem_limit(nbytes))


def _rms(x, g):
    return x * lax.rsqrt(jnp.mean(x * x, axis=-1, keepdims=True) + EPS) * g


def _split3(x):
    hi = x.astype(BF16)
    r1 = x - hi.astype(F32)
    mid = r1.astype(BF16)
    lo = (r1 - mid.astype(F32)).astype(BF16)
    return hi, mid, lo


def _dot(a, b):
    return jnp.dot(a, b, preferred_element_type=F32)


def _dot_nt(a, b):
    return lax.dot_general(a, b, (((1,), (1,)), ((), ())), preferred_element_type=F32)


def _dot_tn(a, b):
    return lax.dot_general(a, b, (((0,), (0,)), ((), ())), preferred_element_type=F32)


def _dot3_left01(t01, x):
    hi, mid, lo = _split3(x)
    return _dot(t01, hi) + _dot(t01, mid) + _dot(t01, lo)


def _dot3_right01(x, t01):
    hi, mid, lo = _split3(x)
    return _dot(hi, t01) + _dot(mid, t01) + _dot(lo, t01)


def _bf16_bits(x):
    u = lax.bitcast_convert_type(x, U32)
    return (u + jnp.uint32(0x7FFF) + ((u >> 16) & jnp.uint32(1))) >> 16


def _store_packed_rows(ref, y):
    tm, d = y.shape
    half = d // 2
    rs = ROW_SUBLANES
    words = []
    for s in range(rs):
        lo = _bf16_bits(y[:, LANES * s:LANES * (s + 1)])
        hi = _bf16_bits(y[:, half + LANES * s:half + LANES * (s + 1)])
        words.append(((hi << 16) | lo).reshape(tm // rs, rs, LANES))
    ref[...] = jnp.swapaxes(jnp.stack(words, axis=1), 1, 2).reshape(tm * rs, LANES)


def _load_packed_rows(ref):
    rs = ROW_SUBLANES
    tm = ref.shape[0] // rs
    w = jnp.swapaxes(ref[...].reshape(tm // rs, rs, rs, LANES), 1, 2)
    return [w[:, s].reshape(tm, LANES) for s in range(rs)]


def _unpack_lo(u):
    return lax.bitcast_convert_type(u << 16, F32)


def _unpack_hi(u):
    return lax.bitcast_convert_type(u & jnp.uint32(0xFFFF0000), F32)


def _mod_kernel(ct_ref, w_ref, b_ref, o_ref):
    k = pl.program_id(2)
    nb = o_ref.shape[1]

    @pl.when(k == 0)
    def _():
        o_ref[0] = jnp.broadcast_to(b_ref[0], o_ref.shape[1:])

    ct = ct_ref[...]
    cond = ct * jax.nn.sigmoid(ct)
    w = w_ref[0]
    for b in range(nb):
        o_ref[0, b:b + 1, :] += jnp.sum(w * cond[:, b:b + 1], axis=0, keepdims=True)


def _mod_all(c, mod_w, mod_b):
    nb, d = c.shape
    nl, _, nout = mod_w.shape
    tk, tn = 512, 2048
    ct = c.T
    return pl.pallas_call(
        _mod_kernel,
        grid=(nl, nout // tn, d // tk),
        in_specs=[pl.BlockSpec((tk, nb), lambda l, j, k: (k, 0)),
                  pl.BlockSpec((1, tk, tn), lambda l, j, k: (l, k, j)),
                  pl.BlockSpec((1, 1, tn), lambda l, j, k: (l, 0, j))],
        out_specs=pl.BlockSpec((1, nb, tn), lambda l, j, k: (l, 0, j)),
        out_shape=jax.ShapeDtypeStruct((nl, nb, nout), F32),
        compiler_params=_cparams(("parallel", "parallel", "arbitrary"), 2 * tk * tn * 4),
    )(ct, mod_w, mod_b.reshape(nl, 1, nout))


def _inproj_kernel(x_ref, g_ref, mod_ref, w_ref, o_ref, h_ref):
    @pl.when(pl.program_id(1) == 0)
    def _():
        rc = NORM_ROWS
        gain = g_ref[...] * (1.0 + mod_ref[0, 1:2, :])
        shift = mod_ref[0, 0:1, :]

        def chunk(c, carry):
            rows = pl.ds(pl.multiple_of(c * rc, rc), rc)
            x = x_ref[rows, :]
            r = lax.rsqrt(jnp.mean(x * x, axis=-1, keepdims=True) + EPS)
            h_ref[rows, :] = (x * r * gain + shift).astype(BF16)
            return carry
        lax.fori_loop(0, x_ref.shape[0] // rc, chunk, 0, unroll=4)

    o_ref[...] = _dot(h_ref[...], w_ref[0]).astype(o_ref.dtype)


def _inproj(x2, g, mod, w, layer, seq):
    n, d = x2.shape
    ncol = w.shape[2]
    tm, tn = 512, ncol // 4
    per_b = seq // tm
    est = 2 * tm * d * 4 + tm * d * 2 + 2 * d * tn * 2 + 2 * tm * tn * 2 + tm * tn * 4
    return pl.pallas_call(
        _inproj_kernel,
        grid=(n // tm, ncol // tn),
        in_specs=[pl.BlockSpec((tm, d), lambda i, j: (i, 0)),
                  pl.BlockSpec((1, d), lambda i, j: (0, 0)),
                  pl.BlockSpec((1, 6, d), lambda i, j: (i // per_b, 0, 0)),
                  pl.BlockSpec((1, d, tn), lambda i, j: (layer, 0, j))],
        out_specs=pl.BlockSpec((tm, tn), lambda i, j: (i, j)),
        out_shape=jax.ShapeDtypeStruct((n, ncol), BF16),
        scratch_shapes=[pltpu.VMEM((tm, d), BF16)],
        compiler_params=_cparams(("parallel", "arbitrary"), est),
    )(x2, g.reshape(1, d), mod, w)


def _rope128(y, tc, ts):
    return y * tc + pltpu.roll(y, 64, 1) * ts


def _mla_prep_kernel(ql_ref, kvl_ref, kr_ref, gq_ref, gkv_ref, wq_ref, wkv_ref, tc_ref, ts_ref,
                     q_ref, k_ref, v_ref):
    scale = (MLA_NOPE + MLA_ROPE) ** -0.5
    tc = tc_ref[...]
    ts = ts_ref[...]
    qn = _rms(ql_ref[...].astype(F32), gq_ref[...]).astype(BF16)
    yq = _dot(qn, wq_ref[...])
    kvn = _rms(kvl_ref[...].astype(F32), gkv_ref[...]).astype(BF16)
    ykv = _dot(kvn, wkv_ref[...])
    k_roped = _rope128(kr_ref[...].astype(F32), tc, ts).astype(BF16)
    for h in range(MLA_HEADS):
        o = 2 * LANES * h
        q_ref[:, o:o + LANES] = (yq[:, o:o + LANES] * scale).astype(BF16)
        q_ref[:, o + LANES:o + 2 * LANES] = (_rope128(yq[:, o + LANES:o + 2 * LANES], tc, ts) * scale).astype(BF16)
        k_ref[:, o:o + LANES] = ykv[:, LANES * h:LANES * (h + 1)].astype(BF16)
        k_ref[:, o + LANES:o + 2 * LANES] = k_roped
    v_ref[...] = ykv[:, MLA_HEADS * LANES:].astype(BF16)


def _mla_prep(proj, gq, gkv, wq, wkv, tc, ts, seq):
    n = proj.shape[0]
    tm = 512
    per_b = seq // tm
    hq = MLA_HEADS * 2 * LANES
    return pl.pallas_call(
        _mla_prep_kernel,
        grid=(n // tm,),
        in_specs=[pl.BlockSpec((tm, MLA_Q_RANK), lambda i: (i, P_QLAT // MLA_Q_RANK)),
                  pl.BlockSpec((tm, MLA_KV_RANK), lambda i: (i, P_KVLAT // MLA_KV_RANK)),
                  pl.BlockSpec((tm, LANES), lambda i: (i, P_KROPE // LANES)),
                  pl.BlockSpec((1, MLA_Q_RANK), lambda i: (0, 0)),
                  pl.BlockSpec((1, MLA_KV_RANK), lambda i: (0, 0)),
                  pl.BlockSpec((MLA_Q_RANK, hq), lambda i: (0, 0)),
                  pl.BlockSpec((MLA_KV_RANK, 2 * GROUP_WIDTH), lambda i: (0, 0)),
                  pl.BlockSpec((tm, LANES), lambda i: (i % per_b, 0)),
                  pl.BlockSpec((tm, LANES), lambda i: (i % per_b, 0))],
        out_specs=[pl.BlockSpec((tm, hq), lambda i: (i, 0)),
                   pl.BlockSpec((tm, hq), lambda i: (i, 0)),
                   pl.BlockSpec((tm, GROUP_WIDTH), lambda i: (i, 0))],
        out_shape=[jax.ShapeDtypeStruct((n, hq), BF16),
                   jax.ShapeDtypeStruct((n, hq), BF16),
                   jax.ShapeDtypeStruct((n, GROUP_WIDTH), BF16)],
        compiler_params=_cparams(("parallel",), 16 * 1024 * 1024),
    )(proj, proj, proj, gq.reshape(1, -1), gkv.reshape(1, -1), wq, wkv, tc, ts)


def _mla_flash_kernel(q_ref, k_ref, v_ref, o_ref):
    i = pl.program_id(2)
    tq = q_ref.shape[0]
    tk = tq
    q = q_ref[...]

    def step(j, carry, masked):
        m, l, acc = carry
        k = k_ref[pl.ds(pl.multiple_of(j * tk, tk), tk), :]
        v = v_ref[pl.ds(pl.multiple_of(j * tk, tk), tk), :]
        s = _dot_nt(q, k)
        if masked:
            qc = lax.broadcasted_iota(jnp.int32, (tq, tk), 0) // CHUNK
            kc = lax.broadcasted_iota(jnp.int32, (tq, tk), 1) // CHUNK
            s = jnp.where(kc <= qc, s, NEG_BIG)
        m_new = jnp.maximum(m, jnp.max(s, axis=-1, keepdims=True))
        alpha = jnp.exp(m - m_new)
        p = jnp.exp(s - m_new)
        l = alpha * l + jnp.sum(p, axis=-1, keepdims=True)
        acc = alpha * acc + _dot(p.astype(BF16), v)
        return m_new, l, acc

    init = (jnp.full((tq, 1), NEG_BIG, F32), jnp.zeros((tq, 1), F32), jnp.zeros((tq, v_ref.shape[1]), F32))
    carry = lax.fori_loop(0, i // 2, lambda jj, c: step(2 * jj + 1, step(2 * jj, c, False), False), init)
    carry = lax.cond(lax.rem(i, 2) == 1, lambda c: step(i - 1, c, False), lambda c: c, carry)
    m, l, acc = step(i, carry, True)
    o_ref[...] = (acc / l).astype(o_ref.dtype)


def _mla_flash(q, k, v, batch, seq):
    n = q.shape[0]
    tq = ATT_TQ
    nq = seq // tq
    est = 2 * (seq * 2 * LANES * 2 + seq * LANES * 2) + 8 * tq * tq * 4
    return pl.pallas_call(
        _mla_flash_kernel,
        grid=(batch, MLA_HEADS, nq),
        in_specs=[pl.BlockSpec((tq, 2 * LANES), lambda b, h, i: (b * nq + i, h)),
                  pl.BlockSpec((seq, 2 * LANES), lambda b, h, i: (b, h)),
                  pl.BlockSpec((seq, LANES), lambda b, h, i: (b, h))],
        out_specs=pl.BlockSpec((tq, LANES), lambda b, h, i: (b * nq + i, h)),
        out_shape=jax.ShapeDtypeStruct((n, GROUP_WIDTH), BF16),
        compiler_params=_cparams(("parallel", "parallel", "arbitrary"), est),
    )(q, k, v)


def _sb_kernel(q_ref, k_ref, v_ref, o_ref):
    i = pl.program_id(2)
    tq = q_ref.shape[0]
    tk = tq
    sub = SB_SUB
    scale = q_ref.shape[1] ** -0.5
    q = q_ref[...]
    tri = (lax.broadcasted_iota(jnp.int32, (sub, sub), 0)
           > lax.broadcasted_iota(jnp.int32, (sub, sub), 1)).astype(BF16)

    def step(j, carry, masked):
        run, acc = carry
        k = k_ref[pl.ds(pl.multiple_of(j * tk, tk), tk), :]
        v = v_ref[pl.ds(pl.multiple_of(j * tk, tk), tk), :]
        u = _dot_nt(q, k) * (scale * LOG2_E)
        log_beta = jnp.minimum(u, 0.0) - jnp.log2(1.0 + jnp.exp2(-jnp.abs(u)))
        log_keep = log_beta - u
        if masked:
            vis = (lax.broadcasted_iota(jnp.int32, (tq, tk), 1)
                   < lax.broadcasted_iota(jnp.int32, (tq, tk), 0))
            log_keep = jnp.where(vis, log_keep, 0.0)
        keep_b = log_keep.astype(BF16)
        pieces = []
        for blk in range(tk // sub - 1, -1, -1):
            sl = slice(blk * sub, (blk + 1) * sub)
            pieces.append(_dot(keep_b[:, sl], tri) + run)
            run = run + jnp.sum(log_keep[:, sl], axis=-1, keepdims=True)
        later = jnp.concatenate(pieces[::-1], axis=1)
        att = jnp.exp2(log_beta + later)
        if masked:
            att = jnp.where(vis, att, 0.0)
        acc = acc + _dot(att.astype(BF16), v)
        return run, acc

    carry = step(i, (jnp.zeros((tq, 1), F32), jnp.zeros((tq, v_ref.shape[1]), F32)), True)
    carry = lax.fori_loop(0, i // 2, lambda jj, c: step(i - 2 - 2 * jj, step(i - 1 - 2 * jj, c, False), False), carry)
    run, acc = lax.cond(lax.rem(i, 2) == 1, lambda c: step(0, c, False), lambda c: c, carry)
    o_ref[...] = acc.astype(o_ref.dtype)


def _sb_attention(proj, batch, seq):
    n = proj.shape[0]
    tq = ATT_TQ
    nq = seq // tq
    cq, ck, cv = P_C // LANES, (P_C + GROUP_WIDTH) // LANES, (P_C + 2 * GROUP_WIDTH) // LANES
    est = 4 * seq * LANES * 2 + 10 * tq * tq * 4
    return pl.pallas_call(
        _sb_kernel,
        grid=(batch, SB_HEADS, nq),
        in_specs=[pl.BlockSpec((tq, LANES), lambda b, h, i: (b * nq + i, cq + h)),
                  pl.BlockSpec((seq, LANES), lambda b, h, i: (b, ck + h)),
                  pl.BlockSpec((seq, LANES), lambda b, h, i: (b, cv + h))],
        out_specs=pl.BlockSpec((tq, LANES), lambda b, h, i: (b * nq + i, h)),
        out_shape=jax.ShapeDtypeStruct((n, GROUP_WIDTH), BF16),
        compiler_params=_cparams(("parallel", "parallel", "arbitrary"), est),
    )(proj, proj, proj)


def _ca_kernel(q_ref, k_ref, v_ref, tab_ref, o_ref, kpad_ref, vpad_ref):
    i = pl.program_id(2)
    tq = q_ref.shape[0]
    past = CA_PAST_CHUNKS * CHUNK
    win = tq + past
    scale = q_ref.shape[1] ** -0.5

    @pl.when(i == 0)
    def _():
        kpad_ref[0:past, :] = jnp.zeros((past, kpad_ref.shape[1]), kpad_ref.dtype)
        vpad_ref[0:past, :] = jnp.zeros((past, vpad_ref.shape[1]), vpad_ref.dtype)
        kpad_ref[past:, :] = k_ref[...]
        vpad_ref[past:, :] = v_ref[...]

    start = pl.multiple_of(i * tq, tq)
    kw = kpad_ref[pl.ds(start, win), :]
    vw = vpad_ref[pl.ds(start, win), :]
    s = _dot_nt(q_ref[...], kw) * scale + tab_ref[0, 0]
    m = jnp.max(s, axis=-1, keepdims=True)
    p = jnp.exp(s - m)
    l = jnp.sum(p, axis=-1, keepdims=True)
    o_ref[...] = (_dot(p.astype(BF16), vw) / l).astype(o_ref.dtype)


def _ca_table(rel_bias, tq):
    past = CA_PAST_CHUNKS * CHUNK
    win = tq + past
    r = jnp.arange(tq)[:, None]
    c = jnp.arange(win)[None, :]
    ring = 1024
    assert ring >= tq + win - 1
    m = jnp.arange(ring)
    m = jnp.where(m >= win, m - ring, m)
    idx = jnp.clip(past - m, -(CHUNK - 1), CA_REL_PAST) + (CHUNK - 1)
    ringvals = rel_bias[:, idx].astype(F32)
    nh = rel_bias.shape[0]
    bias = jnp.tile(ringvals, (1, tq))[:, :tq * (ring - 1)].reshape(nh, tq, ring - 1)[:, :, :win]
    qc = r // CHUNK
    kc = c // CHUNK
    vis = (kc >= qc) & (kc <= qc + CA_PAST_CHUNKS)
    tabs = []
    n_var = past // tq + 1
    for v in range(n_var):
        first = (past - v * tq) // CHUNK if v < n_var - 1 else 0
        ok = vis & (kc >= first)
        tabs.append(jnp.where(ok[None], bias, NEG_BIG))
    return jnp.stack(tabs)


def _ca_attention(proj, table, batch, seq):
    n = proj.shape[0]
    tq = CA_TQ
    nq = seq // tq
    past = CA_PAST_CHUNKS * CHUNK
    win = tq + past
    n_var = table.shape[0]
    cq, ck, cv = P_D // LANES, (P_D + GROUP_WIDTH) // LANES, (P_D + 2 * GROUP_WIDTH) // LANES
    est = 4 * seq * LANES * 2 + 2 * (seq + past) * LANES * 2 + 2 * tq * win * 4 + 6 * tq * win * 4
    return pl.pallas_call(
        _ca_kernel,
        grid=(batch, CA_HEADS, nq),
        in_specs=[pl.BlockSpec((tq, LANES), lambda b, h, i: (b * nq + i, cq + h)),
                  pl.BlockSpec((seq, LANES), lambda b, h, i: (b, ck + h)),
                  pl.BlockSpec((seq, LANES), lambda b, h, i: (b, cv + h)),
                  pl.BlockSpec((1, 1, tq, win), lambda b, h, i: (jnp.minimum(i, n_var - 1), h, 0, 0))],
        out_specs=pl.BlockSpec((tq, LANES), lambda b, h, i: (b * nq + i, h)),
        out_shape=jax.ShapeDtypeStruct((n, GROUP_WIDTH), BF16),
        scratch_shapes=[pltpu.VMEM((seq + past, LANES), BF16), pltpu.VMEM((seq + past, LANES), BF16)],
        compiler_params=_cparams(("parallel", "parallel", "arbitrary"), est),
    )(proj, proj, proj, table)


def _ssd_kernel(z_ref, xbc_ref, dt_ref, cw_ref, cb_ref, dtb_ref, alog_ref, dskip_ref, ng_ref, ex_ref,
                o_ref, xbuf_ref, state_ref, y_ref):
    c = pl.program_id(1)
    lc = z_ref.shape[0]
    halo = SUBLANES
    gw = GROUP_WIDTH // SSM_GROUPS
    hpg = SSM_HEADS // SSM_GROUPS

    @pl.when(c == 0)
    def _():
        xbuf_ref[0:halo, :] = jnp.zeros((halo, xbuf_ref.shape[1]), F32)
        state_ref[...] = jnp.zeros(state_ref.shape, F32)

    xbuf_ref[halo:halo + lc, :] = xbc_ref[...].astype(F32)
    acc = jnp.broadcast_to(cb_ref[...], (lc, SSM_CONV_DIM))
    for k in range(SSM_CONV):
        acc = acc + cw_ref[k:k + 1, :] * xbuf_ref[pl.ds(halo - (SSM_CONV - 1) + k, lc), :]
    xbuf_ref[0:halo, :] = xbuf_ref[lc:lc + halo, :]
    u = acc * jax.nn.sigmoid(acc)
    xs = u[:, :GROUP_WIDTH]
    bm = u[:, GROUP_WIDTH:GROUP_WIDTH + SSM_GROUPS * SSM_STATE].astype(BF16)
    cm = u[:, GROUP_WIDTH + SSM_GROUPS * SSM_STATE:].astype(BF16)

    dt = jax.nn.softplus(dt_ref[...].astype(F32) + dtb_ref[...])
    a = -jnp.exp(alog_ref[...])
    da = dt * a
    row = lax.broadcasted_iota(jnp.int32, (lc, lc), 0)
    col = lax.broadcasted_iota(jnp.int32, (lc, lc), 1)
    lower = col <= row
    cs = _dot3_left01(lower.astype(BF16), da)
    cs_last = cs[lc - 1:lc, :]
    cs_t = cs.T
    dt_t = dt.T
    ex = ex_ref[...]
    ecs_x = _dot3_right01(jnp.exp(cs), ex)
    wdec_x = _dot3_right01(jnp.exp(cs_last - cs) * dt, ex)
    xs_b = xs.astype(BF16)

    for g in range(SSM_GROUPS):
        bg = bm[:, g * SSM_STATE:(g + 1) * SSM_STATE]
        cg = cm[:, g * SSM_STATE:(g + 1) * SSM_STATE]
        cb = _dot_nt(cg, bg)
        for r in range(hpg):
            h = g * hpg + r
            diff = cs[:, h:h + 1] - cs_t[h:h + 1, :]
            wgt = cb * jnp.exp(jnp.where(lower, diff, NEG_BIG)) * dt_t[h:h + 1, :]
            y_ref[:, h * SSM_HEAD_DIM:(h + 1) * SSM_HEAD_DIM] = _dot(
                wgt.astype(BF16), xs_b[:, h * SSM_HEAD_DIM:(h + 1) * SSM_HEAD_DIM])
        lo, hi = g * gw, (g + 1) * gw
        st = state_ref[g]
        y_off = _dot(cg, st.astype(BF16)) * ecs_x[:, lo:hi]
        y_ref[:, lo:hi] += y_off
        xw = (xs[:, lo:hi] * wdec_x[:, lo:hi]).astype(BF16)
        state_ref[g] = st * ecs_x[lc - 1:lc, lo:hi] + _dot_tn(bg, xw)

    zf = z_ref[...].astype(F32)
    y = (y_ref[...] + xs * dskip_ref[...]) * (zf * jax.nn.sigmoid(zf))
    for g in range(SSM_GROUPS):
        lo, hi = g * gw, (g + 1) * gw
        yg = y[:, lo:hi]
        yg = yg * lax.rsqrt(jnp.mean(yg * yg, axis=-1, keepdims=True) + EPS)
        o_ref[:, lo:hi] = (yg * ng_ref[:, lo:hi]).astype(o_ref.dtype)


def _ssd(proj, conv_w, conv_b, dt_bias, a_log, d_skip, norm_g, batch, seq):
    n = proj.shape[0]
    lc = SSD_LC
    nc = seq // lc
    dtb = jnp.zeros((1, LANES), F32).at[0, :SSM_HEADS].set(dt_bias)
    alog = jnp.full((1, LANES), NEG_BIG, F32).at[0, :SSM_HEADS].set(a_log)
    dskip = jnp.repeat(d_skip, SSM_HEAD_DIM).reshape(1, GROUP_WIDTH)
    ex = (jnp.arange(LANES)[:, None] == jnp.arange(GROUP_WIDTH)[None, :] // SSM_HEAD_DIM).astype(BF16)
    full = lambda shape: pl.BlockSpec(shape, lambda b, c: tuple(0 for _ in shape))
    return pl.pallas_call(
        _ssd_kernel,
        grid=(batch, nc),
        in_specs=[pl.BlockSpec((lc, GROUP_WIDTH), lambda b, c: (b * nc + c, P_Z // GROUP_WIDTH)),
                  pl.BlockSpec((lc, SSM_CONV_DIM), lambda b, c: (b * nc + c, P_XBC // SSM_CONV_DIM)),
                  pl.BlockSpec((lc, LANES), lambda b, c: (b * nc + c, P_DT // LANES)),
                  full((SSM_CONV, SSM_CONV_DIM)), full((1, SSM_CONV_DIM)), full((1, LANES)), full((1, LANES)),
                  full((1, GROUP_WIDTH)), full((1, GROUP_WIDTH)), full((LANES, GROUP_WIDTH))],
        out_specs=pl.BlockSpec((lc, GROUP_WIDTH), lambda b, c: (b * nc + c, 0)),
        out_shape=jax.ShapeDtypeStruct((n, GROUP_WIDTH), BF16),
        scratch_shapes=[pltpu.VMEM((lc + SUBLANES, SSM_CONV_DIM), F32),
                        pltpu.VMEM((SSM_GROUPS, SSM_STATE, GROUP_WIDTH // SSM_GROUPS), F32),
                        pltpu.VMEM((lc, GROUP_WIDTH), F32)],
        compiler_params=_cparams(("parallel", "arbitrary"), 16 * 1024 * 1024),
    )(proj, proj, proj, conv_w, conv_b.reshape(1, -1), dtb, alog, dskip, norm_g.reshape(1, -1), ex)


def _outproj_kernel(a_ref, b_ref, c_ref, d_ref, x_ref, gmix_ref, w_ref, mod_ref, gf_ref, rw_ref, rb_ref,
                    xo_ref, h_ref, e_ref, gt_ref):
    gw = GROUP_WIDTH
    a = _rms(a_ref[...].astype(F32), gmix_ref[0:1, :]).astype(BF16)
    c = _rms(c_ref[...].astype(F32), gmix_ref[1:2, :]).astype(BF16)
    d = _rms(d_ref[...].astype(F32), gmix_ref[2:3, :]).astype(BF16)
    mixed = (_dot(a, w_ref[0:gw, :]) + _dot(b_ref[...], w_ref[gw:2 * gw, :])
             + _dot(c, w_ref[2 * gw:3 * gw, :]) + _dot(d, w_ref[3 * gw:, :]))
    x = x_ref[...] + mod_ref[0, 2:3, :] * mixed
    xo_ref[...] = x
    h = _rms(x, gf_ref[...]) * (1.0 + mod_ref[0, 4:5, :]) + mod_ref[0, 3:4, :]
    _store_packed_rows(h_ref, h)

    hh = h.astype(BF16)
    hl = (h - hh.astype(F32)).astype(BF16)
    rw = rw_ref[...]
    wh = rw.astype(BF16)
    wl = (rw - wh.astype(F32)).astype(BF16)
    logits = _dot(hh, wh) + _dot(hh, wl) + _dot(hl, wh) + rb_ref[...]
    lane = lax.broadcasted_iota(jnp.int32, logits.shape, 1)
    vals, idxs = [], []
    for _ in range(TOP_K):
        m = jnp.max(logits, axis=-1, keepdims=True)
        idx = jnp.min(jnp.where(logits == m, lane, LANES), axis=-1, keepdims=True)
        vals.append(m)
        idxs.append(idx)
        logits = jnp.where(lane == idx, -jnp.inf, logits)
    ex = [jnp.exp(v - vals[0]) for v in vals]
    tot = ex[0] + ex[1] + ex[2] + ex[3]
    e_out = jnp.zeros(logits.shape, jnp.int32)
    g_out = jnp.zeros(logits.shape, F32)
    for k in range(TOP_K):
        e_out = jnp.where(lane == k, idxs[k], e_out)
        g_out = jnp.where(lane == k, ex[k] / tot, g_out)
    e_ref[...] = e_out
    gt_ref[...] = g_out


def _outproj(oa, ob, oc, od, x2, gmix, w_out, mod, gf, rw, rb, seq):
    n, d = x2.shape
    tm = 256
    per_b = seq // tm
    gw = GROUP_WIDTH
    row = lambda width: pl.BlockSpec((tm, width), lambda i: (i, 0))
    full = lambda shape: pl.BlockSpec(shape, lambda i: tuple(0 for _ in shape))
    est = 2 * d * d * 2 + 4 * tm * d * 4 + 2 * tm * d * 2 + 8 * tm * gw * 2 + 6 * tm * d * 4
    return pl.pallas_call(
        _outproj_kernel,
        grid=(n // tm,),
        in_specs=[row(gw), row(gw), row(gw), row(gw), row(d), full((3, gw)), full((d, d)),
                  pl.BlockSpec((1, 6, d), lambda i: (i // per_b, 0, 0)), full((1, d)),
                  full((d, LANES)), full((1, LANES))],
        out_specs=[row(d), pl.BlockSpec((tm * ROW_SUBLANES, LANES), lambda i: (i, 0)), row(LANES), row(LANES)],
        out_shape=[jax.ShapeDtypeStruct((n, d), F32), jax.ShapeDtypeStruct((n * ROW_SUBLANES, LANES), U32),
                   jax.ShapeDtypeStruct((n, LANES), jnp.int32), jax.ShapeDtypeStruct((n, LANES), F32)],
        compiler_params=_cparams(("parallel",), est),
    )(oa, ob, oc, od, x2, gmix, w_out, mod, gf.reshape(1, d), rw, rb)


def _row_copy(src_ref, src_row, dst_ref, dst_row, sem):
    rs = ROW_SUBLANES
    return pltpu.make_async_copy(src_ref.at[pl.ds(pl.multiple_of(src_row * rs, rs), rs)],
                                 dst_ref.at[pl.ds(pl.multiple_of(dst_row * rs, rs), rs)], sem)


def _expert_kernel(be_ref, nu_ref, tok_first_ref, step_ref, dst_last_ref, hp_ref, w1_ref, b1_ref, w2_ref,
                   b2_ref, y_ref, xa_ref, xb_ref, ya_ref, yb_ref, gsem, ssem):
    i = pl.program_id(0)
    nu = nu_ref[0]
    tm = MOE_BLOCK
    rs = ROW_SUBLANES

    def start_rows(copy_of_row):
        for r in range(tm):
            copy_of_row(r).start(priority=r % 2)

    def wait_rows(src_ref, dst_ref, sem):
        for _ in range(tm):
            _row_copy(src_ref, 0, dst_ref, 0, sem).wait()

    @pl.when(i == 0)
    def _():
        start_rows(lambda r: _row_copy(hp_ref, tok_first_ref[0, 0, r], xa_ref, r, gsem.at[0]))
        yb_ref[...] = jnp.zeros(yb_ref.shape, U32)
        spare = pltpu.make_async_copy(yb_ref, y_ref.at[pl.ds(y_ref.shape[0] - 2 * tm * rs, tm * rs)], ssem.at[0])
        spare.start()
        spare.wait()

    def body(xcur, xnext, ycur, yprev, gcur, gnext, scur, sprev):
        wait_rows(hp_ref, xcur, gcur)
        start_rows(lambda r: _row_copy(hp_ref, step_ref[0, 0, r], xnext, r, gnext))
        start_rows(lambda r: _row_copy(yprev, r, y_ref, step_ref[0, 0, tm + r], sprev))
        words = _load_packed_rows(xcur)
        x = jnp.concatenate([_unpack_lo(u).astype(BF16) for u in words]
                            + [_unpack_hi(u).astype(BF16) for u in words], axis=1)
        cw = EXPERT_COL_CHUNK
        hu = jnp.concatenate([_dot(x, w1_ref[0, :, c:c + cw].astype(BF16)) for c in range(0, w1_ref.shape[2], cw)],
                             axis=1) + b1_ref[0]
        gate = jnp.minimum(hu[:, :D_EXPERT], SWIGLU_LIMIT)
        up = jnp.clip(hu[:, D_EXPERT:], -SWIGLU_LIMIT, SWIGLU_LIMIT)
        act = (gate * jax.nn.sigmoid(SWIGLU_ALPHA * gate) * (up + 1.0)).astype(BF16)
        y = jnp.concatenate([_dot(act, w2_ref[0, :, c:c + cw].astype(BF16)) for c in range(0, w2_ref.shape[2], cw)],
                            axis=1) + b2_ref[0]

        @pl.when(i > 0)
        def _():
            wait_rows(ycur, y_ref, scur)

        _store_packed_rows(ycur, y)

        @pl.when(i == nu - 1)
        def _():
            start_rows(lambda r: _row_copy(ycur, r, y_ref, dst_last_ref[0, 0, r], scur))
            wait_rows(yprev, y_ref, sprev)
            wait_rows(ycur, y_ref, scur)
            wait_rows(hp_ref, xnext, gnext)

    even = lax.rem(i, 2) == 0

    @pl.when(jnp.logical_and(i < nu, even))
    def _():
        body(xa_ref, xb_ref, ya_ref, yb_ref, gsem.at[0], gsem.at[1], ssem.at[0], ssem.at[1])

    @pl.when(jnp.logical_and(i < nu, jnp.logical_not(even)))
    def _():
        body(xb_ref, xa_ref, yb_ref, ya_ref, gsem.at[1], gsem.at[0], ssem.at[1], ssem.at[0])


def _experts(hp, row_tok, row_dst, block_e, n_used, n_slots, w1, b1, w2, b2, layer=0):
    tm = MOE_BLOCK
    nblk = row_tok.shape[0]
    d, de2 = w1.shape[1], w1.shape[2]
    de = w2.shape[1]
    rs = ROW_SUBLANES
    block_e = block_e + layer * N_EXPERTS
    last = lambda i, nu: jnp.minimum(i, nu[0] - 1)
    est = (2 * (d * de2 + de * d) * w1.dtype.itemsize + 4 * tm * rs * LANES * 4 + 4 * tm * d * 4
           + 3 * tm * de2 * 4)
    first_prev = (n_slots - tm + jnp.arange(tm, dtype=jnp.int32)).reshape(1, 1, tm)
    step_rows = jnp.concatenate([jnp.concatenate([row_tok[1:], row_tok[-1:]], axis=0),
                                 jnp.concatenate([first_prev, row_dst[:-1]], axis=0)], axis=2)
    dst_last = lax.dynamic_slice_in_dim(row_dst, n_used[0] - 1, 1, axis=0)
    smem = lambda width, f: pl.BlockSpec((1, 1, width), f, memory_space=pltpu.SMEM)
    grid_spec = pltpu.PrefetchScalarGridSpec(
        num_scalar_prefetch=2,
        grid=(nblk,),
        in_specs=[smem(tm, lambda i, be, nu: (0, 0, 0)),
                  smem(2 * tm, lambda i, be, nu: (last(i, nu), 0, 0)),
                  smem(tm, lambda i, be, nu: (0, 0, 0)),
                  pl.BlockSpec(memory_space=pl.ANY),
                  pl.BlockSpec((1, d, de2), lambda i, be, nu: (be[last(i, nu)], 0, 0)),
                  pl.BlockSpec((1, 1, de2), lambda i, be, nu: (be[last(i, nu)], 0, 0)),
                  pl.BlockSpec((1, de, d), lambda i, be, nu: (be[last(i, nu)], 0, 0)),
                  pl.BlockSpec((1, 1, d), lambda i, be, nu: (be[last(i, nu)], 0, 0))],
        out_specs=pl.BlockSpec(memory_space=pl.ANY),
        scratch_shapes=[pltpu.VMEM((tm * rs, LANES), U32) for _ in range(4)]
        + [pltpu.SemaphoreType.DMA((2,)), pltpu.SemaphoreType.DMA((2,))],
    )
    return pl.pallas_call(
        _expert_kernel,
        grid_spec=grid_spec,
        out_shape=jax.ShapeDtypeStruct((n_slots * rs, LANES), U32),
        compiler_params=_cparams(("arbitrary",), est),
    )(block_e, n_used, row_tok, step_rows, dst_last, hp, w1, b1.reshape(b1.shape[0], 1, de2), w2,
      b2.reshape(b2.shape[0], 1, d))


def _combine_kernel(y0_ref, y1_ref, y2_ref, y3_ref, g_ref, x_ref, mod_ref, o_ref):
    tm, d = x_ref.shape
    half = d // 2
    rs = ROW_SUBLANES
    rc = COMBINE_ROWS

    def chunk(c, carry):
        r0 = pl.multiple_of(c * rc, rc)
        rows = pl.ds(r0, rc)
        g = g_ref[rows, :]
        lo = [jnp.zeros((rc, LANES), F32) for _ in range(rs)]
        hi = [jnp.zeros((rc, LANES), F32) for _ in range(rs)]
        for k, y_ref in enumerate((y0_ref, y1_ref, y2_ref, y3_ref)):
            blk = y_ref[pl.ds(pl.multiple_of(r0 * rs, rc * rs), rc * rs), :]
            w = jnp.swapaxes(blk.reshape(rc // rs, rs, rs, LANES), 1, 2)
            gk = g[:, k:k + 1]
            for s in range(rs):
                u = w[:, s].reshape(rc, LANES)
                lo[s] = lo[s] + _unpack_lo(u) * gk
                hi[s] = hi[s] + _unpack_hi(u) * gk
        for s in range(rs):
            a, b = LANES * s, half + LANES * s
            o_ref[rows, a:a + LANES] = x_ref[rows, a:a + LANES] + mod_ref[0, 5:6, a:a + LANES] * lo[s]
            o_ref[rows, b:b + LANES] = x_ref[rows, b:b + LANES] + mod_ref[0, 5:6, b:b + LANES] * hi[s]
        return carry
    lax.fori_loop(0, tm // rc, chunk, 0)


def _combine(y4, gates, x2, mod, seq):
    n, d = x2.shape
    tm = 256
    per_b = seq // tm
    nt = n // tm
    rows = tm * ROW_SUBLANES
    est = 2 * TOP_K * rows * LANES * 4 + 4 * tm * d * 4 + 2 * tm * LANES * 4
    yspec = lambda k: pl.BlockSpec((rows, LANES), lambda i: (k * nt + i, 0))
    return pl.pallas_call(
        _combine_kernel,
        grid=(nt,),
        in_specs=[yspec(0), yspec(1), yspec(2), yspec(3),
                  pl.BlockSpec((tm, LANES), lambda i: (i, 0)),
                  pl.BlockSpec((tm, d), lambda i: (i, 0)),
                  pl.BlockSpec((1, 6, d), lambda i: (i // per_b, 0, 0))],
        out_specs=pl.BlockSpec((tm, d), lambda i: (i, 0)),
        out_shape=jax.ShapeDtypeStruct((n, d), F32),
        compiler_params=_cparams(("parallel",), est),
    )(y4, y4, y4, y4, gates, x2, mod)


def _final_norm_kernel(x_ref, g_ref, o_ref):
    o_ref[...] = _rms(x_ref[...], g_ref[...])


def _final_norm(x2, g):
    n, d = x2.shape
    tm = 512
    return pl.pallas_call(
        _final_norm_kernel,
        grid=(n // tm,),
        in_specs=[pl.BlockSpec((tm, d), lambda i: (i, 0)), pl.BlockSpec((1, d), lambda i: (0, 0))],
        out_specs=pl.BlockSpec((tm, d), lambda i: (i, 0)),
        out_shape=jax.ShapeDtypeStruct((n, d), F32),
        compiler_params=_cparams(("parallel",), 4 * tm * d * 4),
    )(x2, g.reshape(1, d))


def _swap_halves(w):
    half = w.shape[-1] // 2
    return jnp.concatenate([w[..., half:], w[..., :half]], axis=-1)


def _pack_w_in_kernel(w_ref, o_ref):
    w = w_ref[0]
    b0, c0 = A_COLS, A_COLS + B_COLS
    kr0 = MLA_Q_RANK + MLA_KV_RANK
    half = MLA_ROPE // 2
    pieces = [w[:, :MLA_Q_RANK],
              w[:, b0:b0 + GROUP_WIDTH + SSM_CONV_DIM],
              w[:, MLA_Q_RANK:kr0 + MLA_ROPE],
              w[:, kr0 + half:kr0 + MLA_ROPE], w[:, kr0:kr0 + half],
              w[:, c0 - SSM_HEADS:c0], jnp.zeros((w.shape[0], LANES - SSM_HEADS), w.dtype),
              w[:, c0:]]
    o_ref[0] = jnp.concatenate(pieces, axis=1).astype(BF16)


def _pack_w_in(w_in):
    nl, d, nc = w_in.shape
    tr = 256
    est = 2 * tr * nc * 4 + 2 * tr * P_COLS * 2 + 2 * tr * P_COLS * 4
    return pl.pallas_call(
        _pack_w_in_kernel,
        grid=(nl, d // tr),
        in_specs=[pl.BlockSpec((1, tr, nc), lambda l, i: (l, i, 0))],
        out_specs=pl.BlockSpec((1, tr, P_COLS), lambda l, i: (l, i, 0)),
        out_shape=jax.ShapeDtypeStruct((nl, d, P_COLS), BF16),
        compiler_params=_cparams(("parallel", "parallel"), est),
    )(w_in)


def _pack_wq(wq):
    r = wq.shape[0]
    w = wq.reshape(r, MLA_HEADS, MLA_NOPE + MLA_ROPE)
    rope = w[..., MLA_NOPE:]
    return jnp.concatenate([w[..., :MLA_NOPE], rope, _swap_halves(rope)], axis=-1).reshape(r, -1).astype(BF16)


def _pack_wkv(wkv):
    r = wkv.shape[0]
    w = wkv.reshape(r, MLA_HEADS, MLA_NOPE + MLA_V)
    return jnp.concatenate([w[..., :MLA_NOPE].reshape(r, -1), w[..., MLA_NOPE:].reshape(r, -1)], axis=-1).astype(BF16)


def _rope_tables(seq):
    inv = 1.0 / (ROPE_THETA ** (jnp.arange(0, MLA_ROPE, 2, dtype=F32) / MLA_ROPE))
    ang = jnp.arange(seq, dtype=F32)[:, None] * inv[None, :]
    cos, sin = jnp.cos(ang), jnp.sin(ang)
    zero = jnp.zeros((seq, LANES - MLA_ROPE), F32)
    return jnp.concatenate([cos, cos, zero], axis=1), jnp.concatenate([-sin, sin, zero], axis=1)


def _route(top_e, n_tok):
    n_assign = n_tok * TOP_K
    n_blocks = -(-(n_assign + N_EXPERTS * (MOE_BLOCK - 1)) // MOE_BLOCK)
    n_rows = n_blocks * MOE_BLOCK
    dest, meta = _route_dest(top_e)
    padded_end = meta[2, :N_EXPERTS]
    block_start = jnp.arange(n_blocks, dtype=jnp.int32) * MOE_BLOCK
    block_e = jnp.minimum(jnp.sum((padded_end[None, :] <= block_start[:, None]).astype(jnp.int32), axis=1),
                          N_EXPERTS - 1)
    n_used = (padded_end[N_EXPERTS - 1] // MOE_BLOCK).reshape(1)
    row_assign = _route_invert(meta, dest[:, :TOP_K].reshape(n_assign), n_rows)
    real = row_assign >= 0
    row = jnp.arange(n_rows, dtype=jnp.int32)
    row_tok = jnp.where(real, row_assign // TOP_K, 0)
    spare = n_assign + (row // MOE_BLOCK % 2) * MOE_BLOCK + row % MOE_BLOCK
    row_dst = jnp.where(real, row_assign % TOP_K * n_tok + row_assign // TOP_K, spare)
    shape = (n_blocks, 1, MOE_BLOCK)
    return row_tok.reshape(shape), row_dst.reshape(shape), block_e, n_used, n_assign + 2 * MOE_BLOCK


NORM_ROWS = 16
ROUTE_TM = 512
COMBINE_ROWS = 16


def _route_dest_kernel(e_ref, dest_ref, meta_ref, carry_ref, base_ref):
    p = pl.program_id(0)
    i = pl.program_id(1)
    tm = e_ref.shape[0]
    e = e_ref[...]
    lane = lax.broadcasted_iota(jnp.int32, (tm, LANES), 1)
    hit = [lane == e[:, k:k + 1] for k in range(TOP_K)]
    tot = sum(h.astype(F32) for h in hit)

    @pl.when(jnp.logical_and(p == 0, i == 0))
    def _():
        carry_ref[...] = jnp.zeros(carry_ref.shape, F32)
        meta_ref[...] = jnp.zeros(meta_ref.shape, jnp.int32)

    @pl.when(jnp.logical_and(p == 0, i == pl.num_programs(1) - 1))
    def _():
        counts = carry_ref[...] + jnp.sum(tot, axis=0, keepdims=True)
        padded = jnp.floor((counts + (MOE_BLOCK - 1)) * (1.0 / MOE_BLOCK)) * MOE_BLOCK
        upper = (lax.broadcasted_iota(jnp.int32, (LANES, LANES), 0)
                 <= lax.broadcasted_iota(jnp.int32, (LANES, LANES), 1)).astype(BF16)
        padded_end = _dot3_right01(jnp.broadcast_to(padded, (SUBLANES, LANES)), upper)[0:1]
        base_ref[...] = padded_end - padded
        meta_ref[0:1, :] = counts.astype(jnp.int32)
        meta_ref[1:2, :] = (padded_end - padded).astype(jnp.int32)
        meta_ref[2:3, :] = padded_end.astype(jnp.int32)
        carry_ref[...] = -jnp.sum(tot, axis=0, keepdims=True)

    @pl.when(p == 1)
    def _():
        earlier = (lax.broadcasted_iota(jnp.int32, (tm, tm), 1)
                   < lax.broadcasted_iota(jnp.int32, (tm, tm), 0)).astype(BF16)
        pos = _dot(earlier, tot.astype(BF16)) + carry_ref[...] + base_ref[...]
        dest = jnp.zeros((tm, LANES), jnp.int32)
        for k in range(TOP_K):
            dk = jnp.sum(jnp.where(hit[k], pos, 0.0), axis=-1, keepdims=True)
            dest = jnp.where(lane == k, dk.astype(jnp.int32), dest)
        dest_ref[...] = dest

    carry_ref[...] += jnp.sum(tot, axis=0, keepdims=True)


def _route_dest(top_e):
    n = top_e.shape[0]
    tm = ROUTE_TM
    return pl.pallas_call(
        _route_dest_kernel,
        grid=(2, n // tm),
        in_specs=[pl.BlockSpec((tm, LANES), lambda p, i: (i, 0))],
        out_specs=[pl.BlockSpec((tm, LANES), lambda p, i: (i * p, 0)),
                   pl.BlockSpec((SUBLANES, LANES), lambda p, i: (0, 0))],
        out_shape=[jax.ShapeDtypeStruct((n, LANES), jnp.int32), jax.ShapeDtypeStruct((SUBLANES, LANES), jnp.int32)],
        scratch_shapes=[pltpu.VMEM((1, LANES), F32), pltpu.VMEM((1, LANES), F32)],
        compiler_params=_cparams(("arbitrary", "arbitrary"), 8 * tm * tm * 4),
    )(top_e)


ROUTE_CHUNK = 2048


def _route_invert_kernel(meta_ref, dest_ref, out_ref):
    s = pl.program_id(0)
    ch = dest_ref.shape[2]
    per = SUBLANES

    def clear(r, carry):
        out_ref[r] = -1
        return carry

    @pl.when(s == 0)
    def _():
        def expert_padding(e, carry):
            return lax.fori_loop(meta_ref[1, e] + meta_ref[0, e], meta_ref[2, e], clear, carry)
        lax.fori_loop(0, N_EXPERTS, expert_padding, 0)
        lax.fori_loop(meta_ref[2, N_EXPERTS - 1], out_ref.shape[0], clear, 0)

    def place(j, carry):
        for u in range(per):
            a = j * per + u
            out_ref[dest_ref[0, 0, a]] = s * ch + a
        return carry
    lax.fori_loop(0, ch // per, place, 0)


def _route_invert(meta, dest_flat, n_rows):
    ch = ROUTE_CHUNK
    n_place = dest_flat.shape[0] // ch
    assert n_place * ch == dest_flat.shape[0]
    return pl.pallas_call(
        _route_invert_kernel,
        grid=(n_place,),
        in_specs=[pl.BlockSpec(memory_space=pltpu.SMEM),
                  pl.BlockSpec((1, 1, ch), lambda s: (s, 0, 0), memory_space=pltpu.SMEM)],
        out_specs=pl.BlockSpec(memory_space=pltpu.SMEM),
        out_shape=jax.ShapeDtypeStruct((n_rows,), jnp.int32),
        compiler_params=pltpu.CompilerParams(dimension_semantics=("arbitrary",)),
    )(meta, dest_flat.reshape(n_place, 1, ch))


def kernel(x, c, attn_norm, ffn_norm, mod_w, mod_b, w_in, mla_q_norm, mla_w_q_up, mla_kv_norm, mla_w_kv_up,
           ssm_conv_w, ssm_conv_b, ssm_dt_bias, ssm_a_log, ssm_d, ssm_norm, ca_rel_bias, mix_out_norm, w_out,
           router_w, router_b, moe_w1, moe_b1, moe_w2, moe_b2, final_norm):
    batch, seq, d = x.shape
    n = batch * seq
    depth = w_in.shape[0]
    x2 = x.reshape(n, d)
    mod_all = _mod_all(c, mod_w, mod_b).reshape(depth, batch, 6, d)
    tc, ts = _rope_tables(seq)
    w_in_packed = _pack_w_in(w_in)
    for l in range(depth):
        mod = mod_all[l]
        proj = _inproj(x2, attn_norm[l], mod, w_in_packed, l, seq)
        q, k, v = _mla_prep(proj, mla_q_norm[l], mla_kv_norm[l], _pack_wq(mla_w_q_up[l]),
                            _pack_wkv(mla_w_kv_up[l]), tc, ts, seq)
        out_a = _mla_flash(q, k, v, batch, seq)
        out_b = _ssd(proj, ssm_conv_w[l], ssm_conv_b[l], ssm_dt_bias[l], ssm_a_log[l], ssm_d[l], ssm_norm[l],
                     batch, seq)
        out_c = _sb_attention(proj, batch, seq)
        out_d = _ca_attention(proj, _ca_table(ca_rel_bias[l], CA_TQ), batch, seq)
        rw = jnp.zeros((d, LANES), F32).at[:, :N_EXPERTS].set(router_w[l])
        rb = jnp.full((1, LANES), NEG_BIG, F32).at[0, :N_EXPERTS].set(router_b[l])
        x2, hp, top_e, gates = _outproj(out_a, out_b, out_c, out_d, x2, mix_out_norm[l], w_out[l].astype(BF16),
                                        mod, ffn_norm[l], rw, rb, seq)
        row_tok, row_dst, block_e, n_used, n_slots = _route(top_e, n)
        y4 = _experts(hp, row_tok, row_dst, block_e, n_used, n_slots,
                      moe_w1.reshape(-1, *moe_w1.shape[2:]), moe_b1.reshape(-1, moe_b1.shape[2]),
                      moe_w2.reshape(-1, *moe_w2.shape[2:]), moe_b2.reshape(-1, moe_b2.shape[2]), layer=l)
        x2 = _combine(y4, gates, x2, mod, seq)
    return _final_norm(x2, final_norm).reshape(batch, seq, d)
```

```python
import functools
import math

import jax
import jax.numpy as jnp
from jax import lax
from jax.experimental import pallas as pl
from jax.experimental.pallas import tpu as pltpu

F32 = jnp.float32
BF16 = jnp.bfloat16
U32 = jnp.uint32

D_MODEL = 2048
DEPTH = 4
CHUNK = 64
EPS = 1e-6
GROUP_WIDTH = 512
MLA_HEADS = 4
MLA_NOPE = 128
MLA_ROPE = 64
MLA_V = 128
MLA_Q_RANK = 512
MLA_KV_RANK = 256
ROPE_THETA = 10000.0
SSM_HEAD_DIM = 64
SSM_HEADS = 8
SSM_GROUPS = 2
SSM_STATE = 128
SSM_CONV = 4
SSM_CONV_DIM = 1024
SB_HEADS = 4
CA_HEADS = 4
CA_PAST_CHUNKS = 8
CA_REL_PAST = 256
N_EXPERTS = 32
TOP_K = 4
D_EXPERT = 768
SWIGLU_ALPHA = 1.702
SWIGLU_LIMIT = 7.0

A_COLS = MLA_Q_RANK + MLA_KV_RANK + MLA_ROPE
B_COLS = GROUP_WIDTH + SSM_CONV_DIM + SSM_HEADS
C_COLS = 3 * GROUP_WIDTH

LANES = 128
SUBLANES = 8
VMEM_BYTES_V7X = 64 * 1024 * 1024
NEG_BIG = -1e30
LOG2_E = math.log2(math.e)

P_QLAT = 0
P_Z = 512
P_XBC = 1024
P_KVLAT = 2048
P_KROPE = 2304
P_DT = 2432
P_C = 2560
P_D = 4096
P_COLS = 5632

MOE_BLOCK = 256
EXPERT_COL_CHUNK = 256
ATT_TQ = 512
SB_SUB = 256
ROW_SUBLANES = 8
CA_TQ = 256
CA_WIN = CA_TQ + CA_PAST_CHUNKS * CHUNK
SSD_LC = 256


def _vmem_limit(nbytes):
    return int(min(max(2 * nbytes, 16 * 1024 * 1024), VMEM_BYTES_V7X - 8 * 1024 * 1024))


def _cparams(sem, nbytes):
    return pltpu.CompilerParams(dimension_semantics=sem, vmem_limit_bytes=_vmem_limit(nbytes))


def _rms(x, g):
    return x * lax.rsqrt(jnp.mean(x * x, axis=-1, keepdims=True) + EPS) * g


def _split3(x):
    hi = x.astype(BF16)
    r1 = x - hi.astype(F32)
    mid = r1.astype(BF16)
    lo = (r1 - mid.astype(F32)).astype(BF16)
    return hi, mid, lo


def _dot(a, b):
    return jnp.dot(a, b, preferred_element_type=F32)


def _dot_nt(a, b):
    return lax.dot_general(a, b, (((1,), (1,)), ((), ())), preferred_element_type=F32)


def _dot_tn(a, b):
    return lax.dot_general(a, b, (((0,), (0,)), ((), ())), preferred_element_type=F32)


def _dot3_left01(t01, x):
    hi, mid, lo = _split3(x)
    return _dot(t01, hi) + _dot(t01, mid) + _dot(t01, lo)


def _dot3_right01(x, t01):
    hi, mid, lo = _split3(x)
    return _dot(hi, t01) + _dot(mid, t01) + _dot(lo, t01)


def _bf16_bits(x):
    u = lax.bitcast_convert_type(x, U32)
    return (u + jnp.uint32(0x7FFF) + ((u >> 16) & jnp.uint32(1))) >> 16


def _store_packed_rows(ref, y):
    tm, d = y.shape
    half = d // 2
    rs = ROW_SUBLANES
    words = []
    for s in range(rs):
        lo = _bf16_bits(y[:, LANES * s:LANES * (s + 1)])
        hi = _bf16_bits(y[:, half + LANES * s:half + LANES * (s + 1)])
        words.append(((hi << 16) | lo).reshape(tm // rs, rs, LANES))
    ref[...] = jnp.swapaxes(jnp.stack(words, axis=1), 1, 2).reshape(tm * rs, LANES)


def _load_packed_rows(ref):
    rs = ROW_SUBLANES
    tm = ref.shape[0] // rs
    w = jnp.swapaxes(ref[...].reshape(tm // rs, rs, rs, LANES), 1, 2)
    return [w[:, s].reshape(tm, LANES) for s in range(rs)]


def _unpack_lo(u):
    return lax.bitcast_convert_type(u << 16, F32)


def _unpack_hi(u):
    return lax.bitcast_convert_type(u & jnp.uint32(0xFFFF0000), F32)


def _mod_kernel(ct_ref, w_ref, b_ref, o_ref):
    k = pl.program_id(2)
    nb = o_ref.shape[1]

    @pl.when(k == 0)
    def _():
        o_ref[0] = jnp.broadcast_to(b_ref[0], o_ref.shape[1:])

    ct = ct_ref[...]
    cond = ct * jax.nn.sigmoid(ct)
    w = w_ref[0]
    for b in range(nb):
        o_ref[0, b:b + 1, :] += jnp.sum(w * cond[:, b:b + 1], axis=0, keepdims=True)


def _mod_all(c, mod_w, mod_b):
    nb, d = c.shape
    nl, _, nout = mod_w.shape
    tk, tn = 512, 2048
    ct = c.T
    return pl.pallas_call(
        _mod_kernel,
        grid=(nl, nout // tn, d // tk),
        in_specs=[pl.BlockSpec((tk, nb), lambda l, j, k: (k, 0)),
                  pl.BlockSpec((1, tk, tn), lambda l, j, k: (l, k, j)),
                  pl.BlockSpec((1, 1, tn), lambda l, j, k: (l, 0, j))],
        out_specs=pl.BlockSpec((1, nb, tn), lambda l, j, k: (l, 0, j)),
        out_shape=jax.ShapeDtypeStruct((nl, nb, nout), F32),
        compiler_params=_cparams(("parallel", "parallel", "arbitrary"), 2 * tk * tn * 4),
    )(ct, mod_w, mod_b.reshape(nl, 1, nout))


def _inproj_kernel(x_ref, g_ref, mod_ref, w_ref, o_ref, h_ref):
    @pl.when(pl.program_id(1) == 0)
    def _():
        rc = NORM_ROWS
        gain = g_ref[...] * (1.0 + mod_ref[0, 1:2, :])
        shift = mod_ref[0, 0:1, :]

        def chunk(c, carry):
            rows = pl.ds(pl.multiple_of(c * rc, rc), rc)
            x = x_ref[rows, :]
            r = lax.rsqrt(jnp.mean(x * x, axis=-1, keepdims=True) + EPS)
            h_ref[rows, :] = (x * r * gain + shift).astype(BF16)
            return carry
        lax.fori_loop(0, x_ref.shape[0] // rc, chunk, 0, unroll=4)

    o_ref[...] = _dot(h_ref[...], w_ref[0]).astype(o_ref.dtype)


def _inproj(x2, g, mod, w, layer, seq):
    n, d = x2.shape
    ncol = w.shape[2]
    tm, tn = 512, ncol // 4
    per_b = seq // tm
    est = 2 * tm * d * 4 + tm * d * 2 + 2 * d * tn * 2 + 2 * tm * tn * 2 + tm * tn * 4
    return pl.pallas_call(
        _inproj_kernel,
        grid=(n // tm, ncol // tn),
        in_specs=[pl.BlockSpec((tm, d), lambda i, j: (i, 0)),
                  pl.BlockSpec((1, d), lambda i, j: (0, 0)),
                  pl.BlockSpec((1, 6, d), lambda i, j: (i // per_b, 0, 0)),
                  pl.BlockSpec((1, d, tn), lambda i, j: (layer, 0, j))],
        out_specs=pl.BlockSpec((tm, tn), lambda i, j: (i, j)),
        out_shape=jax.ShapeDtypeStruct((n, ncol), BF16),
        scratch_shapes=[pltpu.VMEM((tm, d), BF16)],
        compiler_params=_cparams(("parallel", "arbitrary"), est),
    )(x2, g.reshape(1, d), mod, w)


def _rope128(y, tc, ts):
    return y * tc + pltpu.roll(y, 64, 1) * ts


def _mla_prep_kernel(ql_ref, kvl_ref, kr_ref, gq_ref, gkv_ref, wq_ref, wkv_ref, tc_ref, ts_ref,
                     q_ref, k_ref, v_ref):
    scale = (MLA_NOPE + MLA_ROPE) ** -0.5
    tc = tc_ref[...]
    ts = ts_ref[...]
    qn = _rms(ql_ref[...].astype(F32), gq_ref[...]).astype(BF16)
    yq = _dot(qn, wq_ref[...])
    kvn = _rms(kvl_ref[...].astype(F32), gkv_ref[...]).astype(BF16)
    ykv = _dot(kvn, wkv_ref[...])
    k_roped = _rope128(kr_ref[...].astype(F32), tc, ts).astype(BF16)
    for h in range(MLA_HEADS):
        o = 2 * LANES * h
        q_ref[:, o:o + LANES] = (yq[:, o:o + LANES] * scale).astype(BF16)
        q_ref[:, o + LANES:o + 2 * LANES] = (_rope128(yq[:, o + LANES:o + 2 * LANES], tc, ts) * scale).astype(BF16)
        k_ref[:, o:o + LANES] = ykv[:, LANES * h:LANES * (h + 1)].astype(BF16)
        k_ref[:, o + LANES:o + 2 * LANES] = k_roped
    v_ref[...] = ykv[:, MLA_HEADS * LANES:].astype(BF16)


def _mla_prep(proj, gq, gkv, wq, wkv, tc, ts, seq):
    n = proj.shape[0]
    tm = 512
    per_b = seq // tm
    hq = MLA_HEADS * 2 * LANES
    return pl.pallas_call(
        _mla_prep_kernel,
        grid=(n // tm,),
        in_specs=[pl.BlockSpec((tm, MLA_Q_RANK), lambda i: (i, P_QLAT // MLA_Q_RANK)),
                  pl.BlockSpec((tm, MLA_KV_RANK), lambda i: (i, P_KVLAT // MLA_KV_RANK)),
                  pl.BlockSpec((tm, LANES), lambda i: (i, P_KROPE // LANES)),
                  pl.BlockSpec((1, MLA_Q_RANK), lambda i: (0, 0)),
                  pl.BlockSpec((1, MLA_KV_RANK), lambda i: (0, 0)),
                  pl.BlockSpec((MLA_Q_RANK, hq), lambda i: (0, 0)),
                  pl.BlockSpec((MLA_KV_RANK, 2 * GROUP_WIDTH), lambda i: (0, 0)),
                  pl.BlockSpec((tm, LANES), lambda i: (i % per_b, 0)),
                  pl.BlockSpec((tm, LANES), lambda i: (i % per_b, 0))],
        out_specs=[pl.BlockSpec((tm, hq), lambda i: (i, 0)),
                   pl.BlockSpec((tm, hq), lambda i: (i, 0)),
                   pl.BlockSpec((tm, GROUP_WIDTH), lambda i: (i, 0))],
        out_shape=[jax.ShapeDtypeStruct((n, hq), BF16),
                   jax.ShapeDtypeStruct((n, hq), BF16),
                   jax.ShapeDtypeStruct((n, GROUP_WIDTH), BF16)],
        compiler_params=_cparams(("parallel",), 16 * 1024 * 1024),
    )(proj, proj, proj, gq.reshape(1, -1), gkv.reshape(1, -1), wq, wkv, tc, ts)


def _mla_flash_kernel(q_ref, k_ref, v_ref, o_ref):
    i = pl.program_id(2)
    tq = q_ref.shape[0]
    tk = tq
    q = q_ref[...]

    def step(j, carry, masked):
        m, l, acc = carry
        k = k_ref[pl.ds(pl.multiple_of(j * tk, tk), tk), :]
        v = v_ref[pl.ds(pl.multiple_of(j * tk, tk), tk), :]
        s = _dot_nt(q, k)
        if masked:
            qc = lax.broadcasted_iota(jnp.int32, (tq, tk), 0) // CHUNK
            kc = lax.broadcasted_iota(jnp.int32, (tq, tk), 1) // CHUNK
            s = jnp.where(kc <= qc, s, NEG_BIG)
        m_new = jnp.maximum(m, jnp.max(s, axis=-1, keepdims=True))
        alpha = jnp.exp(m - m_new)
        p = jnp.exp(s - m_new)
        l = alpha * l + jnp.sum(p, axis=-1, keepdims=True)
        acc = alpha * acc + _dot(p.astype(BF16), v)
        return m_new, l, acc

    init = (jnp.full((tq, 1), NEG_BIG, F32), jnp.zeros((tq, 1), F32), jnp.zeros((tq, v_ref.shape[1]), F32))
    carry = lax.fori_loop(0, i // 2, lambda jj, c: step(2 * jj + 1, step(2 * jj, c, False), False), init)
    carry = lax.cond(lax.rem(i, 2) == 1, lambda c: step(i - 1, c, False), lambda c: c, carry)
    m, l, acc = step(i, carry, True)
    o_ref[...] = (acc / l).astype(o_ref.dtype)


def _mla_flash(q, k, v, batch, seq):
    n = q.shape[0]
    tq = ATT_TQ
    nq = seq // tq
    est = 2 * (seq * 2 * LANES * 2 + seq * LANES * 2) + 8 * tq * tq * 4
    return pl.pallas_call(
        _mla_flash_kernel,
        grid=(batch, MLA_HEADS, nq),
        in_specs=[pl.BlockSpec((tq, 2 * LANES), lambda b, h, i: (b * nq + i, h)),
                  pl.BlockSpec((seq, 2 * LANES), lambda b, h, i: (b, h)),
                  pl.BlockSpec((seq, LANES), lambda b, h, i: (b, h))],
        out_specs=pl.BlockSpec((tq, LANES), lambda b, h, i: (b * nq + i, h)),
        out_shape=jax.ShapeDtypeStruct((n, GROUP_WIDTH), BF16),
        compiler_params=_cparams(("parallel", "parallel", "arbitrary"), est),
    )(q, k, v)


def _sb_kernel(q_ref, k_ref, v_ref, o_ref):
    i = pl.program_id(2)
    tq = q_ref.shape[0]
    tk = tq
    sub = SB_SUB
    scale = q_ref.shape[1] ** -0.5
    q = q_ref[...]
    tri = (lax.broadcasted_iota(jnp.int32, (sub, sub), 0)
           > lax.broadcasted_iota(jnp.int32, (sub, sub), 1)).astype(BF16)

    def step(j, carry, masked):
        run, acc = carry
        k = k_ref[pl.ds(pl.multiple_of(j * tk, tk), tk), :]
        v = v_ref[pl.ds(pl.multiple_of(j * tk, tk), tk), :]
        u = _dot_nt(q, k) * (scale * LOG2_E)
        log_beta = jnp.minimum(u, 0.0) - jnp.log2(1.0 + jnp.exp2(-jnp.abs(u)))
        log_keep = log_beta - u
        if masked:
            vis = (lax.broadcasted_iota(jnp.int32, (tq, tk), 1)
                   < lax.broadcasted_iota(jnp.int32, (tq, tk), 0))
            log_keep = jnp.where(vis, log_keep, 0.0)
        keep_b = log_keep.astype(BF16)
        pieces = []
        for blk in range(tk // sub - 1, -1, -1):
            sl = slice(blk * sub, (blk + 1) * sub)
            pieces.append(_dot(keep_b[:, sl], tri) + run)
            run = run + jnp.sum(log_keep[:, sl], axis=-1, keepdims=True)
        later = jnp.concatenate(pieces[::-1], axis=1)
        att = jnp.exp2(log_beta + later)
        if masked:
            att = jnp.where(vis, att, 0.0)
        acc = acc + _dot(att.astype(BF16), v)
        return run, acc

    carry = step(i, (jnp.zeros((tq, 1), F32), jnp.zeros((tq, v_ref.shape[1]), F32)), True)
    carry = lax.fori_loop(0, i // 2, lambda jj, c: step(i - 2 - 2 * jj, step(i - 1 - 2 * jj, c, False), False), carry)
    run, acc = lax.cond(lax.rem(i, 2) == 1, lambda c: step(0, c, False), lambda c: c, carry)
    o_ref[...] = acc.astype(o_ref.dtype)


def _sb_attention(proj, batch, seq):
    n = proj.shape[0]
    tq = ATT_TQ
    nq = seq // tq
    cq, ck, cv = P_C // LANES, (P_C + GROUP_WIDTH) // LANES, (P_C + 2 * GROUP_WIDTH) // LANES
    est = 4 * seq * LANES * 2 + 10 * tq * tq * 4
    return pl.pallas_call(
        _sb_kernel,
        grid=(batch, SB_HEADS, nq),
        in_specs=[pl.BlockSpec((tq, LANES), lambda b, h, i: (b * nq + i, cq + h)),
                  pl.BlockSpec((seq, LANES), lambda b, h, i: (b, ck + h)),
                  pl.BlockSpec((seq, LANES), lambda b, h, i: (b, cv + h))],
        out_specs=pl.BlockSpec((tq, LANES), lambda b, h, i: (b * nq + i, h)),
        out_shape=jax.ShapeDtypeStruct((n, GROUP_WIDTH), BF16),
        compiler_params=_cparams(("parallel", "parallel", "arbitrary"), est),
    )(proj, proj, proj)


def _ca_kernel(q_ref, k_ref, v_ref, tab_ref, o_ref, kpad_ref, vpad_ref):
    i = pl.program_id(1)
    tq = q_ref.shape[0]
    past = CA_PAST_CHUNKS * CHUNK
    win = tq + past
    scale = LANES ** -0.5

    @pl.when(i == 0)
    def _():
        kpad_ref[0:past, :] = jnp.zeros((past, kpad_ref.shape[1]), kpad_ref.dtype)
        vpad_ref[0:past, :] = jnp.zeros((past, vpad_ref.shape[1]), vpad_ref.dtype)
        kpad_ref[past:, :] = k_ref[...]
        vpad_ref[past:, :] = v_ref[...]

    start = pl.multiple_of(i * tq, tq)
    for h in range(CA_HEADS):
        cols = slice(h * LANES, (h + 1) * LANES)
        kw = kpad_ref[pl.ds(start, win), cols]
        vw = vpad_ref[pl.ds(start, win), cols]
        s = _dot_nt(q_ref[:, cols], kw) * scale + tab_ref[0, h]
        m = jnp.max(s, axis=-1, keepdims=True)
        p = jnp.exp(s - m)
        l = jnp.sum(p, axis=-1, keepdims=True)
        o_ref[:, cols] = (_dot(p.astype(BF16), vw) / l).astype(o_ref.dtype)


def _ca_table(rel_bias, tq):
    past = CA_PAST_CHUNKS * CHUNK
    win = tq + past
    r = jnp.arange(tq)[:, None]
    c = jnp.arange(win)[None, :]
    ring = 1024
    assert ring >= tq + win - 1
    m = jnp.arange(ring)
    m = jnp.where(m >= win, m - ring, m)
    idx = jnp.clip(past - m, -(CHUNK - 1), CA_REL_PAST) + (CHUNK - 1)
    ringvals = rel_bias[:, idx].astype(F32)
    nh = rel_bias.shape[0]
    bias = jnp.tile(ringvals, (1, tq))[:, :tq * (ring - 1)].reshape(nh, tq, ring - 1)[:, :, :win]
    qc = r // CHUNK
    kc = c // CHUNK
    vis = (kc >= qc) & (kc <= qc + CA_PAST_CHUNKS)
    tabs = []
    n_var = past // tq + 1
    for v in range(n_var):
        first = (past - v * tq) // CHUNK if v < n_var - 1 else 0
        ok = vis & (kc >= first)
        tabs.append(jnp.where(ok[None], bias, NEG_BIG))
    return jnp.stack(tabs)


def _ca_attention(proj, table, batch, seq):
    n = proj.shape[0]
    tq = CA_TQ
    nq = seq // tq
    past = CA_PAST_CHUNKS * CHUNK
    win = tq + past
    n_var = table.shape[0]
    gw = GROUP_WIDTH
    cq, ck, cv = P_D // gw, (P_D + gw) // gw, (P_D + 2 * gw) // gw
    est = (4 * seq * gw * 2 + 2 * (seq + past) * gw * 2 + 2 * CA_HEADS * tq * win * 4 + 8 * tq * win * 4
           + 4 * tq * gw * 2)
    return pl.pallas_call(
        _ca_kernel,
        grid=(batch, nq),
        in_specs=[pl.BlockSpec((tq, gw), lambda b, i: (b * nq + i, cq)),
                  pl.BlockSpec((seq, gw), lambda b, i: (b, ck)),
                  pl.BlockSpec((seq, gw), lambda b, i: (b, cv)),
                  pl.BlockSpec((1, CA_HEADS, tq, win), lambda b, i: (jnp.minimum(i, n_var - 1), 0, 0, 0))],
        out_specs=pl.BlockSpec((tq, gw), lambda b, i: (b * nq + i, 0)),
        out_shape=jax.ShapeDtypeStruct((n, gw), BF16),
        scratch_shapes=[pltpu.VMEM((seq + past, gw), BF16), pltpu.VMEM((seq + past, gw), BF16)],
        compiler_params=_cparams(("parallel", "arbitrary"), est),
    )(proj, proj, proj, table)


def _ssd_kernel(z_ref, xbc_ref, dt_ref, cw_ref, cb_ref, dtb_ref, alog_ref, dskip_ref, ng_ref, ex_ref,
                o_ref, xbuf_ref, state_ref, y_ref):
    c = pl.program_id(1)
    lc = z_ref.shape[0]
    halo = SUBLANES
    gw = GROUP_WIDTH // SSM_GROUPS
    hpg = SSM_HEADS // SSM_GROUPS

    @pl.when(c == 0)
    def _():
        xbuf_ref[0:halo, :] = jnp.zeros((halo, xbuf_ref.shape[1]), F32)
        state_ref[...] = jnp.zeros(state_ref.shape, F32)

    xbuf_ref[halo:halo + lc, :] = xbc_ref[...].astype(F32)
    acc = jnp.broadcast_to(cb_ref[...], (lc, SSM_CONV_DIM))
    for k in range(SSM_CONV):
        acc = acc + cw_ref[k:k + 1, :] * xbuf_ref[pl.ds(halo - (SSM_CONV - 1) + k, lc), :]
    xbuf_ref[0:halo, :] = xbuf_ref[lc:lc + halo, :]
    u = acc * jax.nn.sigmoid(acc)
    xs = u[:, :GROUP_WIDTH]
    bm = u[:, GROUP_WIDTH:GROUP_WIDTH + SSM_GROUPS * SSM_STATE].astype(BF16)
    cm = u[:, GROUP_WIDTH + SSM_GROUPS * SSM_STATE:].astype(BF16)

    dt = jax.nn.softplus(dt_ref[...].astype(F32) + dtb_ref[...])
    a = -jnp.exp(alog_ref[...])
    da = dt * a
    row = lax.broadcasted_iota(jnp.int32, (lc, lc), 0)
    col = lax.broadcasted_iota(jnp.int32, (lc, lc), 1)
    lower = col <= row
    cs = _dot3_left01(lower.astype(BF16), da)
    cs_last = cs[lc - 1:lc, :]
    cs_t = cs.T
    dt_t = dt.T
    ex = ex_ref[...]
    ecs_x = _dot3_right01(jnp.exp(cs), ex)
    wdec_x = _dot3_right01(jnp.exp(cs_last - cs) * dt, ex)
    xs_b = xs.astype(BF16)

    for g in range(SSM_GROUPS):
        bg = bm[:, g * SSM_STATE:(g + 1) * SSM_STATE]
        cg = cm[:, g * SSM_STATE:(g + 1) * SSM_STATE]
        cb = _dot_nt(cg, bg)
        for r in range(hpg):
            h = g * hpg + r
            diff = cs[:, h:h + 1] - cs_t[h:h + 1, :]
            wgt = cb * jnp.exp(jnp.where(lower, diff, NEG_BIG)) * dt_t[h:h + 1, :]
            y_ref[:, h * SSM_HEAD_DIM:(h + 1) * SSM_HEAD_DIM] = _dot(
                wgt.astype(BF16), xs_b[:, h * SSM_HEAD_DIM:(h + 1) * SSM_HEAD_DIM])
        lo, hi = g * gw, (g + 1) * gw
        st = state_ref[g]
        y_off = _dot(cg, st.astype(BF16)) * ecs_x[:, lo:hi]
        y_ref[:, lo:hi] += y_off
        xw = (xs[:, lo:hi] * wdec_x[:, lo:hi]).astype(BF16)
        state_ref[g] = st * ecs_x[lc - 1:lc, lo:hi] + _dot_tn(bg, xw)

    zf = z_ref[...].astype(F32)
    y = (y_ref[...] + xs * dskip_ref[...]) * (zf * jax.nn.sigmoid(zf))
    for g in range(SSM_GROUPS):
        lo, hi = g * gw, (g + 1) * gw
        yg = y[:, lo:hi]
        yg = yg * lax.rsqrt(jnp.mean(yg * yg, axis=-1, keepdims=True) + EPS)
        o_ref[:, lo:hi] = (yg * ng_ref[:, lo:hi]).astype(o_ref.dtype)


def _ssd(proj, conv_w, conv_b, dt_bias, a_log, d_skip, norm_g, batch, seq):
    n = proj.shape[0]
    lc = SSD_LC
    nc = seq // lc
    dtb = jnp.zeros((1, LANES), F32).at[0, :SSM_HEADS].set(dt_bias)
    alog = jnp.full((1, LANES), NEG_BIG, F32).at[0, :SSM_HEADS].set(a_log)
    dskip = jnp.repeat(d_skip, SSM_HEAD_DIM).reshape(1, GROUP_WIDTH)
    ex = (jnp.arange(LANES)[:, None] == jnp.arange(GROUP_WIDTH)[None, :] // SSM_HEAD_DIM).astype(BF16)
    full = lambda shape: pl.BlockSpec(shape, lambda b, c: tuple(0 for _ in shape))
    return pl.pallas_call(
        _ssd_kernel,
        grid=(batch, nc),
        in_specs=[pl.BlockSpec((lc, GROUP_WIDTH), lambda b, c: (b * nc + c, P_Z // GROUP_WIDTH)),
                  pl.BlockSpec((lc, SSM_CONV_DIM), lambda b, c: (b * nc + c, P_XBC // SSM_CONV_DIM)),
                  pl.BlockSpec((lc, LANES), lambda b, c: (b * nc + c, P_DT // LANES)),
                  full((SSM_CONV, SSM_CONV_DIM)), full((1, SSM_CONV_DIM)), full((1, LANES)), full((1, LANES)),
                  full((1, GROUP_WIDTH)), full((1, GROUP_WIDTH)), full((LANES, GROUP_WIDTH))],
        out_specs=pl.BlockSpec((lc, GROUP_WIDTH), lambda b, c: (b * nc + c, 0)),
        out_shape=jax.ShapeDtypeStruct((n, GROUP_WIDTH), BF16),
        scratch_shapes=[pltpu.VMEM((lc + SUBLANES, SSM_CONV_DIM), F32),
                        pltpu.VMEM((SSM_GROUPS, SSM_STATE, GROUP_WIDTH // SSM_GROUPS), F32),
                        pltpu.VMEM((lc, GROUP_WIDTH), F32)],
        compiler_params=_cparams(("parallel", "arbitrary"), 16 * 1024 * 1024),
    )(proj, proj, proj, conv_w, conv_b.reshape(1, -1), dtb, alog, dskip, norm_g.reshape(1, -1), ex)


def _outproj_kernel(a_ref, b_ref, c_ref, d_ref, x_ref, gmix_ref, w_ref, mod_ref, gf_ref, rw_ref, rb_ref,
                    xo_ref, h_ref, e_ref, gt_ref):
    gw = GROUP_WIDTH
    a = _rms(a_ref[...].astype(F32), gmix_ref[0:1, :]).astype(BF16)
    c = _rms(c_ref[...].astype(F32), gmix_ref[1:2, :]).astype(BF16)
    d = _rms(d_ref[...].astype(F32), gmix_ref[2:3, :]).astype(BF16)
    mixed = (_dot(a, w_ref[0:gw, :]) + _dot(b_ref[...], w_ref[gw:2 * gw, :])
             + _dot(c, w_ref[2 * gw:3 * gw, :]) + _dot(d, w_ref[3 * gw:, :]))
    x = x_ref[...] + mod_ref[0, 2:3, :] * mixed
    xo_ref[...] = x
    h = _rms(x, gf_ref[...]) * (1.0 + mod_ref[0, 4:5, :]) + mod_ref[0, 3:4, :]
    _store_packed_rows(h_ref, h)

    hh = h.astype(BF16)
    hl = (h - hh.astype(F32)).astype(BF16)
    rw = rw_ref[...]
    wh = rw.astype(BF16)
    wl = (rw - wh.astype(F32)).astype(BF16)
    logits = _dot(hh, wh) + _dot(hh, wl) + _dot(hl, wh) + rb_ref[...]
    lane = lax.broadcasted_iota(jnp.int32, logits.shape, 1)
    vals, idxs = [], []
    for _ in range(TOP_K):
        m = jnp.max(logits, axis=-1, keepdims=True)
        idx = jnp.min(jnp.where(logits == m, lane, LANES), axis=-1, keepdims=True)
        vals.append(m)
        idxs.append(idx)
        logits = jnp.where(lane == idx, -jnp.inf, logits)
    ex = [jnp.exp(v - vals[0]) for v in vals]
    tot = ex[0] + ex[1] + ex[2] + ex[3]
    e_out = jnp.zeros(logits.shape, jnp.int32)
    g_out = jnp.zeros(logits.shape, F32)
    for k in range(TOP_K):
        e_out = jnp.where(lane == k, idxs[k], e_out)
        g_out = jnp.where(lane == k, ex[k] / tot, g_out)
    e_ref[...] = e_out
    gt_ref[...] = g_out


def _outproj(oa, ob, oc, od, x2, gmix, w_out, mod, gf, rw, rb, seq):
    n, d = x2.shape
    tm = 256
    per_b = seq // tm
    gw = GROUP_WIDTH
    row = lambda width: pl.BlockSpec((tm, width), lambda i: (i, 0))
    full = lambda shape: pl.BlockSpec(shape, lambda i: tuple(0 for _ in shape))
    est = 2 * d * d * 2 + 4 * tm * d * 4 + 2 * tm * d * 2 + 8 * tm * gw * 2 + 6 * tm * d * 4
    return pl.pallas_call(
        _outproj_kernel,
        grid=(n // tm,),
        in_specs=[row(gw), row(gw), row(gw), row(gw), row(d), full((3, gw)), full((d, d)),
                  pl.BlockSpec((1, 6, d), lambda i: (i // per_b, 0, 0)), full((1, d)),
                  full((d, LANES)), full((1, LANES))],
        out_specs=[row(d), pl.BlockSpec((tm * ROW_SUBLANES, LANES), lambda i: (i, 0)), row(LANES), row(LANES)],
        out_shape=[jax.ShapeDtypeStruct((n, d), F32), jax.ShapeDtypeStruct((n * ROW_SUBLANES, LANES), U32),
                   jax.ShapeDtypeStruct((n, LANES), jnp.int32), jax.ShapeDtypeStruct((n, LANES), F32)],
        compiler_params=_cparams(("parallel",), est),
    )(oa, ob, oc, od, x2, gmix, w_out, mod, gf.reshape(1, d), rw, rb)


def _row_copy(src_ref, src_row, dst_ref, dst_row, sem):
    rs = ROW_SUBLANES
    return pltpu.make_async_copy(src_ref.at[pl.ds(pl.multiple_of(src_row * rs, rs), rs)],
                                 dst_ref.at[pl.ds(pl.multiple_of(dst_row * rs, rs), rs)], sem)


def _expert_kernel(be_ref, nu_ref, tok_first_ref, step_ref, dst_last_ref, hp_ref, w1_ref, b1_ref, w2_ref,
                   b2_ref, y_ref, xa_ref, xb_ref, ya_ref, yb_ref, gsem, ssem):
    i = pl.program_id(0)
    nu = nu_ref[0]
    tm = MOE_BLOCK
    rs = ROW_SUBLANES

    def start_rows(copy_of_row):
        for r in range(tm):
            copy_of_row(r).start(priority=r % 2)

    def wait_rows(src_ref, dst_ref, sem):
        for _ in range(tm):
            _row_copy(src_ref, 0, dst_ref, 0, sem).wait()

    @pl.when(i == 0)
    def _():
        start_rows(lambda r: _row_copy(hp_ref, tok_first_ref[0, 0, r], xa_ref, r, gsem.at[0]))
        yb_ref[...] = jnp.zeros(yb_ref.shape, U32)
        spare = pltpu.make_async_copy(yb_ref, y_ref.at[pl.ds(y_ref.shape[0] - 2 * tm * rs, tm * rs)], ssem.at[0])
        spare.start()
        spare.wait()

    def body(xcur, xnext, ycur, yprev, gcur, gnext, scur, sprev):
        wait_rows(hp_ref, xcur, gcur)
        start_rows(lambda r: _row_copy(hp_ref, step_ref[0, 0, r], xnext, r, gnext))
        start_rows(lambda r: _row_copy(yprev, r, y_ref, step_ref[0, 0, tm + r], sprev))
        words = _load_packed_rows(xcur)
        x = jnp.concatenate([_unpack_lo(u).astype(BF16) for u in words]
                            + [_unpack_hi(u).astype(BF16) for u in words], axis=1)
        cw = EXPERT_COL_CHUNK
        hu = jnp.concatenate([_dot(x, w1_ref[0, :, c:c + cw].astype(BF16)) for c in range(0, w1_ref.shape[2], cw)],
                             axis=1) + b1_ref[0]
        gate = jnp.minimum(hu[:, :D_EXPERT], SWIGLU_LIMIT)
        up = jnp.clip(hu[:, D_EXPERT:], -SWIGLU_LIMIT, SWIGLU_LIMIT)
        act = (gate * jax.nn.sigmoid(SWIGLU_ALPHA * gate) * (up + 1.0)).astype(BF16)
        y = jnp.concatenate([_dot(act, w2_ref[0, :, c:c + cw].astype(BF16)) for c in range(0, w2_ref.shape[2], cw)],
                            axis=1) + b2_ref[0]

        @pl.when(i > 0)
        def _():
            wait_rows(ycur, y_ref, scur)

        _store_packed_rows(ycur, y)

        @pl.when(i == nu - 1)
        def _():
            start_rows(lambda r: _row_copy(ycur, r, y_ref, dst_last_ref[0, 0, r], scur))
            wait_rows(yprev, y_ref, sprev)
            wait_rows(ycur, y_ref, scur)
            wait_rows(hp_ref, xnext, gnext)

    even = lax.rem(i, 2) == 0

    @pl.when(jnp.logical_and(i < nu, even))
    def _():
        body(xa_ref, xb_ref, ya_ref, yb_ref, gsem.at[0], gsem.at[1], ssem.at[0], ssem.at[1])

    @pl.when(jnp.logical_and(i < nu, jnp.logical_not(even)))
    def _():
        body(xb_ref, xa_ref, yb_ref, ya_ref, gsem.at[1], gsem.at[0], ssem.at[1], ssem.at[0])


def _experts(hp, row_tok, row_dst, block_e, n_used, n_slots, w1, b1, w2, b2, layer=0):
    tm = MOE_BLOCK
    nblk = row_tok.shape[0]
    d, de2 = w1.shape[1], w1.shape[2]
    de = w2.shape[1]
    rs = ROW_SUBLANES
    block_e = block_e + layer * N_EXPERTS
    last = lambda i, nu: jnp.minimum(i, nu[0] - 1)
    est = (2 * (d * de2 + de * d) * w1.dtype.itemsize + 4 * tm * rs * LANES * 4 + 4 * tm * d * 4
           + 3 * tm * de2 * 4)
    first_prev = (n_slots - tm + jnp.arange(tm, dtype=jnp.int32)).reshape(1, 1, tm)
    step_rows = jnp.concatenate([jnp.concatenate([row_tok[1:], row_tok[-1:]], axis=0),
                                 jnp.concatenate([first_prev, row_dst[:-1]], axis=0)], axis=2)
    dst_last = lax.dynamic_slice_in_dim(row_dst, n_used[0] - 1, 1, axis=0)
    smem = lambda width, f: pl.BlockSpec((1, 1, width), f, memory_space=pltpu.SMEM)
    grid_spec = pltpu.PrefetchScalarGridSpec(
        num_scalar_prefetch=2,
        grid=(nblk,),
        in_specs=[smem(tm, lambda i, be, nu: (0, 0, 0)),
                  smem(2 * tm, lambda i, be, nu: (last(i, nu), 0, 0)),
                  smem(tm, lambda i, be, nu: (0, 0, 0)),
                  pl.BlockSpec(memory_space=pl.ANY),
                  pl.BlockSpec((1, d, de2), lambda i, be, nu: (be[last(i, nu)], 0, 0)),
                  pl.BlockSpec((1, 1, de2), lambda i, be, nu: (be[last(i, nu)], 0, 0)),
                  pl.BlockSpec((1, de, d), lambda i, be, nu: (be[last(i, nu)], 0, 0)),
                  pl.BlockSpec((1, 1, d), lambda i, be, nu: (be[last(i, nu)], 0, 0))],
        out_specs=pl.BlockSpec(memory_space=pl.ANY),
        scratch_shapes=[pltpu.VMEM((tm * rs, LANES), U32) for _ in range(4)]
        + [pltpu.SemaphoreType.DMA((2,)), pltpu.SemaphoreType.DMA((2,))],
    )
    return pl.pallas_call(
        _expert_kernel,
        grid_spec=grid_spec,
        out_shape=jax.ShapeDtypeStruct((n_slots * rs, LANES), U32),
        compiler_params=_cparams(("arbitrary",), est),
    )(block_e, n_used, row_tok, step_rows, dst_last, hp, w1, b1.reshape(b1.shape[0], 1, de2), w2,
      b2.reshape(b2.shape[0], 1, d))


def _combine_kernel(y0_ref, y1_ref, y2_ref, y3_ref, g_ref, x_ref, mod_ref, o_ref):
    tm, d = x_ref.shape
    half = d // 2
    rs = ROW_SUBLANES
    rc = COMBINE_ROWS

    def chunk(c, carry):
        r0 = pl.multiple_of(c * rc, rc)
        rows = pl.ds(r0, rc)
        g = g_ref[rows, :]
        lo = [jnp.zeros((rc, LANES), F32) for _ in range(rs)]
        hi = [jnp.zeros((rc, LANES), F32) for _ in range(rs)]
        for k, y_ref in enumerate((y0_ref, y1_ref, y2_ref, y3_ref)):
            blk = y_ref[pl.ds(pl.multiple_of(r0 * rs, rc * rs), rc * rs), :]
            w = jnp.swapaxes(blk.reshape(rc // rs, rs, rs, LANES), 1, 2)
            gk = g[:, k:k + 1]
            for s in range(rs):
                u = w[:, s].reshape(rc, LANES)
                lo[s] = lo[s] + _unpack_lo(u) * gk
                hi[s] = hi[s] + _unpack_hi(u) * gk
        for s in range(rs):
            a, b = LANES * s, half + LANES * s
            o_ref[rows, a:a + LANES] = x_ref[rows, a:a + LANES] + mod_ref[0, 5:6, a:a + LANES] * lo[s]
            o_ref[rows, b:b + LANES] = x_ref[rows, b:b + LANES] + mod_ref[0, 5:6, b:b + LANES] * hi[s]
        return carry
    lax.fori_loop(0, tm // rc, chunk, 0)


def _combine(y4, gates, x2, mod, seq):
    n, d = x2.shape
    tm = 256
    per_b = seq // tm
    nt = n // tm
    rows = tm * ROW_SUBLANES
    est = 2 * TOP_K * rows * LANES * 4 + 4 * tm * d * 4 + 2 * tm * LANES * 4
    yspec = lambda k: pl.BlockSpec((rows, LANES), lambda i: (k * nt + i, 0))
    return pl.pallas_call(
        _combine_kernel,
        grid=(nt,),
        in_specs=[yspec(0), yspec(1), yspec(2), yspec(3),
                  pl.BlockSpec((tm, LANES), lambda i: (i, 0)),
                  pl.BlockSpec((tm, d), lambda i: (i, 0)),
                  pl.BlockSpec((1, 6, d), lambda i: (i // per_b, 0, 0))],
        out_specs=pl.BlockSpec((tm, d), lambda i: (i, 0)),
        out_shape=jax.ShapeDtypeStruct((n, d), F32),
        compiler_params=_cparams(("parallel",), est),
    )(y4, y4, y4, y4, gates, x2, mod)


def _final_norm_kernel(x_ref, g_ref, o_ref):
    o_ref[...] = _rms(x_ref[...], g_ref[...])


def _final_norm(x2, g):
    n, d = x2.shape
    tm = 512
    return pl.pallas_call(
        _final_norm_kernel,
        grid=(n // tm,),
        in_specs=[pl.BlockSpec((tm, d), lambda i: (i, 0)), pl.BlockSpec((1, d), lambda i: (0, 0))],
        out_specs=pl.BlockSpec((tm, d), lambda i: (i, 0)),
        out_shape=jax.ShapeDtypeStruct((n, d), F32),
        compiler_params=_cparams(("parallel",), 4 * tm * d * 4),
    )(x2, g.reshape(1, d))


def _swap_halves(w):
    half = w.shape[-1] // 2
    return jnp.concatenate([w[..., half:], w[..., :half]], axis=-1)


def _pack_w_in_kernel(w_ref, o_ref):
    w = w_ref[0]
    b0, c0 = A_COLS, A_COLS + B_COLS
    kr0 = MLA_Q_RANK + MLA_KV_RANK
    half = MLA_ROPE // 2
    pieces = [w[:, :MLA_Q_RANK],
              w[:, b0:b0 + GROUP_WIDTH + SSM_CONV_DIM],
              w[:, MLA_Q_RANK:kr0 + MLA_ROPE],
              w[:, kr0 + half:kr0 + MLA_ROPE], w[:, kr0:kr0 + half],
              w[:, c0 - SSM_HEADS:c0], jnp.zeros((w.shape[0], LANES - SSM_HEADS), w.dtype),
              w[:, c0:]]
    o_ref[0] = jnp.concatenate(pieces, axis=1).astype(BF16)


def _pack_w_in(w_in):
    nl, d, nc = w_in.shape
    tr = 256
    est = 2 * tr * nc * 4 + 2 * tr * P_COLS * 2 + 2 * tr * P_COLS * 4
    return pl.pallas_call(
        _pack_w_in_kernel,
        grid=(nl, d // tr),
        in_specs=[pl.BlockSpec((1, tr, nc), lambda l, i: (l, i, 0))],
        out_specs=pl.BlockSpec((1, tr, P_COLS), lambda l, i: (l, i, 0)),
        out_shape=jax.ShapeDtypeStruct((nl, d, P_COLS), BF16),
        compiler_params=_cparams(("parallel", "parallel"), est),
    )(w_in)


def _pack_wq(wq):
    r = wq.shape[0]
    w = wq.reshape(r, MLA_HEADS, MLA_NOPE + MLA_ROPE)
    rope = w[..., MLA_NOPE:]
    return jnp.concatenate([w[..., :MLA_NOPE], rope, _swap_halves(rope)], axis=-1).reshape(r, -1).astype(BF16)


def _pack_wkv(wkv):
    r = wkv.shape[0]
    w = wkv.reshape(r, MLA_HEADS, MLA_NOPE + MLA_V)
    return jnp.concatenate([w[..., :MLA_NOPE].reshape(r, -1), w[..., MLA_NOPE:].reshape(r, -1)], axis=-1).astype(BF16)


def _rope_tables(seq):
    inv = 1.0 / (ROPE_THETA ** (jnp.arange(0, MLA_ROPE, 2, dtype=F32) / MLA_ROPE))
    ang = jnp.arange(seq, dtype=F32)[:, None] * inv[None, :]
    cos, sin = jnp.cos(ang), jnp.sin(ang)
    zero = jnp.zeros((seq, LANES - MLA_ROPE), F32)
    return jnp.concatenate([cos, cos, zero], axis=1), jnp.concatenate([-sin, sin, zero], axis=1)


def _route(top_e, n_tok):
    n_assign = n_tok * TOP_K
    n_blocks = -(-(n_assign + N_EXPERTS * (MOE_BLOCK - 1)) // MOE_BLOCK)
    n_rows = n_blocks * MOE_BLOCK
    dest, meta = _route_dest(top_e)
    padded_end = meta[2, :N_EXPERTS]
    block_start = jnp.arange(n_blocks, dtype=jnp.int32) * MOE_BLOCK
    block_e = jnp.minimum(jnp.sum((padded_end[None, :] <= block_start[:, None]).astype(jnp.int32), axis=1),
                          N_EXPERTS - 1)
    n_used = (padded_end[N_EXPERTS - 1] // MOE_BLOCK).reshape(1)
    row_assign = _route_invert(meta, dest[:, :TOP_K].reshape(n_assign), n_rows)
    real = row_assign >= 0
    row = jnp.arange(n_rows, dtype=jnp.int32)
    row_tok = jnp.where(real, row_assign // TOP_K, 0)
    spare = n_assign + (row // MOE_BLOCK % 2) * MOE_BLOCK + row % MOE_BLOCK
    row_dst = jnp.where(real, row_assign % TOP_K * n_tok + row_assign // TOP_K, spare)
    shape = (n_blocks, 1, MOE_BLOCK)
    return row_tok.reshape(shape), row_dst.reshape(shape), block_e, n_used, n_assign + 2 * MOE_BLOCK


NORM_ROWS = 16
ROUTE_TM = 512
COMBINE_ROWS = 16


def _route_dest_kernel(e_ref, dest_ref, meta_ref, carry_ref, base_ref):
    p = pl.program_id(0)
    i = pl.program_id(1)
    tm = e_ref.shape[0]
    e = e_ref[...]
    lane = lax.broadcasted_iota(jnp.int32, (tm, LANES), 1)
    hit = [lane == e[:, k:k + 1] for k in range(TOP_K)]
    tot = sum(h.astype(F32) for h in hit)

    @pl.when(jnp.logical_and(p == 0, i == 0))
    def _():
        carry_ref[...] = jnp.zeros(carry_ref.shape, F32)
        meta_ref[...] = jnp.zeros(meta_ref.shape, jnp.int32)

    @pl.when(jnp.logical_and(p == 0, i == pl.num_programs(1) - 1))
    def _():
        counts = carry_ref[...] + jnp.sum(tot, axis=0, keepdims=True)
        padded = jnp.floor((counts + (MOE_BLOCK - 1)) * (1.0 / MOE_BLOCK)) * MOE_BLOCK
        upper = (lax.broadcasted_iota(jnp.int32, (LANES, LANES), 0)
                 <= lax.broadcasted_iota(jnp.int32, (LANES, LANES), 1)).astype(BF16)
        padded_end = _dot3_right01(jnp.broadcast_to(padded, (SUBLANES, LANES)), upper)[0:1]
        base_ref[...] = padded_end - padded
        meta_ref[0:1, :] = counts.astype(jnp.int32)
        meta_ref[1:2, :] = (padded_end - padded).astype(jnp.int32)
        meta_ref[2:3, :] = padded_end.astype(jnp.int32)
        carry_ref[...] = -jnp.sum(tot, axis=0, keepdims=True)

    @pl.when(p == 1)
    def _():
        earlier = (lax.broadcasted_iota(jnp.int32, (tm, tm), 1)
                   < lax.broadcasted_iota(jnp.int32, (tm, tm), 0)).astype(BF16)
        pos = _dot(earlier, tot.astype(BF16)) + carry_ref[...] + base_ref[...]
        dest = jnp.zeros((tm, LANES), jnp.int32)
        for k in range(TOP_K):
            dk = jnp.sum(jnp.where(hit[k], pos, 0.0), axis=-1, keepdims=True)
            dest = jnp.where(lane == k, dk.astype(jnp.int32), dest)
        dest_ref[...] = dest

    carry_ref[...] += jnp.sum(tot, axis=0, keepdims=True)


def _route_dest(top_e):
    n = top_e.shape[0]
    tm = ROUTE_TM
    return pl.pallas_call(
        _route_dest_kernel,
        grid=(2, n // tm),
        in_specs=[pl.BlockSpec((tm, LANES), lambda p, i: (i, 0))],
        out_specs=[pl.BlockSpec((tm, LANES), lambda p, i: (i * p, 0)),
                   pl.BlockSpec((SUBLANES, LANES), lambda p, i: (0, 0))],
        out_shape=[jax.ShapeDtypeStruct((n, LANES), jnp.int32), jax.ShapeDtypeStruct((SUBLANES, LANES), jnp.int32)],
        scratch_shapes=[pltpu.VMEM((1, LANES), F32), pltpu.VMEM((1, LANES), F32)],
        compiler_params=_cparams(("arbitrary", "arbitrary"), 8 * tm * tm * 4),
    )(top_e)


ROUTE_CHUNK = 2048


def _route_invert_kernel(meta_ref, dest_ref, out_ref):
    s = pl.program_id(0)
    ch = dest_ref.shape[2]
    per = SUBLANES

    def clear(r, carry):
        out_ref[r] = -1
        return carry

    @pl.when(s == 0)
    def _():
        def expert_padding(e, carry):
            return lax.fori_loop(meta_ref[1, e] + meta_ref[0, e], meta_ref[2, e], clear, carry)
        lax.fori_loop(0, N_EXPERTS, expert_padding, 0)
        lax.fori_loop(meta_ref[2, N_EXPERTS - 1], out_ref.shape[0], clear, 0)

    def place(j, carry):
        for u in range(per):
            a = j * per + u
            out_ref[dest_ref[0, 0, a]] = s * ch + a
        return carry
    lax.fori_loop(0, ch // per, place, 0)


def _route_invert(meta, dest_flat, n_rows):
    ch = ROUTE_CHUNK
    n_place = dest_flat.shape[0] // ch
    assert n_place * ch == dest_flat.shape[0]
    return pl.pallas_call(
        _route_invert_kernel,
        grid=(n_place,),
        in_specs=[pl.BlockSpec(memory_space=pltpu.SMEM),
                  pl.BlockSpec((1, 1, ch), lambda s: (s, 0, 0), memory_space=pltpu.SMEM)],
        out_specs=pl.BlockSpec(memory_space=pltpu.SMEM),
        out_shape=jax.ShapeDtypeStruct((n_rows,), jnp.int32),
        compiler_params=pltpu.CompilerParams(dimension_semantics=("arbitrary",)),
    )(meta, dest_flat.reshape(n_place, 1, ch))


def kernel(x, c, attn_norm, ffn_norm, mod_w, mod_b, w_in, mla_q_norm, mla_w_q_up, mla_kv_norm, mla_w_kv_up,
           ssm_conv_w, ssm_conv_b, ssm_dt_bias, ssm_a_log, ssm_d, ssm_norm, ca_rel_bias, mix_out_norm, w_out,
           router_w, router_b, moe_w1, moe_b1, moe_w2, moe_b2, final_norm):
    batch, seq, d = x.shape
    n = batch * seq
    depth = w_in.shape[0]
    x2 = x.reshape(n, d)
    mod_all = _mod_all(c, mod_w, mod_b).reshape(depth, batch, 6, d)
    tc, ts = _rope_tables(seq)
    w_in_packed = _pack_w_in(w_in)
    for l in range(depth):
        mod = mod_all[l]
        proj = _inproj(x2, attn_norm[l], mod, w_in_packed, l, seq)
        q, k, v = _mla_prep(proj, mla_q_norm[l], mla_kv_norm[l], _pack_wq(mla_w_q_up[l]),
                            _pack_wkv(mla_w_kv_up[l]), tc, ts, seq)
        out_a = _mla_flash(q, k, v, batch, seq)
        out_b = _ssd(proj, ssm_conv_w[l], ssm_conv_b[l], ssm_dt_bias[l], ssm_a_log[l], ssm_d[l], ssm_norm[l],
                     batch, seq)
        out_c = _sb_attention(proj, batch, seq)
        out_d = _ca_attention(proj, _ca_table(ca_rel_bias[l], CA_TQ), batch, seq)
        rw = jnp.zeros((d, LANES), F32).at[:, :N_EXPERTS].set(router_w[l])
        rb = jnp.full((1, LANES), NEG_BIG, F32).at[0, :N_EXPERTS].set(router_b[l])
        x2, hp, top_e, gates = _outproj(out_a, out_b, out_c, out_d, x2, mix_out_norm[l], w_out[l].astype(BF16),
                                        mod, ffn_norm[l], rw, rb, seq)
        row_tok, row_dst, block_e, n_used, n_slots = _route(top_e, n)
        y4 = _experts(hp, row_tok, row_dst, block_e, n_used, n_slots,
                      moe_w1.reshape(-1, *moe_w1.shape[2:]), moe_b1.reshape(-1, moe_b1.shape[2]),
                      moe_w2.reshape(-1, *moe_w2.shape[2:]), moe_b2.reshape(-1, moe_b2.shape[2]), layer=l)
        x2 = _combine(y4, gates, x2, mod, seq)
    return _final_norm(x2, final_norm).reshape(batch, seq, d)
```

```python
import functools
import math

import jax
import jax.numpy as jnp
from jax import lax
from jax.experimental import pallas as pl
from jax.experimental.pallas import tpu as pltpu

F32 = jnp.float32
BF16 = jnp.bfloat16
U32 = jnp.uint32

D_MODEL = 2048
DEPTH = 4
CHUNK = 64
EPS = 1e-6
GROUP_WIDTH = 512
MLA_HEADS = 4
MLA_NOPE = 128
MLA_ROPE = 64
MLA_V = 128
MLA_Q_RANK = 512
MLA_KV_RANK = 256
ROPE_THETA = 10000.0
SSM_HEAD_DIM = 64
SSM_HEADS = 8
SSM_GROUPS = 2
SSM_STATE = 128
SSM_CONV = 4
SSM_CONV_DIM = 1024
SB_HEADS = 4
CA_HEADS = 4
CA_PAST_CHUNKS = 8
CA_REL_PAST = 256
N_EXPERTS = 32
TOP_K = 4
D_EXPERT = 768
SWIGLU_ALPHA = 1.702
SWIGLU_LIMIT = 7.0

A_COLS = MLA_Q_RANK + MLA_KV_RANK + MLA_ROPE
B_COLS = GROUP_WIDTH + SSM_CONV_DIM + SSM_HEADS
C_COLS = 3 * GROUP_WIDTH

LANES = 128
SUBLANES = 8
VMEM_BYTES_V7X = 64 * 1024 * 1024
NEG_BIG = -1e30
LOG2_E = math.log2(math.e)

P_QLAT = 0
P_Z = 512
P_XBC = 1024
P_KVLAT = 2048
P_KROPE = 2304
P_DT = 2432
P_C = 2560
P_D = 4096
P_COLS = 5632

MOE_BLOCK = 256
EXPERT_COL_CHUNK = 256
ATT_TQ = 512
SB_SUB = 256
ROW_SUBLANES = 8
CA_TQ = 256
CA_WIN = CA_TQ + CA_PAST_CHUNKS * CHUNK
SSD_LC = 256


def _vmem_limit(nbytes):
    return int(min(max(2 * nbytes, 16 * 1024 * 1024), VMEM_BYTES_V7X - 8 * 1024 * 1024))


def _cparams(sem, nbytes):
    return pltpu.CompilerParams(dimension_semantics=sem, vmem_limit_bytes=_vmem_limit(nbytes))


def _rms(x, g):
    return x * lax.rsqrt(jnp.mean(x * x, axis=-1, keepdims=True) + EPS) * g


def _split3(x):
    hi = x.astype(BF16)
    r1 = x - hi.astype(F32)
    mid = r1.astype(BF16)
    lo = (r1 - mid.astype(F32)).astype(BF16)
    return hi, mid, lo


def _dot(a, b):
    return jnp.dot(a, b, preferred_element_type=F32)


def _dot_nt(a, b):
    return lax.dot_general(a, b, (((1,), (1,)), ((), ())), preferred_element_type=F32)


def _dot_tn(a, b):
    return lax.dot_general(a, b, (((0,), (0,)), ((), ())), preferred_element_type=F32)


def _dot3_left01(t01, x):
    hi, mid, lo = _split3(x)
    return _dot(t01, hi) + _dot(t01, mid) + _dot(t01, lo)


def _dot3_right01(x, t01):
    hi, mid, lo = _split3(x)
    return _dot(hi, t01) + _dot(mid, t01) + _dot(lo, t01)


def _bf16_bits(x):
    u = lax.bitcast_convert_type(x, U32)
    return (u + jnp.uint32(0x7FFF) + ((u >> 16) & jnp.uint32(1))) >> 16


def _store_packed_rows(ref, y):
    tm, d = y.shape
    half = d // 2
    rs = ROW_SUBLANES
    words = []
    for s in range(rs):
        lo = _bf16_bits(y[:, LANES * s:LANES * (s + 1)])
        hi = _bf16_bits(y[:, half + LANES * s:half + LANES * (s + 1)])
        words.append(((hi << 16) | lo).reshape(tm // rs, rs, LANES))
    ref[...] = jnp.swapaxes(jnp.stack(words, axis=1), 1, 2).reshape(tm * rs, LANES)


def _load_packed_rows(ref):
    rs = ROW_SUBLANES
    tm = ref.shape[0] // rs
    w = jnp.swapaxes(ref[...].reshape(tm // rs, rs, rs, LANES), 1, 2)
    return [w[:, s].reshape(tm, LANES) for s in range(rs)]


def _unpack_lo(u):
    return lax.bitcast_convert_type(u << 16, F32)


def _unpack_hi(u):
    return lax.bitcast_convert_type(u & jnp.uint32(0xFFFF0000), F32)


def _mod_kernel(ct_ref, w_ref, b_ref, o_ref):
    k = pl.program_id(2)
    nb = o_ref.shape[1]

    @pl.when(k == 0)
    def _():
        o_ref[0] = jnp.broadcast_to(b_ref[0], o_ref.shape[1:])

    ct = ct_ref[...]
    cond = ct * jax.nn.sigmoid(ct)
    w = w_ref[0]
    for b in range(nb):
        o_ref[0, b:b + 1, :] += jnp.sum(w * cond[:, b:b + 1], axis=0, keepdims=True)


def _mod_all(c, mod_w, mod_b):
    nb, d = c.shape
    nl, _, nout = mod_w.shape
    tk, tn = 512, 2048
    ct = c.T
    return pl.pallas_call(
        _mod_kernel,
        grid=(nl, nout // tn, d // tk),
        in_specs=[pl.BlockSpec((tk, nb), lambda l, j, k: (k, 0)),
                  pl.BlockSpec((1, tk, tn), lambda l, j, k: (l, k, j)),
                  pl.BlockSpec((1, 1, tn), lambda l, j, k: (l, 0, j))],
        out_specs=pl.BlockSpec((1, nb, tn), lambda l, j, k: (l, 0, j)),
        out_shape=jax.ShapeDtypeStruct((nl, nb, nout), F32),
        compiler_params=_cparams(("parallel", "parallel", "arbitrary"), 2 * tk * tn * 4),
    )(ct, mod_w, mod_b.reshape(nl, 1, nout))


def _inproj_kernel(x_ref, g_ref, mod_ref, w_ref, o_ref, h_ref):
    @pl.when(pl.program_id(1) == 0)
    def _():
        rc = NORM_ROWS
        gain = g_ref[...] * (1.0 + mod_ref[0, 1:2, :])
        shift = mod_ref[0, 0:1, :]

        def chunk(c, carry):
            rows = pl.ds(pl.multiple_of(c * rc, rc), rc)
            x = x_ref[rows, :]
            r = lax.rsqrt(jnp.mean(x * x, axis=-1, keepdims=True) + EPS)
            h_ref[rows, :] = (x * r * gain + shift).astype(BF16)
            return carry
        lax.fori_loop(0, x_ref.shape[0] // rc, chunk, 0, unroll=4)

    o_ref[...] = _dot(h_ref[...], w_ref[0]).astype(o_ref.dtype)


def _inproj(x2, g, mod, w, layer, seq):
    n, d = x2.shape
    ncol = w.shape[2]
    tm, tn = 512, ncol // 2
    per_b = seq // tm
    est = 2 * tm * d * 4 + tm * d * 2 + 2 * d * tn * 2 + 2 * tm * tn * 2 + tm * tn * 4
    return pl.pallas_call(
        _inproj_kernel,
        grid=(n // tm, ncol // tn),
        in_specs=[pl.BlockSpec((tm, d), lambda i, j: (i, 0)),
                  pl.BlockSpec((1, d), lambda i, j: (0, 0)),
                  pl.BlockSpec((1, 6, d), lambda i, j: (i // per_b, 0, 0)),
                  pl.BlockSpec((1, d, tn), lambda i, j: (layer, 0, j))],
        out_specs=pl.BlockSpec((tm, tn), lambda i, j: (i, j)),
        out_shape=jax.ShapeDtypeStruct((n, ncol), BF16),
        scratch_shapes=[pltpu.VMEM((tm, d), BF16)],
        compiler_params=_cparams(("parallel", "arbitrary"), est),
    )(x2, g.reshape(1, d), mod, w)


def _rope128(y, tc, ts):
    return y * tc + pltpu.roll(y, 64, 1) * ts


def _mla_prep_kernel(ql_ref, kvl_ref, kr_ref, gq_ref, gkv_ref, wq_ref, wkv_ref, tc_ref, ts_ref,
                     q_ref, k_ref, v_ref):
    scale = (MLA_NOPE + MLA_ROPE) ** -0.5
    tc = tc_ref[...]
    ts = ts_ref[...]
    qn = _rms(ql_ref[...].astype(F32), gq_ref[...]).astype(BF16)
    yq = _dot(qn, wq_ref[...])
    kvn = _rms(kvl_ref[...].astype(F32), gkv_ref[...]).astype(BF16)
    ykv = _dot(kvn, wkv_ref[...])
    k_roped = _rope128(kr_ref[...].astype(F32), tc, ts).astype(BF16)
    for h in range(MLA_HEADS):
        o = 2 * LANES * h
        q_ref[:, o:o + LANES] = (yq[:, o:o + LANES] * scale).astype(BF16)
        q_ref[:, o + LANES:o + 2 * LANES] = (_rope128(yq[:, o + LANES:o + 2 * LANES], tc, ts) * scale).astype(BF16)
        k_ref[:, o:o + LANES] = ykv[:, LANES * h:LANES * (h + 1)].astype(BF16)
        k_ref[:, o + LANES:o + 2 * LANES] = k_roped
    v_ref[...] = ykv[:, MLA_HEADS * LANES:].astype(BF16)


def _mla_prep(proj, gq, gkv, wq, wkv, tc, ts, seq):
    n = proj.shape[0]
    tm = 512
    per_b = seq // tm
    hq = MLA_HEADS * 2 * LANES
    return pl.pallas_call(
        _mla_prep_kernel,
        grid=(n // tm,),
        in_specs=[pl.BlockSpec((tm, MLA_Q_RANK), lambda i: (i, P_QLAT // MLA_Q_RANK)),
                  pl.BlockSpec((tm, MLA_KV_RANK), lambda i: (i, P_KVLAT // MLA_KV_RANK)),
                  pl.BlockSpec((tm, LANES), lambda i: (i, P_KROPE // LANES)),
                  pl.BlockSpec((1, MLA_Q_RANK), lambda i: (0, 0)),
                  pl.BlockSpec((1, MLA_KV_RANK), lambda i: (0, 0)),
                  pl.BlockSpec((MLA_Q_RANK, hq), lambda i: (0, 0)),
                  pl.BlockSpec((MLA_KV_RANK, 2 * GROUP_WIDTH), lambda i: (0, 0)),
                  pl.BlockSpec((tm, LANES), lambda i: (i % per_b, 0)),
                  pl.BlockSpec((tm, LANES), lambda i: (i % per_b, 0))],
        out_specs=[pl.BlockSpec((tm, hq), lambda i: (i, 0)),
                   pl.BlockSpec((tm, hq), lambda i: (i, 0)),
                   pl.BlockSpec((tm, GROUP_WIDTH), lambda i: (i, 0))],
        out_shape=[jax.ShapeDtypeStruct((n, hq), BF16),
                   jax.ShapeDtypeStruct((n, hq), BF16),
                   jax.ShapeDtypeStruct((n, GROUP_WIDTH), BF16)],
        compiler_params=_cparams(("parallel",), 16 * 1024 * 1024),
    )(proj, proj, proj, gq.reshape(1, -1), gkv.reshape(1, -1), wq, wkv, tc, ts)


def _mla_flash_kernel(q_ref, k_ref, v_ref, o_ref):
    i = pl.program_id(2)
    tq = q_ref.shape[0]
    tk = tq
    q = q_ref[...]

    def step(j, carry, masked):
        m, l, acc = carry
        k = k_ref[pl.ds(pl.multiple_of(j * tk, tk), tk), :]
        v = v_ref[pl.ds(pl.multiple_of(j * tk, tk), tk), :]
        s = _dot_nt(q, k)
        if masked:
            qc = lax.broadcasted_iota(jnp.int32, (tq, tk), 0) // CHUNK
            kc = lax.broadcasted_iota(jnp.int32, (tq, tk), 1) // CHUNK
            s = jnp.where(kc <= qc, s, NEG_BIG)
        m_new = jnp.maximum(m, jnp.max(s, axis=-1, keepdims=True))
        alpha = jnp.exp(m - m_new)
        p = jnp.exp(s - m_new)
        l = alpha * l + jnp.sum(p, axis=-1, keepdims=True)
        acc = alpha * acc + _dot(p.astype(BF16), v)
        return m_new, l, acc

    init = (jnp.full((tq, 1), NEG_BIG, F32), jnp.zeros((tq, 1), F32), jnp.zeros((tq, v_ref.shape[1]), F32))
    carry = lax.fori_loop(0, i // 2, lambda jj, c: step(2 * jj + 1, step(2 * jj, c, False), False), init)
    carry = lax.cond(lax.rem(i, 2) == 1, lambda c: step(i - 1, c, False), lambda c: c, carry)
    m, l, acc = step(i, carry, True)
    o_ref[...] = (acc / l).astype(o_ref.dtype)


def _mla_flash(q, k, v, batch, seq):
    n = q.shape[0]
    tq = ATT_TQ
    nq = seq // tq
    est = 2 * (seq * 2 * LANES * 2 + seq * LANES * 2) + 8 * tq * tq * 4
    return pl.pallas_call(
        _mla_flash_kernel,
        grid=(batch, MLA_HEADS, nq),
        in_specs=[pl.BlockSpec((tq, 2 * LANES), lambda b, h, i: (b * nq + i, h)),
                  pl.BlockSpec((seq, 2 * LANES), lambda b, h, i: (b, h)),
                  pl.BlockSpec((seq, LANES), lambda b, h, i: (b, h))],
        out_specs=pl.BlockSpec((tq, LANES), lambda b, h, i: (b * nq + i, h)),
        out_shape=jax.ShapeDtypeStruct((n, GROUP_WIDTH), BF16),
        compiler_params=_cparams(("parallel", "parallel", "arbitrary"), est),
    )(q, k, v)


def _sb_kernel(q_ref, k_ref, v_ref, o_ref):
    i = pl.program_id(2)
    tq = q_ref.shape[0]
    tk = tq
    sub = SB_SUB
    scale = q_ref.shape[1] ** -0.5
    q = q_ref[...]
    tri = (lax.broadcasted_iota(jnp.int32, (sub, sub), 0)
           > lax.broadcasted_iota(jnp.int32, (sub, sub), 1)).astype(BF16)

    def step(j, carry, masked):
        run, acc = carry
        k = k_ref[pl.ds(pl.multiple_of(j * tk, tk), tk), :]
        v = v_ref[pl.ds(pl.multiple_of(j * tk, tk), tk), :]
        u = _dot_nt(q, k) * (scale * LOG2_E)
        log_beta = jnp.minimum(u, 0.0) - jnp.log2(1.0 + jnp.exp2(-jnp.abs(u)))
        log_keep = log_beta - u
        if masked:
            vis = (lax.broadcasted_iota(jnp.int32, (tq, tk), 1)
                   < lax.broadcasted_iota(jnp.int32, (tq, tk), 0))
            log_keep = jnp.where(vis, log_keep, 0.0)
        keep_b = log_keep.astype(BF16)
        pieces = []
        for blk in range(tk // sub - 1, -1, -1):
            sl = slice(blk * sub, (blk + 1) * sub)
            pieces.append(_dot(keep_b[:, sl], tri) + run)
            run = run + jnp.sum(log_keep[:, sl], axis=-1, keepdims=True)
        later = jnp.concatenate(pieces[::-1], axis=1)
        att = jnp.exp2(log_beta + later)
        if masked:
            att = jnp.where(vis, att, 0.0)
        acc = acc + _dot(att.astype(BF16), v)
        return run, acc

    carry = step(i, (jnp.zeros((tq, 1), F32), jnp.zeros((tq, v_ref.shape[1]), F32)), True)
    carry = lax.fori_loop(0, i // 2, lambda jj, c: step(i - 2 - 2 * jj, step(i - 1 - 2 * jj, c, False), False), carry)
    run, acc = lax.cond(lax.rem(i, 2) == 1, lambda c: step(0, c, False), lambda c: c, carry)
    o_ref[...] = acc.astype(o_ref.dtype)


def _sb_attention(proj, batch, seq):
    n = proj.shape[0]
    tq = ATT_TQ
    nq = seq // tq
    cq, ck, cv = P_C // LANES, (P_C + GROUP_WIDTH) // LANES, (P_C + 2 * GROUP_WIDTH) // LANES
    est = 4 * seq * LANES * 2 + 10 * tq * tq * 4
    return pl.pallas_call(
        _sb_kernel,
        grid=(batch, SB_HEADS, nq),
        in_specs=[pl.BlockSpec((tq, LANES), lambda b, h, i: (b * nq + i, cq + h)),
                  pl.BlockSpec((seq, LANES), lambda b, h, i: (b, ck + h)),
                  pl.BlockSpec((seq, LANES), lambda b, h, i: (b, cv + h))],
        out_specs=pl.BlockSpec((tq, LANES), lambda b, h, i: (b * nq + i, h)),
        out_shape=jax.ShapeDtypeStruct((n, GROUP_WIDTH), BF16),
        compiler_params=_cparams(("parallel", "parallel", "arbitrary"), est),
    )(proj, proj, proj)


def _ca_kernel(q_ref, k_ref, v_ref, tab_ref, o_ref, kpad_ref, vpad_ref):
    i = pl.program_id(1)
    tq = q_ref.shape[0]
    past = CA_PAST_CHUNKS * CHUNK
    win = tq + past
    scale = LANES ** -0.5

    @pl.when(i == 0)
    def _():
        kpad_ref[0:past, :] = jnp.zeros((past, kpad_ref.shape[1]), kpad_ref.dtype)
        vpad_ref[0:past, :] = jnp.zeros((past, vpad_ref.shape[1]), vpad_ref.dtype)
        kpad_ref[past:, :] = k_ref[...]
        vpad_ref[past:, :] = v_ref[...]

    start = pl.multiple_of(i * tq, tq)
    for h in range(CA_HEADS):
        cols = slice(h * LANES, (h + 1) * LANES)
        kw = kpad_ref[pl.ds(start, win), cols]
        vw = vpad_ref[pl.ds(start, win), cols]
        s = _dot_nt(q_ref[:, cols], kw) * scale + tab_ref[0, h]
        m = jnp.max(s, axis=-1, keepdims=True)
        p = jnp.exp(s - m)
        l = jnp.sum(p, axis=-1, keepdims=True)
        o_ref[:, cols] = (_dot(p.astype(BF16), vw) / l).astype(o_ref.dtype)


def _ca_table(rel_bias, tq):
    past = CA_PAST_CHUNKS * CHUNK
    win = tq + past
    r = jnp.arange(tq)[:, None]
    c = jnp.arange(win)[None, :]
    ring = 1024
    assert ring >= tq + win - 1
    m = jnp.arange(ring)
    m = jnp.where(m >= win, m - ring, m)
    idx = jnp.clip(past - m, -(CHUNK - 1), CA_REL_PAST) + (CHUNK - 1)
    ringvals = rel_bias[:, idx].astype(F32)
    nh = rel_bias.shape[0]
    bias = jnp.tile(ringvals, (1, tq))[:, :tq * (ring - 1)].reshape(nh, tq, ring - 1)[:, :, :win]
    qc = r // CHUNK
    kc = c // CHUNK
    vis = (kc >= qc) & (kc <= qc + CA_PAST_CHUNKS)
    tabs = []
    n_var = past // tq + 1
    for v in range(n_var):
        first = (past - v * tq) // CHUNK if v < n_var - 1 else 0
        ok = vis & (kc >= first)
        tabs.append(jnp.where(ok[None], bias, NEG_BIG))
    return jnp.stack(tabs)


def _ca_attention(proj, table, batch, seq):
    n = proj.shape[0]
    tq = CA_TQ
    nq = seq // tq
    past = CA_PAST_CHUNKS * CHUNK
    win = tq + past
    n_var = table.shape[0]
    gw = GROUP_WIDTH
    cq, ck, cv = P_D // gw, (P_D + gw) // gw, (P_D + 2 * gw) // gw
    est = (4 * seq * gw * 2 + 2 * (seq + past) * gw * 2 + 2 * CA_HEADS * tq * win * 4 + 8 * tq * win * 4
           + 4 * tq * gw * 2)
    return pl.pallas_call(
        _ca_kernel,
        grid=(batch, nq),
        in_specs=[pl.BlockSpec((tq, gw), lambda b, i: (b * nq + i, cq)),
                  pl.BlockSpec((seq, gw), lambda b, i: (b, ck)),
                  pl.BlockSpec((seq, gw), lambda b, i: (b, cv)),
                  pl.BlockSpec((1, CA_HEADS, tq, win), lambda b, i: (jnp.minimum(i, n_var - 1), 0, 0, 0))],
        out_specs=pl.BlockSpec((tq, gw), lambda b, i: (b * nq + i, 0)),
        out_shape=jax.ShapeDtypeStruct((n, gw), BF16),
        scratch_shapes=[pltpu.VMEM((seq + past, gw), BF16), pltpu.VMEM((seq + past, gw), BF16)],
        compiler_params=_cparams(("parallel", "arbitrary"), est),
    )(proj, proj, proj, table)


def _ssd_kernel(z_ref, xbc_ref, dt_ref, cw_ref, cb_ref, dtb_ref, alog_ref, dskip_ref, ng_ref, ex_ref,
                o_ref, xbuf_ref, state_ref, y_ref):
    c = pl.program_id(1)
    lc = z_ref.shape[0]
    halo = SUBLANES
    gw = GROUP_WIDTH // SSM_GROUPS
    hpg = SSM_HEADS // SSM_GROUPS

    @pl.when(c == 0)
    def _():
        xbuf_ref[0:halo, :] = jnp.zeros((halo, xbuf_ref.shape[1]), F32)
        state_ref[...] = jnp.zeros(state_ref.shape, F32)

    xbuf_ref[halo:halo + lc, :] = xbc_ref[...].astype(F32)
    acc = jnp.broadcast_to(cb_ref[...], (lc, SSM_CONV_DIM))
    for k in range(SSM_CONV):
        acc = acc + cw_ref[k:k + 1, :] * xbuf_ref[pl.ds(halo - (SSM_CONV - 1) + k, lc), :]
    xbuf_ref[0:halo, :] = xbuf_ref[lc:lc + halo, :]
    u = acc * jax.nn.sigmoid(acc)
    xs = u[:, :GROUP_WIDTH]
    bm = u[:, GROUP_WIDTH:GROUP_WIDTH + SSM_GROUPS * SSM_STATE].astype(BF16)
    cm = u[:, GROUP_WIDTH + SSM_GROUPS * SSM_STATE:].astype(BF16)

    dt = jax.nn.softplus(dt_ref[...].astype(F32) + dtb_ref[...])
    a = -jnp.exp(alog_ref[...])
    da = dt * a
    row = lax.broadcasted_iota(jnp.int32, (lc, lc), 0)
    col = lax.broadcasted_iota(jnp.int32, (lc, lc), 1)
    lower = col <= row
    cs = _dot3_left01(lower.astype(BF16), da)
    cs_last = cs[lc - 1:lc, :]
    cs_t = cs.T
    dt_t = dt.T
    ex = ex_ref[...]
    ecs_x = _dot3_right01(jnp.exp(cs), ex)
    wdec_x = _dot3_right01(jnp.exp(cs_last - cs) * dt, ex)
    xs_b = xs.astype(BF16)

    for g in range(SSM_GROUPS):
        bg = bm[:, g * SSM_STATE:(g + 1) * SSM_STATE]
        cg = cm[:, g * SSM_STATE:(g + 1) * SSM_STATE]
        cb = _dot_nt(cg, bg)
        for r in range(hpg):
            h = g * hpg + r
            diff = cs[:, h:h + 1] - cs_t[h:h + 1, :]
            wgt = cb * jnp.exp(jnp.where(lower, diff, NEG_BIG)) * dt_t[h:h + 1, :]
            y_ref[:, h * SSM_HEAD_DIM:(h + 1) * SSM_HEAD_DIM] = _dot(
                wgt.astype(BF16), xs_b[:, h * SSM_HEAD_DIM:(h + 1) * SSM_HEAD_DIM])
        lo, hi = g * gw, (g + 1) * gw
        st = state_ref[g]
        y_off = _dot(cg, st.astype(BF16)) * ecs_x[:, lo:hi]
        y_ref[:, lo:hi] += y_off
        xw = (xs[:, lo:hi] * wdec_x[:, lo:hi]).astype(BF16)
        state_ref[g] = st * ecs_x[lc - 1:lc, lo:hi] + _dot_tn(bg, xw)

    zf = z_ref[...].astype(F32)
    y = (y_ref[...] + xs * dskip_ref[...]) * (zf * jax.nn.sigmoid(zf))
    for g in range(SSM_GROUPS):
        lo, hi = g * gw, (g + 1) * gw
        yg = y[:, lo:hi]
        yg = yg * lax.rsqrt(jnp.mean(yg * yg, axis=-1, keepdims=True) + EPS)
        o_ref[:, lo:hi] = (yg * ng_ref[:, lo:hi]).astype(o_ref.dtype)


def _ssd(proj, conv_w, conv_b, dt_bias, a_log, d_skip, norm_g, batch, seq):
    n = proj.shape[0]
    lc = SSD_LC
    nc = seq // lc
    dtb = jnp.zeros((1, LANES), F32).at[0, :SSM_HEADS].set(dt_bias)
    alog = jnp.full((1, LANES), NEG_BIG, F32).at[0, :SSM_HEADS].set(a_log)
    dskip = jnp.repeat(d_skip, SSM_HEAD_DIM).reshape(1, GROUP_WIDTH)
    ex = (jnp.arange(LANES)[:, None] == jnp.arange(GROUP_WIDTH)[None, :] // SSM_HEAD_DIM).astype(BF16)
    full = lambda shape: pl.BlockSpec(shape, lambda b, c: tuple(0 for _ in shape))
    return pl.pallas_call(
        _ssd_kernel,
        grid=(batch, nc),
        in_specs=[pl.BlockSpec((lc, GROUP_WIDTH), lambda b, c: (b * nc + c, P_Z // GROUP_WIDTH)),
                  pl.BlockSpec((lc, SSM_CONV_DIM), lambda b, c: (b * nc + c, P_XBC // SSM_CONV_DIM)),
                  pl.BlockSpec((lc, LANES), lambda b, c: (b * nc + c, P_DT // LANES)),
                  full((SSM_CONV, SSM_CONV_DIM)), full((1, SSM_CONV_DIM)), full((1, LANES)), full((1, LANES)),
                  full((1, GROUP_WIDTH)), full((1, GROUP_WIDTH)), full((LANES, GROUP_WIDTH))],
        out_specs=pl.BlockSpec((lc, GROUP_WIDTH), lambda b, c: (b * nc + c, 0)),
        out_shape=jax.ShapeDtypeStruct((n, GROUP_WIDTH), BF16),
        scratch_shapes=[pltpu.VMEM((lc + SUBLANES, SSM_CONV_DIM), F32),
                        pltpu.VMEM((SSM_GROUPS, SSM_STATE, GROUP_WIDTH // SSM_GROUPS), F32),
                        pltpu.VMEM((lc, GROUP_WIDTH), F32)],
        compiler_params=_cparams(("parallel", "arbitrary"), 16 * 1024 * 1024),
    )(proj, proj, proj, conv_w, conv_b.reshape(1, -1), dtb, alog, dskip, norm_g.reshape(1, -1), ex)


def _outproj_kernel(a_ref, b_ref, c_ref, d_ref, x_ref, gmix_ref, w_ref, mod_ref, gf_ref, rw_ref, rb_ref,
                    xo_ref, h_ref, e_ref, gt_ref):
    gw = GROUP_WIDTH
    a = _rms(a_ref[...].astype(F32), gmix_ref[0:1, :]).astype(BF16)
    c = _rms(c_ref[...].astype(F32), gmix_ref[1:2, :]).astype(BF16)
    d = _rms(d_ref[...].astype(F32), gmix_ref[2:3, :]).astype(BF16)
    mixed = (_dot(a, w_ref[0:gw, :]) + _dot(b_ref[...], w_ref[gw:2 * gw, :])
             + _dot(c, w_ref[2 * gw:3 * gw, :]) + _dot(d, w_ref[3 * gw:, :]))
    x = x_ref[...] + mod_ref[0, 2:3, :] * mixed
    xo_ref[...] = x
    h = _rms(x, gf_ref[...]) * (1.0 + mod_ref[0, 4:5, :]) + mod_ref[0, 3:4, :]
    _store_packed_rows(h_ref, h)

    hh = h.astype(BF16)
    hl = (h - hh.astype(F32)).astype(BF16)
    rw = rw_ref[...]
    wh = rw.astype(BF16)
    wl = (rw - wh.astype(F32)).astype(BF16)
    logits = _dot(hh, wh) + _dot(hh, wl) + _dot(hl, wh) + rb_ref[...]
    lane = lax.broadcasted_iota(jnp.int32, logits.shape, 1)
    vals, idxs = [], []
    for _ in range(TOP_K):
        m = jnp.max(logits, axis=-1, keepdims=True)
        idx = jnp.min(jnp.where(logits == m, lane, LANES), axis=-1, keepdims=True)
        vals.append(m)
        idxs.append(idx)
        logits = jnp.where(lane == idx, -jnp.inf, logits)
    ex = [jnp.exp(v - vals[0]) for v in vals]
    tot = ex[0] + ex[1] + ex[2] + ex[3]
    e_out = jnp.zeros(logits.shape, jnp.int32)
    g_out = jnp.zeros(logits.shape, F32)
    for k in range(TOP_K):
        e_out = jnp.where(lane == k, idxs[k], e_out)
        g_out = jnp.where(lane == k, ex[k] / tot, g_out)
    e_ref[...] = e_out
    gt_ref[...] = g_out


def _outproj(oa, ob, oc, od, x2, gmix, w_out, mod, gf, rw, rb, seq):
    n, d = x2.shape
    tm = 512
    per_b = seq // tm
    gw = GROUP_WIDTH
    row = lambda width: pl.BlockSpec((tm, width), lambda i: (i, 0))
    full = lambda shape: pl.BlockSpec(shape, lambda i: tuple(0 for _ in shape))
    est = 2 * d * d * 2 + 4 * tm * d * 4 + 2 * tm * d * 2 + 8 * tm * gw * 2 + 6 * tm * d * 4
    return pl.pallas_call(
        _outproj_kernel,
        grid=(n // tm,),
        in_specs=[row(gw), row(gw), row(gw), row(gw), row(d), full((3, gw)), full((d, d)),
                  pl.BlockSpec((1, 6, d), lambda i: (i // per_b, 0, 0)), full((1, d)),
                  full((d, LANES)), full((1, LANES))],
        out_specs=[row(d), pl.BlockSpec((tm * ROW_SUBLANES, LANES), lambda i: (i, 0)), row(LANES), row(LANES)],
        out_shape=[jax.ShapeDtypeStruct((n, d), F32), jax.ShapeDtypeStruct((n * ROW_SUBLANES, LANES), U32),
                   jax.ShapeDtypeStruct((n, LANES), jnp.int32), jax.ShapeDtypeStruct((n, LANES), F32)],
        compiler_params=_cparams(("parallel",), est),
    )(oa, ob, oc, od, x2, gmix, w_out, mod, gf.reshape(1, d), rw, rb)


def _row_copy(src_ref, src_row, dst_ref, dst_row, sem):
    rs = ROW_SUBLANES
    return pltpu.make_async_copy(src_ref.at[pl.ds(pl.multiple_of(src_row * rs, rs), rs)],
                                 dst_ref.at[pl.ds(pl.multiple_of(dst_row * rs, rs), rs)], sem)


def _expert_kernel(be_ref, nu_ref, tok_first_ref, step_ref, dst_last_ref, hp_ref, w1_ref, b1_ref, w2_ref,
                   b2_ref, y_ref, xa_ref, xb_ref, ya_ref, yb_ref, gsem, ssem):
    i = pl.program_id(0)
    nu = nu_ref[0]
    tm = MOE_BLOCK
    rs = ROW_SUBLANES

    def start_rows(copy_of_row):
        for r in range(tm):
            copy_of_row(r).start(priority=r % 2)

    def wait_rows(src_ref, dst_ref, sem):
        for _ in range(tm):
            _row_copy(src_ref, 0, dst_ref, 0, sem).wait()

    @pl.when(i == 0)
    def _():
        start_rows(lambda r: _row_copy(hp_ref, tok_first_ref[0, 0, r], xa_ref, r, gsem.at[0]))
        yb_ref[...] = jnp.zeros(yb_ref.shape, U32)
        spare = pltpu.make_async_copy(yb_ref, y_ref.at[pl.ds(y_ref.shape[0] - 2 * tm * rs, tm * rs)], ssem.at[0])
        spare.start()
        spare.wait()

    def body(xcur, xnext, ycur, yprev, gcur, gnext, scur, sprev):
        wait_rows(hp_ref, xcur, gcur)
        start_rows(lambda r: _row_copy(hp_ref, step_ref[0, 0, r], xnext, r, gnext))
        start_rows(lambda r: _row_copy(yprev, r, y_ref, step_ref[0, 0, tm + r], sprev))
        words = _load_packed_rows(xcur)
        x = jnp.concatenate([_unpack_lo(u).astype(BF16) for u in words]
                            + [_unpack_hi(u).astype(BF16) for u in words], axis=1)
        cw = EXPERT_COL_CHUNK
        hu = jnp.concatenate([_dot(x, w1_ref[0, :, c:c + cw].astype(BF16)) for c in range(0, w1_ref.shape[2], cw)],
                             axis=1) + b1_ref[0]
        gate = jnp.minimum(hu[:, :D_EXPERT], SWIGLU_LIMIT)
        up = jnp.clip(hu[:, D_EXPERT:], -SWIGLU_LIMIT, SWIGLU_LIMIT)
        act = (gate * jax.nn.sigmoid(SWIGLU_ALPHA * gate) * (up + 1.0)).astype(BF16)
        y = jnp.concatenate([_dot(act, w2_ref[0, :, c:c + cw].astype(BF16)) for c in range(0, w2_ref.shape[2], cw)],
                            axis=1) + b2_ref[0]

        @pl.when(i > 0)
        def _():
            wait_rows(ycur, y_ref, scur)

        _store_packed_rows(ycur, y)

        @pl.when(i == nu - 1)
        def _():
            start_rows(lambda r: _row_copy(ycur, r, y_ref, dst_last_ref[0, 0, r], scur))
            wait_rows(yprev, y_ref, sprev)
            wait_rows(ycur, y_ref, scur)
            wait_rows(hp_ref, xnext, gnext)

    even = lax.rem(i, 2) == 0

    @pl.when(jnp.logical_and(i < nu, even))
    def _():
        body(xa_ref, xb_ref, ya_ref, yb_ref, gsem.at[0], gsem.at[1], ssem.at[0], ssem.at[1])

    @pl.when(jnp.logical_and(i < nu, jnp.logical_not(even)))
    def _():
        body(xb_ref, xa_ref, yb_ref, ya_ref, gsem.at[1], gsem.at[0], ssem.at[1], ssem.at[0])


def _experts(hp, row_tok, row_dst, block_e, n_used, n_slots, w1, b1, w2, b2, layer=0):
    tm = MOE_BLOCK
    nblk = row_tok.shape[0]
    d, de2 = w1.shape[1], w1.shape[2]
    de = w2.shape[1]
    rs = ROW_SUBLANES
    block_e = block_e + layer * N_EXPERTS
    last = lambda i, nu: jnp.minimum(i, nu[0] - 1)
    est = (2 * (d * de2 + de * d) * w1.dtype.itemsize + 4 * tm * rs * LANES * 4 + 4 * tm * d * 4
           + 3 * tm * de2 * 4)
    first_prev = (n_slots - tm + jnp.arange(tm, dtype=jnp.int32)).reshape(1, 1, tm)
    step_rows = jnp.concatenate([jnp.concatenate([row_tok[1:], row_tok[-1:]], axis=0),
                                 jnp.concatenate([first_prev, row_dst[:-1]], axis=0)], axis=2)
    dst_last = lax.dynamic_slice_in_dim(row_dst, n_used[0] - 1, 1, axis=0)
    smem = lambda width, f: pl.BlockSpec((1, 1, width), f, memory_space=pltpu.SMEM)
    grid_spec = pltpu.PrefetchScalarGridSpec(
        num_scalar_prefetch=2,
        grid=(nblk,),
        in_specs=[smem(tm, lambda i, be, nu: (0, 0, 0)),
                  smem(2 * tm, lambda i, be, nu: (last(i, nu), 0, 0)),
                  smem(tm, lambda i, be, nu: (0, 0, 0)),
                  pl.BlockSpec(memory_space=pl.ANY),
                  pl.BlockSpec((1, d, de2), lambda i, be, nu: (be[last(i, nu)], 0, 0)),
                  pl.BlockSpec((1, 1, de2), lambda i, be, nu: (be[last(i, nu)], 0, 0)),
                  pl.BlockSpec((1, de, d), lambda i, be, nu: (be[last(i, nu)], 0, 0)),
                  pl.BlockSpec((1, 1, d), lambda i, be, nu: (be[last(i, nu)], 0, 0))],
        out_specs=pl.BlockSpec(memory_space=pl.ANY),
        scratch_shapes=[pltpu.VMEM((tm * rs, LANES), U32) for _ in range(4)]
        + [pltpu.SemaphoreType.DMA((2,)), pltpu.SemaphoreType.DMA((2,))],
    )
    return pl.pallas_call(
        _expert_kernel,
        grid_spec=grid_spec,
        out_shape=jax.ShapeDtypeStruct((n_slots * rs, LANES), U32),
        compiler_params=_cparams(("arbitrary",), est),
    )(block_e, n_used, row_tok, step_rows, dst_last, hp, w1, b1.reshape(b1.shape[0], 1, de2), w2,
      b2.reshape(b2.shape[0], 1, d))


def _combine_kernel(y0_ref, y1_ref, y2_ref, y3_ref, g_ref, x_ref, mod_ref, o_ref):
    tm, d = x_ref.shape
    half = d // 2
    rs = ROW_SUBLANES
    rc = COMBINE_ROWS

    def chunk(c, carry):
        r0 = pl.multiple_of(c * rc, rc)
        rows = pl.ds(r0, rc)
        g = g_ref[rows, :]
        lo = [jnp.zeros((rc, LANES), F32) for _ in range(rs)]
        hi = [jnp.zeros((rc, LANES), F32) for _ in range(rs)]
        for k, y_ref in enumerate((y0_ref, y1_ref, y2_ref, y3_ref)):
            blk = y_ref[pl.ds(pl.multiple_of(r0 * rs, rc * rs), rc * rs), :]
            w = jnp.swapaxes(blk.reshape(rc // rs, rs, rs, LANES), 1, 2)
            gk = g[:, k:k + 1]
            for s in range(rs):
                u = w[:, s].reshape(rc, LANES)
                lo[s] = lo[s] + _unpack_lo(u) * gk
                hi[s] = hi[s] + _unpack_hi(u) * gk
        for s in range(rs):
            a, b = LANES * s, half + LANES * s
            o_ref[rows, a:a + LANES] = x_ref[rows, a:a + LANES] + mod_ref[0, 5:6, a:a + LANES] * lo[s]
            o_ref[rows, b:b + LANES] = x_ref[rows, b:b + LANES] + mod_ref[0, 5:6, b:b + LANES] * hi[s]
        return carry
    lax.fori_loop(0, tm // rc, chunk, 0)


def _combine(y4, gates, x2, mod, seq):
    n, d = x2.shape
    tm = 256
    per_b = seq // tm
    nt = n // tm
    rows = tm * ROW_SUBLANES
    est = 2 * TOP_K * rows * LANES * 4 + 4 * tm * d * 4 + 2 * tm * LANES * 4
    yspec = lambda k: pl.BlockSpec((rows, LANES), lambda i: (k * nt + i, 0))
    return pl.pallas_call(
        _combine_kernel,
        grid=(nt,),
        in_specs=[yspec(0), yspec(1), yspec(2), yspec(3),
                  pl.BlockSpec((tm, LANES), lambda i: (i, 0)),
                  pl.BlockSpec((tm, d), lambda i: (i, 0)),
                  pl.BlockSpec((1, 6, d), lambda i: (i // per_b, 0, 0))],
        out_specs=pl.BlockSpec((tm, d), lambda i: (i, 0)),
        out_shape=jax.ShapeDtypeStruct((n, d), F32),
        compiler_params=_cparams(("parallel",), est),
    )(y4, y4, y4, y4, gates, x2, mod)


def _final_norm_kernel(x_ref, g_ref, o_ref):
    o_ref[...] = _rms(x_ref[...], g_ref[...])


def _final_norm(x2, g):
    n, d = x2.shape
    tm = 512
    return pl.pallas_call(
        _final_norm_kernel,
        grid=(n // tm,),
        in_specs=[pl.BlockSpec((tm, d), lambda i: (i, 0)), pl.BlockSpec((1, d), lambda i: (0, 0))],
        out_specs=pl.BlockSpec((tm, d), lambda i: (i, 0)),
        out_shape=jax.ShapeDtypeStruct((n, d), F32),
        compiler_params=_cparams(("parallel",), 4 * tm * d * 4),
    )(x2, g.reshape(1, d))


def _swap_halves(w):
    half = w.shape[-1] // 2
    return jnp.concatenate([w[..., half:], w[..., :half]], axis=-1)


def _pack_w_in_kernel(w_ref, o_ref):
    w = w_ref[0]
    b0, c0 = A_COLS, A_COLS + B_COLS
    kr0 = MLA_Q_RANK + MLA_KV_RANK
    half = MLA_ROPE // 2
    pieces = [w[:, :MLA_Q_RANK],
              w[:, b0:b0 + GROUP_WIDTH + SSM_CONV_DIM],
              w[:, MLA_Q_RANK:kr0 + MLA_ROPE],
              w[:, kr0 + half:kr0 + MLA_ROPE], w[:, kr0:kr0 + half],
              w[:, c0 - SSM_HEADS:c0], jnp.zeros((w.shape[0], LANES - SSM_HEADS), w.dtype),
              w[:, c0:]]
    o_ref[0] = jnp.concatenate(pieces, axis=1).astype(BF16)


def _pack_w_in(w_in):
    nl, d, nc = w_in.shape
    tr = 256
    est = 2 * tr * nc * 4 + 2 * tr * P_COLS * 2 + 2 * tr * P_COLS * 4
    return pl.pallas_call(
        _pack_w_in_kernel,
        grid=(nl, d // tr),
        in_specs=[pl.BlockSpec((1, tr, nc), lambda l, i: (l, i, 0))],
        out_specs=pl.BlockSpec((1, tr, P_COLS), lambda l, i: (l, i, 0)),
        out_shape=jax.ShapeDtypeStruct((nl, d, P_COLS), BF16),
        compiler_params=_cparams(("parallel", "parallel"), est),
    )(w_in)


def _pack_wq(wq):
    r = wq.shape[0]
    w = wq.reshape(r, MLA_HEADS, MLA_NOPE + MLA_ROPE)
    rope = w[..., MLA_NOPE:]
    return jnp.concatenate([w[..., :MLA_NOPE], rope, _swap_halves(rope)], axis=-1).reshape(r, -1).astype(BF16)


def _pack_wkv(wkv):
    r = wkv.shape[0]
    w = wkv.reshape(r, MLA_HEADS, MLA_NOPE + MLA_V)
    return jnp.concatenate([w[..., :MLA_NOPE].reshape(r, -1), w[..., MLA_NOPE:].reshape(r, -1)], axis=-1).astype(BF16)


def _rope_tables(seq):
    inv = 1.0 / (ROPE_THETA ** (jnp.arange(0, MLA_ROPE, 2, dtype=F32) / MLA_ROPE))
    ang = jnp.arange(seq, dtype=F32)[:, None] * inv[None, :]
    cos, sin = jnp.cos(ang), jnp.sin(ang)
    zero = jnp.zeros((seq, LANES - MLA_ROPE), F32)
    return jnp.concatenate([cos, cos, zero], axis=1), jnp.concatenate([-sin, sin, zero], axis=1)


def _route(top_e, n_tok):
    n_assign = n_tok * TOP_K
    n_blocks = -(-(n_assign + N_EXPERTS * (MOE_BLOCK - 1)) // MOE_BLOCK)
    n_rows = n_blocks * MOE_BLOCK
    dest, meta = _route_dest(top_e)
    padded_end = meta[2, :N_EXPERTS]
    block_start = jnp.arange(n_blocks, dtype=jnp.int32) * MOE_BLOCK
    block_e = jnp.minimum(jnp.sum((padded_end[None, :] <= block_start[:, None]).astype(jnp.int32), axis=1),
                          N_EXPERTS - 1)
    n_used = (padded_end[N_EXPERTS - 1] // MOE_BLOCK).reshape(1)
    row_assign = _route_invert(meta, dest[:, :TOP_K].reshape(n_assign), n_rows)
    real = row_assign >= 0
    row = jnp.arange(n_rows, dtype=jnp.int32)
    row_tok = jnp.where(real, row_assign // TOP_K, 0)
    spare = n_assign + (row // MOE_BLOCK % 2) * MOE_BLOCK + row % MOE_BLOCK
    row_dst = jnp.where(real, row_assign % TOP_K * n_tok + row_assign // TOP_K, spare)
    shape = (n_blocks, 1, MOE_BLOCK)
    return row_tok.reshape(shape), row_dst.reshape(shape), block_e, n_used, n_assign + 2 * MOE_BLOCK


NORM_ROWS = 16
ROUTE_TM = 512
COMBINE_ROWS = 16


def _route_dest_kernel(e_ref, dest_ref, meta_ref, carry_ref, base_ref):
    p = pl.program_id(0)
    i = pl.program_id(1)
    tm = e_ref.shape[0]
    e = e_ref[...]
    lane = lax.broadcasted_iota(jnp.int32, (tm, LANES), 1)
    hit = [lane == e[:, k:k + 1] for k in range(TOP_K)]
    tot = sum(h.astype(F32) for h in hit)

    @pl.when(jnp.logical_and(p == 0, i == 0))
    def _():
        carry_ref[...] = jnp.zeros(carry_ref.shape, F32)
        meta_ref[...] = jnp.zeros(meta_ref.shape, jnp.int32)

    @pl.when(jnp.logical_and(p == 0, i == pl.num_programs(1) - 1))
    def _():
        counts = carry_ref[...] + jnp.sum(tot, axis=0, keepdims=True)
        padded = jnp.floor((counts + (MOE_BLOCK - 1)) * (1.0 / MOE_BLOCK)) * MOE_BLOCK
        upper = (lax.broadcasted_iota(jnp.int32, (LANES, LANES), 0)
                 <= lax.broadcasted_iota(jnp.int32, (LANES, LANES), 1)).astype(BF16)
        padded_end = _dot3_right01(jnp.broadcast_to(padded, (SUBLANES, LANES)), upper)[0:1]
        base_ref[...] = padded_end - padded
        meta_ref[0:1, :] = counts.astype(jnp.int32)
        meta_ref[1:2, :] = (padded_end - padded).astype(jnp.int32)
        meta_ref[2:3, :] = padded_end.astype(jnp.int32)
        carry_ref[...] = -jnp.sum(tot, axis=0, keepdims=True)

    @pl.when(p == 1)
    def _():
        earlier = (lax.broadcasted_iota(jnp.int32, (tm, tm), 1)
                   < lax.broadcasted_iota(jnp.int32, (tm, tm), 0)).astype(BF16)
        pos = _dot(earlier, tot.astype(BF16)) + carry_ref[...] + base_ref[...]
        dest = jnp.zeros((tm, LANES), jnp.int32)
        for k in range(TOP_K):
            dk = jnp.sum(jnp.where(hit[k], pos, 0.0), axis=-1, keepdims=True)
            dest = jnp.where(lane == k, dk.astype(jnp.int32), dest)
        dest_ref[...] = dest

    carry_ref[...] += jnp.sum(tot, axis=0, keepdims=True)


def _route_dest(top_e):
    n = top_e.shape[0]
    tm = ROUTE_TM
    return pl.pallas_call(
        _route_dest_kernel,
        grid=(2, n // tm),
        in_specs=[pl.BlockSpec((tm, LANES), lambda p, i: (i, 0))],
        out_specs=[pl.BlockSpec((tm, LANES), lambda p, i: (i * p, 0)),
                   pl.BlockSpec((SUBLANES, LANES), lambda p, i: (0, 0))],
        out_shape=[jax.ShapeDtypeStruct((n, LANES), jnp.int32), jax.ShapeDtypeStruct((SUBLANES, LANES), jnp.int32)],
        scratch_shapes=[pltpu.VMEM((1, LANES), F32), pltpu.VMEM((1, LANES), F32)],
        compiler_params=_cparams(("arbitrary", "arbitrary"), 8 * tm * tm * 4),
    )(top_e)


ROUTE_CHUNK = 2048


def _route_invert_kernel(meta_ref, dest_ref, out_ref):
    s = pl.program_id(0)
    ch = dest_ref.shape[2]
    per = SUBLANES

    def clear(r, carry):
        out_ref[r] = -1
        return carry

    @pl.when(s == 0)
    def _():
        def expert_padding(e, carry):
            return lax.fori_loop(meta_ref[1, e] + meta_ref[0, e], meta_ref[2, e], clear, carry)
        lax.fori_loop(0, N_EXPERTS, expert_padding, 0)
        lax.fori_loop(meta_ref[2, N_EXPERTS - 1], out_ref.shape[0], clear, 0)

    def place(j, carry):
        for u in range(per):
            a = j * per + u
            out_ref[dest_ref[0, 0, a]] = s * ch + a
        return carry
    lax.fori_loop(0, ch // per, place, 0)


def _route_invert(meta, dest_flat, n_rows):
    ch = ROUTE_CHUNK
    n_place = dest_flat.shape[0] // ch
    assert n_place * ch == dest_flat.shape[0]
    return pl.pallas_call(
        _route_invert_kernel,
        grid=(n_place,),
        in_specs=[pl.BlockSpec(memory_space=pltpu.SMEM),
                  pl.BlockSpec((1, 1, ch), lambda s: (s, 0, 0), memory_space=pltpu.SMEM)],
        out_specs=pl.BlockSpec(memory_space=pltpu.SMEM),
        out_shape=jax.ShapeDtypeStruct((n_rows,), jnp.int32),
        compiler_params=pltpu.CompilerParams(dimension_semantics=("arbitrary",)),
    )(meta, dest_flat.reshape(n_place, 1, ch))


def kernel(x, c, attn_norm, ffn_norm, mod_w, mod_b, w_in, mla_q_norm, mla_w_q_up, mla_kv_norm, mla_w_kv_up,
           ssm_conv_w, ssm_conv_b, ssm_dt_bias, ssm_a_log, ssm_d, ssm_norm, ca_rel_bias, mix_out_norm, w_out,
           router_w, router_b, moe_w1, moe_b1, moe_w2, moe_b2, final_norm):
    batch, seq, d = x.shape
    n = batch * seq
    depth = w_in.shape[0]
    x2 = x.reshape(n, d)
    mod_all = _mod_all(c, mod_w, mod_b).reshape(depth, batch, 6, d)
    tc, ts = _rope_tables(seq)
    w_in_packed = _pack_w_in(w_in)
    for l in range(depth):
        mod = mod_all[l]
        proj = _inproj(x2, attn_norm[l], mod, w_in_packed, l, seq)
        q, k, v = _mla_prep(proj, mla_q_norm[l], mla_kv_norm[l], _pack_wq(mla_w_q_up[l]),
                            _pack_wkv(mla_w_kv_up[l]), tc, ts, seq)
        out_a = _mla_flash(q, k, v, batch, seq)
        out_b = _ssd(proj, ssm_conv_w[l], ssm_conv_b[l], ssm_dt_bias[l], ssm_a_log[l], ssm_d[l], ssm_norm[l],
                     batch, seq)
        out_c = _sb_attention(proj, batch, seq)
        out_d = _ca_attention(proj, _ca_table(ca_rel_bias[l], CA_TQ), batch, seq)
        rw = jnp.zeros((d, LANES), F32).at[:, :N_EXPERTS].set(router_w[l])
        rb = jnp.full((1, LANES), NEG_BIG, F32).at[0, :N_EXPERTS].set(router_b[l])
        x2, hp, top_e, gates = _outproj(out_a, out_b, out_c, out_d, x2, mix_out_norm[l], w_out[l].astype(BF16),
                                        mod, ffn_norm[l], rw, rb, seq)
        row_tok, row_dst, block_e, n_used, n_slots = _route(top_e, n)
        y4 = _experts(hp, row_tok, row_dst, block_e, n_used, n_slots,
                      moe_w1.reshape(-1, *moe_w1.shape[2:]), moe_b1.reshape(-1, moe_b1.shape[2]),
                      moe_w2.reshape(-1, *moe_w2.shape[2:]), moe_b2.reshape(-1, moe_b2.shape[2]), layer=l)
        x2 = _combine(y4, gates, x2, mod, seq)
    return _final_norm(x2, final_norm).reshape(batch, seq, d)
```

```python
import functools
import math

import jax
import jax.numpy as jnp
from jax import lax
from jax.experimental import pallas as pl
from jax.experimental.pallas import tpu as pltpu

F32 = jnp.float32
BF16 = jnp.bfloat16
U32 = jnp.uint32

D_MODEL = 2048
DEPTH = 4
CHUNK = 64
EPS = 1e-6
GROUP_WIDTH = 512
MLA_HEADS = 4
MLA_NOPE = 128
MLA_ROPE = 64
MLA_V = 128
MLA_Q_RANK = 512
MLA_KV_RANK = 256
ROPE_THETA = 10000.0
SSM_HEAD_DIM = 64
SSM_HEADS = 8
SSM_GROUPS = 2
SSM_STATE = 128
SSM_CONV = 4
SSM_CONV_DIM = 1024
SB_HEADS = 4
CA_HEADS = 4
CA_PAST_CHUNKS = 8
CA_REL_PAST = 256
N_EXPERTS = 32
TOP_K = 4
D_EXPERT = 768
SWIGLU_ALPHA = 1.702
SWIGLU_LIMIT = 7.0

A_COLS = MLA_Q_RANK + MLA_KV_RANK + MLA_ROPE
B_COLS = GROUP_WIDTH + SSM_CONV_DIM + SSM_HEADS
C_COLS = 3 * GROUP_WIDTH

LANES = 128
SUBLANES = 8
VMEM_BYTES_V7X = 64 * 1024 * 1024
NEG_BIG = -1e30
LOG2_E = math.log2(math.e)

P_QLAT = 0
P_Z = 512
P_XBC = 1024
P_KVLAT = 2048
P_KROPE = 2304
P_DT = 2432
P_C = 2560
P_D = 4096
P_COLS = 5632

MOE_BLOCK = 256
EXPERT_COL_CHUNK = 256
ATT_TQ = 512
SB_SUB = 256
ROW_SUBLANES = 8
CA_TQ = 256
CA_WIN = CA_TQ + CA_PAST_CHUNKS * CHUNK
SSD_LC = 256


def _vmem_limit(nbytes):
    return int(min(max(2 * nbytes, 16 * 1024 * 1024), VMEM_BYTES_V7X - 8 * 1024 * 1024))


def _cparams(sem, nbytes):
    return pltpu.CompilerParams(dimension_semantics=sem, vmem_limit_bytes=_vmem_limit(nbytes))


def _rms(x, g):
    return x * lax.rsqrt(jnp.mean(x * x, axis=-1, keepdims=True) + EPS) * g


def _split3(x):
    hi = x.astype(BF16)
    r1 = x - hi.astype(F32)
    mid = r1.astype(BF16)
    lo = (r1 - mid.astype(F32)).astype(BF16)
    return hi, mid, lo


def _dot(a, b):
    return jnp.dot(a, b, preferred_element_type=F32)


def _dot_nt(a, b):
    return lax.dot_general(a, b, (((1,), (1,)), ((), ())), preferred_element_type=F32)


def _dot_tn(a, b):
    return lax.dot_general(a, b, (((0,), (0,)), ((), ())), preferred_element_type=F32)


def _dot3_left01(t01, x):
    hi, mid, lo = _split3(x)
    return _dot(t01, hi) + _dot(t01, mid) + _dot(t01, lo)


def _dot3_right01(x, t01):
    hi, mid, lo = _split3(x)
    return _dot(hi, t01) + _dot(mid, t01) + _dot(lo, t01)


def _bf16_bits(x):
    u = lax.bitcast_convert_type(x, U32)
    return (u + jnp.uint32(0x7FFF) + ((u >> 16) & jnp.uint32(1))) >> 16


def _store_packed_rows(ref, y):
    tm, d = y.shape
    half = d // 2
    rs = ROW_SUBLANES
    words = []
    for s in range(rs):
        lo = _bf16_bits(y[:, LANES * s:LANES * (s + 1)])
        hi = _bf16_bits(y[:, half + LANES * s:half + LANES * (s + 1)])
        words.append(((hi << 16) | lo).reshape(tm // rs, rs, LANES))
    ref[...] = jnp.swapaxes(jnp.stack(words, axis=1), 1, 2).reshape(tm * rs, LANES)


def _load_packed_rows(ref):
    rs = ROW_SUBLANES
    tm = ref.shape[0] // rs
    w = jnp.swapaxes(ref[...].reshape(tm // rs, rs, rs, LANES), 1, 2)
    return [w[:, s].reshape(tm, LANES) for s in range(rs)]


def _unpack_lo(u):
    return lax.bitcast_convert_type(u << 16, F32)


def _unpack_hi(u):
    return lax.bitcast_convert_type(u & jnp.uint32(0xFFFF0000), F32)


def _mod_kernel(ct_ref, w_ref, b_ref, o_ref):
    k = pl.program_id(2)
    nb = o_ref.shape[1]

    @pl.when(k == 0)
    def _():
        o_ref[0] = jnp.broadcast_to(b_ref[0], o_ref.shape[1:])

    ct = ct_ref[...]
    cond = ct * jax.nn.sigmoid(ct)
    w = w_ref[0]
    for b in range(nb):
        o_ref[0, b:b + 1, :] += jnp.sum(w * cond[:, b:b + 1], axis=0, keepdims=True)


def _mod_all(c, mod_w, mod_b):
    nb, d = c.shape
    nl, _, nout = mod_w.shape
    tk, tn = 512, 2048
    ct = c.T
    return pl.pallas_call(
        _mod_kernel,
        grid=(nl, nout // tn, d // tk),
        in_specs=[pl.BlockSpec((tk, nb), lambda l, j, k: (k, 0)),
                  pl.BlockSpec((1, tk, tn), lambda l, j, k: (l, k, j)),
                  pl.BlockSpec((1, 1, tn), lambda l, j, k: (l, 0, j))],
        out_specs=pl.BlockSpec((1, nb, tn), lambda l, j, k: (l, 0, j)),
        out_shape=jax.ShapeDtypeStruct((nl, nb, nout), F32),
        compiler_params=_cparams(("parallel", "parallel", "arbitrary"), 2 * tk * tn * 4),
    )(ct, mod_w, mod_b.reshape(nl, 1, nout))


def _inproj_kernel(x_ref, g_ref, mod_ref, w_ref, o_ref, h_ref):
    @pl.when(pl.program_id(1) == 0)
    def _():
        rc = NORM_ROWS
        gain = g_ref[...] * (1.0 + mod_ref[0, 1:2, :])
        shift = mod_ref[0, 0:1, :]

        def chunk(c, carry):
            rows = pl.ds(pl.multiple_of(c * rc, rc), rc)
            x = x_ref[rows, :]
            r = lax.rsqrt(jnp.mean(x * x, axis=-1, keepdims=True) + EPS)
            h_ref[rows, :] = (x * r * gain + shift).astype(BF16)
            return carry
        lax.fori_loop(0, x_ref.shape[0] // rc, chunk, 0, unroll=4)

    o_ref[...] = _dot(h_ref[...], w_ref[0]).astype(o_ref.dtype)


def _inproj(x2, g, mod, w, layer, seq):
    n, d = x2.shape
    ncol = w.shape[2]
    tm, tn = 512, ncol // 2
    per_b = seq // tm
    est = 2 * tm * d * 4 + tm * d * 2 + 2 * d * tn * 2 + 2 * tm * tn * 2 + tm * tn * 4
    return pl.pallas_call(
        _inproj_kernel,
        grid=(n // tm, ncol // tn),
        in_specs=[pl.BlockSpec((tm, d), lambda i, j: (i, 0)),
                  pl.BlockSpec((1, d), lambda i, j: (0, 0)),
                  pl.BlockSpec((1, 6, d), lambda i, j: (i // per_b, 0, 0)),
                  pl.BlockSpec((1, d, tn), lambda i, j: (layer, 0, j))],
        out_specs=pl.BlockSpec((tm, tn), lambda i, j: (i, j)),
        out_shape=jax.ShapeDtypeStruct((n, ncol), BF16),
        scratch_shapes=[pltpu.VMEM((tm, d), BF16)],
        compiler_params=_cparams(("parallel", "arbitrary"), est),
    )(x2, g.reshape(1, d), mod, w)


def _rope128(y, tc, ts):
    return y * tc + pltpu.roll(y, 64, 1) * ts


def _mla_prep_kernel(ql_ref, kvl_ref, kr_ref, gq_ref, gkv_ref, wq_ref, wkv_ref, tc_ref, ts_ref,
                     q_ref, k_ref, v_ref):
    scale = (MLA_NOPE + MLA_ROPE) ** -0.5
    tc = tc_ref[...]
    ts = ts_ref[...]
    qn = _rms(ql_ref[...].astype(F32), gq_ref[...]).astype(BF16)
    yq = _dot(qn, wq_ref[...])
    kvn = _rms(kvl_ref[...].astype(F32), gkv_ref[...]).astype(BF16)
    ykv = _dot(kvn, wkv_ref[...])
    k_roped = _rope128(kr_ref[...].astype(F32), tc, ts).astype(BF16)
    for h in range(MLA_HEADS):
        o = 2 * LANES * h
        q_ref[:, o:o + LANES] = (yq[:, o:o + LANES] * scale).astype(BF16)
        q_ref[:, o + LANES:o + 2 * LANES] = (_rope128(yq[:, o + LANES:o + 2 * LANES], tc, ts) * scale).astype(BF16)
        k_ref[:, o:o + LANES] = ykv[:, LANES * h:LANES * (h + 1)].astype(BF16)
        k_ref[:, o + LANES:o + 2 * LANES] = k_roped
    v_ref[...] = ykv[:, MLA_HEADS * LANES:].astype(BF16)


def _mla_prep(proj, gq, gkv, wq, wkv, tc, ts, seq):
    n = proj.shape[0]
    tm = 512
    per_b = seq // tm
    hq = MLA_HEADS * 2 * LANES
    return pl.pallas_call(
        _mla_prep_kernel,
        grid=(n // tm,),
        in_specs=[pl.BlockSpec((tm, MLA_Q_RANK), lambda i: (i, P_QLAT // MLA_Q_RANK)),
                  pl.BlockSpec((tm, MLA_KV_RANK), lambda i: (i, P_KVLAT // MLA_KV_RANK)),
                  pl.BlockSpec((tm, LANES), lambda i: (i, P_KROPE // LANES)),
                  pl.BlockSpec((1, MLA_Q_RANK), lambda i: (0, 0)),
                  pl.BlockSpec((1, MLA_KV_RANK), lambda i: (0, 0)),
                  pl.BlockSpec((MLA_Q_RANK, hq), lambda i: (0, 0)),
                  pl.BlockSpec((MLA_KV_RANK, 2 * GROUP_WIDTH), lambda i: (0, 0)),
                  pl.BlockSpec((tm, LANES), lambda i: (i % per_b, 0)),
                  pl.BlockSpec((tm, LANES), lambda i: (i % per_b, 0))],
        out_specs=[pl.BlockSpec((tm, hq), lambda i: (i, 0)),
                   pl.BlockSpec((tm, hq), lambda i: (i, 0)),
                   pl.BlockSpec((tm, GROUP_WIDTH), lambda i: (i, 0))],
        out_shape=[jax.ShapeDtypeStruct((n, hq), BF16),
                   jax.ShapeDtypeStruct((n, hq), BF16),
                   jax.ShapeDtypeStruct((n, GROUP_WIDTH), BF16)],
        compiler_params=_cparams(("parallel",), 16 * 1024 * 1024),
    )(proj, proj, proj, gq.reshape(1, -1), gkv.reshape(1, -1), wq, wkv, tc, ts)


def _mla_flash_kernel(q_ref, k_ref, v_ref, o_ref):
    i = pl.program_id(2)
    tq = q_ref.shape[0]
    tk = tq
    q = q_ref[...]

    def step(j, carry, masked):
        m, l, acc = carry
        k = k_ref[pl.ds(pl.multiple_of(j * tk, tk), tk), :]
        v = v_ref[pl.ds(pl.multiple_of(j * tk, tk), tk), :]
        s = _dot_nt(q, k)
        if masked:
            qc = lax.broadcasted_iota(jnp.int32, (tq, tk), 0) // CHUNK
            kc = lax.broadcasted_iota(jnp.int32, (tq, tk), 1) // CHUNK
            s = jnp.where(kc <= qc, s, NEG_BIG)
        m_new = jnp.maximum(m, jnp.max(s, axis=-1, keepdims=True))
        alpha = jnp.exp(m - m_new)
        p = jnp.exp(s - m_new)
        l = alpha * l + jnp.sum(p, axis=-1, keepdims=True)
        acc = alpha * acc + _dot(p.astype(BF16), v)
        return m_new, l, acc

    init = (jnp.full((tq, 1), NEG_BIG, F32), jnp.zeros((tq, 1), F32), jnp.zeros((tq, v_ref.shape[1]), F32))
    carry = lax.fori_loop(0, i // 2, lambda jj, c: step(2 * jj + 1, step(2 * jj, c, False), False), init)
    carry = lax.cond(lax.rem(i, 2) == 1, lambda c: step(i - 1, c, False), lambda c: c, carry)
    m, l, acc = step(i, carry, True)
    o_ref[...] = (acc / l).astype(o_ref.dtype)


def _mla_flash(q, k, v, batch, seq):
    n = q.shape[0]
    tq = ATT_TQ
    nq = seq // tq
    est = 2 * (seq * 2 * LANES * 2 + seq * LANES * 2) + 8 * tq * tq * 4
    return pl.pallas_call(
        _mla_flash_kernel,
        grid=(batch, MLA_HEADS, nq),
        in_specs=[pl.BlockSpec((tq, 2 * LANES), lambda b, h, i: (b * nq + i, h)),
                  pl.BlockSpec((seq, 2 * LANES), lambda b, h, i: (b, h)),
                  pl.BlockSpec((seq, LANES), lambda b, h, i: (b, h))],
        out_specs=pl.BlockSpec((tq, LANES), lambda b, h, i: (b * nq + i, h)),
        out_shape=jax.ShapeDtypeStruct((n, GROUP_WIDTH), BF16),
        compiler_params=_cparams(("parallel", "parallel", "arbitrary"), est),
    )(q, k, v)


def _sb_kernel(q_ref, k_ref, v_ref, o_ref):
    i = pl.program_id(2)
    tq = q_ref.shape[0]
    tk = tq
    sub = SB_SUB
    scale = q_ref.shape[1] ** -0.5
    q = q_ref[...]
    tri = (lax.broadcasted_iota(jnp.int32, (sub, sub), 0)
           > lax.broadcasted_iota(jnp.int32, (sub, sub), 1)).astype(BF16)

    def step(j, carry, masked):
        run, acc = carry
        k = k_ref[pl.ds(pl.multiple_of(j * tk, tk), tk), :]
        v = v_ref[pl.ds(pl.multiple_of(j * tk, tk), tk), :]
        u = _dot_nt(q, k) * (scale * LOG2_E)
        log_beta = jnp.minimum(u, 0.0) - jnp.log2(1.0 + jnp.exp2(-jnp.abs(u)))
        log_keep = log_beta - u
        if masked:
            vis = (lax.broadcasted_iota(jnp.int32, (tq, tk), 1)
                   < lax.broadcasted_iota(jnp.int32, (tq, tk), 0))
            log_keep = jnp.where(vis, log_keep, 0.0)
        keep_b = log_keep.astype(BF16)
        pieces = []
        for blk in range(tk // sub - 1, -1, -1):
            sl = slice(blk * sub, (blk + 1) * sub)
            pieces.append(_dot(keep_b[:, sl], tri) + run)
            run = run + jnp.sum(log_keep[:, sl], axis=-1, keepdims=True)
        later = jnp.concatenate(pieces[::-1], axis=1)
        att = jnp.exp2(log_beta + later)
        if masked:
            att = jnp.where(vis, att, 0.0)
        acc = acc + _dot(att.astype(BF16), v)
        return run, acc

    carry = step(i, (jnp.zeros((tq, 1), F32), jnp.zeros((tq, v_ref.shape[1]), F32)), True)
    carry = lax.fori_loop(0, i // 2, lambda jj, c: step(i - 2 - 2 * jj, step(i - 1 - 2 * jj, c, False), False), carry)
    run, acc = lax.cond(lax.rem(i, 2) == 1, lambda c: step(0, c, False), lambda c: c, carry)
    o_ref[...] = acc.astype(o_ref.dtype)


def _sb_attention(proj, batch, seq):
    n = proj.shape[0]
    tq = ATT_TQ
    nq = seq // tq
    cq, ck, cv = P_C // LANES, (P_C + GROUP_WIDTH) // LANES, (P_C + 2 * GROUP_WIDTH) // LANES
    est = 4 * seq * LANES * 2 + 10 * tq * tq * 4
    return pl.pallas_call(
        _sb_kernel,
        grid=(batch, SB_HEADS, nq),
        in_specs=[pl.BlockSpec((tq, LANES), lambda b, h, i: (b * nq + i, cq + h)),
                  pl.BlockSpec((seq, LANES), lambda b, h, i: (b, ck + h)),
                  pl.BlockSpec((seq, LANES), lambda b, h, i: (b, cv + h))],
        out_specs=pl.BlockSpec((tq, LANES), lambda b, h, i: (b * nq + i, h)),
        out_shape=jax.ShapeDtypeStruct((n, GROUP_WIDTH), BF16),
        compiler_params=_cparams(("parallel", "parallel", "arbitrary"), est),
    )(proj, proj, proj)


def _ca_kernel(q_ref, k_ref, v_ref, tab_ref, o_ref, kpad_ref, vpad_ref):
    i = pl.program_id(1)
    tq = q_ref.shape[0]
    past = CA_PAST_CHUNKS * CHUNK
    win = tq + past
    scale = LANES ** -0.5

    @pl.when(i == 0)
    def _():
        kpad_ref[0:past, :] = jnp.zeros((past, kpad_ref.shape[1]), kpad_ref.dtype)
        vpad_ref[0:past, :] = jnp.zeros((past, vpad_ref.shape[1]), vpad_ref.dtype)
        kpad_ref[past:, :] = k_ref[...]
        vpad_ref[past:, :] = v_ref[...]

    start = pl.multiple_of(i * tq, tq)
    for h in range(CA_HEADS):
        cols = slice(h * LANES, (h + 1) * LANES)
        kw = kpad_ref[pl.ds(start, win), cols]
        vw = vpad_ref[pl.ds(start, win), cols]
        s = _dot_nt(q_ref[:, cols], kw) * scale + tab_ref[0, h]
        m = jnp.max(s, axis=-1, keepdims=True)
        p = jnp.exp(s - m)
        l = jnp.sum(p, axis=-1, keepdims=True)
        o_ref[:, cols] = (_dot(p.astype(BF16), vw) / l).astype(o_ref.dtype)


def _ca_table(rel_bias, tq):
    past = CA_PAST_CHUNKS * CHUNK
    win = tq + past
    r = jnp.arange(tq)[:, None]
    c = jnp.arange(win)[None, :]
    ring = 1024
    assert ring >= tq + win - 1
    m = jnp.arange(ring)
    m = jnp.where(m >= win, m - ring, m)
    idx = jnp.clip(past - m, -(CHUNK - 1), CA_REL_PAST) + (CHUNK - 1)
    ringvals = rel_bias[:, idx].astype(F32)
    nh = rel_bias.shape[0]
    bias = jnp.tile(ringvals, (1, tq))[:, :tq * (ring - 1)].reshape(nh, tq, ring - 1)[:, :, :win]
    qc = r // CHUNK
    kc = c // CHUNK
    vis = (kc >= qc) & (kc <= qc + CA_PAST_CHUNKS)
    tabs = []
    n_var = past // tq + 1
    for v in range(n_var):
        first = (past - v * tq) // CHUNK if v < n_var - 1 else 0
        ok = vis & (kc >= first)
        tabs.append(jnp.where(ok[None], bias, NEG_BIG))
    return jnp.stack(tabs)


def _ca_attention(proj, table, batch, seq):
    n = proj.shape[0]
    tq = CA_TQ
    nq = seq // tq
    past = CA_PAST_CHUNKS * CHUNK
    win = tq + past
    n_var = table.shape[0]
    gw = GROUP_WIDTH
    cq, ck, cv = P_D // gw, (P_D + gw) // gw, (P_D + 2 * gw) // gw
    est = (4 * seq * gw * 2 + 2 * (seq + past) * gw * 2 + 2 * CA_HEADS * tq * win * 4 + 8 * tq * win * 4
           + 4 * tq * gw * 2)
    return pl.pallas_call(
        _ca_kernel,
        grid=(batch, nq),
        in_specs=[pl.BlockSpec((tq, gw), lambda b, i: (b * nq + i, cq)),
                  pl.BlockSpec((seq, gw), lambda b, i: (b, ck)),
                  pl.BlockSpec((seq, gw), lambda b, i: (b, cv)),
                  pl.BlockSpec((1, CA_HEADS, tq, win), lambda b, i: (jnp.minimum(i, n_var - 1), 0, 0, 0))],
        out_specs=pl.BlockSpec((tq, gw), lambda b, i: (b * nq + i, 0)),
        out_shape=jax.ShapeDtypeStruct((n, gw), BF16),
        scratch_shapes=[pltpu.VMEM((seq + past, gw), BF16), pltpu.VMEM((seq + past, gw), BF16)],
        compiler_params=_cparams(("parallel", "arbitrary"), est),
    )(proj, proj, proj, table)


def _ssd_kernel(z_ref, xbc_ref, dt_ref, cw_ref, cb_ref, dtb_ref, alog_ref, dskip_ref, ng_ref, ex_ref,
                o_ref, xbuf_ref, state_ref, y_ref):
    c = pl.program_id(1)
    lc = z_ref.shape[0]
    halo = SUBLANES
    gw = GROUP_WIDTH // SSM_GROUPS
    hpg = SSM_HEADS // SSM_GROUPS

    @pl.when(c == 0)
    def _():
        xbuf_ref[0:halo, :] = jnp.zeros((halo, xbuf_ref.shape[1]), F32)
        state_ref[...] = jnp.zeros(state_ref.shape, F32)

    xbuf_ref[halo:halo + lc, :] = xbc_ref[...].astype(F32)
    acc = jnp.broadcast_to(cb_ref[...], (lc, SSM_CONV_DIM))
    for k in range(SSM_CONV):
        acc = acc + cw_ref[k:k + 1, :] * xbuf_ref[pl.ds(halo - (SSM_CONV - 1) + k, lc), :]
    xbuf_ref[0:halo, :] = xbuf_ref[lc:lc + halo, :]
    u = acc * jax.nn.sigmoid(acc)
    xs = u[:, :GROUP_WIDTH]
    bm = u[:, GROUP_WIDTH:GROUP_WIDTH + SSM_GROUPS * SSM_STATE].astype(BF16)
    cm = u[:, GROUP_WIDTH + SSM_GROUPS * SSM_STATE:].astype(BF16)

    dt = jax.nn.softplus(dt_ref[...].astype(F32) + dtb_ref[...])
    a = -jnp.exp(alog_ref[...])
    da = dt * a
    row = lax.broadcasted_iota(jnp.int32, (lc, lc), 0)
    col = lax.broadcasted_iota(jnp.int32, (lc, lc), 1)
    lower = col <= row
    cs = _dot3_left01(lower.astype(BF16), da)
    cs_last = cs[lc - 1:lc, :]
    cs_t = cs.T
    dt_t = dt.T
    ex = ex_ref[...]
    ecs_x = _dot3_right01(jnp.exp(cs), ex)
    wdec_x = _dot3_right01(jnp.exp(cs_last - cs) * dt, ex)
    xs_b = xs.astype(BF16)

    for g in range(SSM_GROUPS):
        bg = bm[:, g * SSM_STATE:(g + 1) * SSM_STATE]
        cg = cm[:, g * SSM_STATE:(g + 1) * SSM_STATE]
        cb = _dot_nt(cg, bg)
        for r in range(hpg):
            h = g * hpg + r
            diff = cs[:, h:h + 1] - cs_t[h:h + 1, :]
            wgt = cb * jnp.exp(jnp.where(lower, diff, NEG_BIG)) * dt_t[h:h + 1, :]
            y_ref[:, h * SSM_HEAD_DIM:(h + 1) * SSM_HEAD_DIM] = _dot(
                wgt.astype(BF16), xs_b[:, h * SSM_HEAD_DIM:(h + 1) * SSM_HEAD_DIM])
        lo, hi = g * gw, (g + 1) * gw
        st = state_ref[g]
        y_off = _dot(cg, st.astype(BF16)) * ecs_x[:, lo:hi]
        y_ref[:, lo:hi] += y_off
        xw = (xs[:, lo:hi] * wdec_x[:, lo:hi]).astype(BF16)
        state_ref[g] = st * ecs_x[lc - 1:lc, lo:hi] + _dot_tn(bg, xw)

    zf = z_ref[...].astype(F32)
    y = (y_ref[...] + xs * dskip_ref[...]) * (zf * jax.nn.sigmoid(zf))
    for g in range(SSM_GROUPS):
        lo, hi = g * gw, (g + 1) * gw
        yg = y[:, lo:hi]
        yg = yg * lax.rsqrt(jnp.mean(yg * yg, axis=-1, keepdims=True) + EPS)
        o_ref[:, lo:hi] = (yg * ng_ref[:, lo:hi]).astype(o_ref.dtype)


def _ssd(proj, conv_w, conv_b, dt_bias, a_log, d_skip, norm_g, batch, seq):
    n = proj.shape[0]
    lc = SSD_LC
    nc = seq // lc
    dtb = jnp.zeros((1, LANES), F32).at[0, :SSM_HEADS].set(dt_bias)
    alog = jnp.full((1, LANES), NEG_BIG, F32).at[0, :SSM_HEADS].set(a_log)
    dskip = jnp.repeat(d_skip, SSM_HEAD_DIM).reshape(1, GROUP_WIDTH)
    ex = (jnp.arange(LANES)[:, None] == jnp.arange(GROUP_WIDTH)[None, :] // SSM_HEAD_DIM).astype(BF16)
    full = lambda shape: pl.BlockSpec(shape, lambda b, c: tuple(0 for _ in shape))
    return pl.pallas_call(
        _ssd_kernel,
        grid=(batch, nc),
        in_specs=[pl.BlockSpec((lc, GROUP_WIDTH), lambda b, c: (b * nc + c, P_Z // GROUP_WIDTH)),
                  pl.BlockSpec((lc, SSM_CONV_DIM), lambda b, c: (b * nc + c, P_XBC // SSM_CONV_DIM)),
                  pl.BlockSpec((lc, LANES), lambda b, c: (b * nc + c, P_DT // LANES)),
                  full((SSM_CONV, SSM_CONV_DIM)), full((1, SSM_CONV_DIM)), full((1, LANES)), full((1, LANES)),
                  full((1, GROUP_WIDTH)), full((1, GROUP_WIDTH)), full((LANES, GROUP_WIDTH))],
        out_specs=pl.BlockSpec((lc, GROUP_WIDTH), lambda b, c: (b * nc + c, 0)),
        out_shape=jax.ShapeDtypeStruct((n, GROUP_WIDTH), BF16),
        scratch_shapes=[pltpu.VMEM((lc + SUBLANES, SSM_CONV_DIM), F32),
                        pltpu.VMEM((SSM_GROUPS, SSM_STATE, GROUP_WIDTH // SSM_GROUPS), F32),
                        pltpu.VMEM((lc, GROUP_WIDTH), F32)],
        compiler_params=_cparams(("parallel", "arbitrary"), 16 * 1024 * 1024),
    )(proj, proj, proj, conv_w, conv_b.reshape(1, -1), dtb, alog, dskip, norm_g.reshape(1, -1), ex)


def _outproj_kernel(a_ref, b_ref, c_ref, d_ref, x_ref, gmix_ref, w_ref, mod_ref, gf_ref, rw_ref, rb_ref,
                    xo_ref, h_ref, e_ref, gt_ref):
    gw = GROUP_WIDTH
    a = _rms(a_ref[...].astype(F32), gmix_ref[0:1, :]).astype(BF16)
    c = _rms(c_ref[...].astype(F32), gmix_ref[1:2, :]).astype(BF16)
    d = _rms(d_ref[...].astype(F32), gmix_ref[2:3, :]).astype(BF16)
    mixed = (_dot(a, w_ref[0:gw, :]) + _dot(b_ref[...], w_ref[gw:2 * gw, :])
             + _dot(c, w_ref[2 * gw:3 * gw, :]) + _dot(d, w_ref[3 * gw:, :]))
    x = x_ref[...] + mod_ref[0, 2:3, :] * mixed
    xo_ref[...] = x
    h = _rms(x, gf_ref[...]) * (1.0 + mod_ref[0, 4:5, :]) + mod_ref[0, 3:4, :]
    _store_packed_rows(h_ref, h)

    hh = h.astype(BF16)
    hl = (h - hh.astype(F32)).astype(BF16)
    rw = rw_ref[...]
    wh = rw.astype(BF16)
    wl = (rw - wh.astype(F32)).astype(BF16)
    logits = _dot(hh, wh) + _dot(hh, wl) + _dot(hl, wh) + rb_ref[...]
    lane = lax.broadcasted_iota(jnp.int32, logits.shape, 1)
    vals, idxs = [], []
    for _ in range(TOP_K):
        m = jnp.max(logits, axis=-1, keepdims=True)
        idx = jnp.min(jnp.where(logits == m, lane, LANES), axis=-1, keepdims=True)
        vals.append(m)
        idxs.append(idx)
        logits = jnp.where(lane == idx, -jnp.inf, logits)
    ex = [jnp.exp(v - vals[0]) for v in vals]
    tot = ex[0] + ex[1] + ex[2] + ex[3]
    e_out = jnp.zeros(logits.shape, jnp.int32)
    g_out = jnp.zeros(logits.shape, F32)
    for k in range(TOP_K):
        e_out = jnp.where(lane == k, idxs[k], e_out)
        g_out = jnp.where(lane == k, ex[k] / tot, g_out)
    e_ref[...] = e_out
    gt_ref[...] = g_out


def _outproj(oa, ob, oc, od, x2, gmix, w_out, mod, gf, rw, rb, seq):
    n, d = x2.shape
    tm = 512
    per_b = seq // tm
    gw = GROUP_WIDTH
    row = lambda width: pl.BlockSpec((tm, width), lambda i: (i, 0))
    full = lambda shape: pl.BlockSpec(shape, lambda i: tuple(0 for _ in shape))
    est = 2 * d * d * 2 + 4 * tm * d * 4 + 2 * tm * d * 2 + 8 * tm * gw * 2 + 6 * tm * d * 4
    return pl.pallas_call(
        _outproj_kernel,
        grid=(n // tm,),
        in_specs=[row(gw), row(gw), row(gw), row(gw), row(d), full((3, gw)), full((d, d)),
                  pl.BlockSpec((1, 6, d), lambda i: (i // per_b, 0, 0)), full((1, d)),
                  full((d, LANES)), full((1, LANES))],
        out_specs=[row(d), pl.BlockSpec((tm * ROW_SUBLANES, LANES), lambda i: (i, 0)), row(LANES), row(LANES)],
        out_shape=[jax.ShapeDtypeStruct((n, d), F32), jax.ShapeDtypeStruct((n * ROW_SUBLANES, LANES), U32),
                   jax.ShapeDtypeStruct((n, LANES), jnp.int32), jax.ShapeDtypeStruct((n, LANES), F32)],
        compiler_params=_cparams(("parallel",), est),
    )(oa, ob, oc, od, x2, gmix, w_out, mod, gf.reshape(1, d), rw, rb)


def _row_copy(src_ref, src_row, dst_ref, dst_row, sem):
    rs = ROW_SUBLANES
    return pltpu.make_async_copy(src_ref.at[pl.ds(pl.multiple_of(src_row * rs, rs), rs)],
                                 dst_ref.at[pl.ds(pl.multiple_of(dst_row * rs, rs), rs)], sem)


def _expert_kernel(be_ref, nu_ref, tok_first_ref, step_ref, dst_last_ref, hp_ref, w1_ref, b1_ref, w2_ref,
                   b2_ref, y_ref, xa_ref, xb_ref, ya_ref, yb_ref, gsem, ssem):
    i = pl.program_id(0)
    nu = nu_ref[0]
    tm = MOE_BLOCK
    rs = ROW_SUBLANES

    def start_rows(copy_of_row):
        for r in range(tm):
            copy_of_row(r).start(priority=1)

    def wait_rows(src_ref, dst_ref, sem):
        for _ in range(tm):
            _row_copy(src_ref, 0, dst_ref, 0, sem).wait()

    @pl.when(i == 0)
    def _():
        start_rows(lambda r: _row_copy(hp_ref, tok_first_ref[0, 0, r], xa_ref, r, gsem.at[0]))
        yb_ref[...] = jnp.zeros(yb_ref.shape, U32)
        spare = pltpu.make_async_copy(yb_ref, y_ref.at[pl.ds(y_ref.shape[0] - 2 * tm * rs, tm * rs)], ssem.at[0])
        spare.start()
        spare.wait()

    def body(xcur, xnext, ycur, yprev, gcur, gnext, scur, sprev):
        wait_rows(hp_ref, xcur, gcur)
        start_rows(lambda r: _row_copy(hp_ref, step_ref[0, 0, r], xnext, r, gnext))
        start_rows(lambda r: _row_copy(yprev, r, y_ref, step_ref[0, 0, tm + r], sprev))
        words = _load_packed_rows(xcur)
        x = jnp.concatenate([_unpack_lo(u).astype(BF16) for u in words]
                            + [_unpack_hi(u).astype(BF16) for u in words], axis=1)
        cw = EXPERT_COL_CHUNK
        hu = jnp.concatenate([_dot(x, w1_ref[0, :, c:c + cw].astype(BF16)) for c in range(0, w1_ref.shape[2], cw)],
                             axis=1) + b1_ref[0]
        gate = jnp.minimum(hu[:, :D_EXPERT], SWIGLU_LIMIT)
        up = jnp.clip(hu[:, D_EXPERT:], -SWIGLU_LIMIT, SWIGLU_LIMIT)
        act = (gate * jax.nn.sigmoid(SWIGLU_ALPHA * gate) * (up + 1.0)).astype(BF16)
        y = jnp.concatenate([_dot(act, w2_ref[0, :, c:c + cw].astype(BF16)) for c in range(0, w2_ref.shape[2], cw)],
                            axis=1) + b2_ref[0]

        @pl.when(i > 0)
        def _():
            wait_rows(ycur, y_ref, scur)

        _store_packed_rows(ycur, y)

        @pl.when(i == nu - 1)
        def _():
            start_rows(lambda r: _row_copy(ycur, r, y_ref, dst_last_ref[0, 0, r], scur))
            wait_rows(yprev, y_ref, sprev)
            wait_rows(ycur, y_ref, scur)
            wait_rows(hp_ref, xnext, gnext)

    even = lax.rem(i, 2) == 0

    @pl.when(jnp.logical_and(i < nu, even))
    def _():
        body(xa_ref, xb_ref, ya_ref, yb_ref, gsem.at[0], gsem.at[1], ssem.at[0], ssem.at[1])

    @pl.when(jnp.logical_and(i < nu, jnp.logical_not(even)))
    def _():
        body(xb_ref, xa_ref, yb_ref, ya_ref, gsem.at[1], gsem.at[0], ssem.at[1], ssem.at[0])


def _experts(hp, row_tok, row_dst, block_e, n_used, n_slots, w1, b1, w2, b2, layer=0):
    tm = MOE_BLOCK
    nblk = row_tok.shape[0]
    d, de2 = w1.shape[1], w1.shape[2]
    de = w2.shape[1]
    rs = ROW_SUBLANES
    block_e = block_e + layer * N_EXPERTS
    last = lambda i, nu: jnp.minimum(i, nu[0] - 1)
    est = (2 * (d * de2 + de * d) * w1.dtype.itemsize + 4 * tm * rs * LANES * 4 + 4 * tm * d * 4
           + 3 * tm * de2 * 4)
    first_prev = (n_slots - tm + jnp.arange(tm, dtype=jnp.int32)).reshape(1, 1, tm)
    step_rows = jnp.concatenate([jnp.concatenate([row_tok[1:], row_tok[-1:]], axis=0),
                                 jnp.concatenate([first_prev, row_dst[:-1]], axis=0)], axis=2)
    dst_last = lax.dynamic_slice_in_dim(row_dst, n_used[0] - 1, 1, axis=0)
    smem = lambda width, f: pl.BlockSpec((1, 1, width), f, memory_space=pltpu.SMEM)
    grid_spec = pltpu.PrefetchScalarGridSpec(
        num_scalar_prefetch=2,
        grid=(nblk,),
        in_specs=[smem(tm, lambda i, be, nu: (0, 0, 0)),
                  smem(2 * tm, lambda i, be, nu: (last(i, nu), 0, 0)),
                  smem(tm, lambda i, be, nu: (0, 0, 0)),
                  pl.BlockSpec(memory_space=pl.ANY),
                  pl.BlockSpec((1, d, de2), lambda i, be, nu: (be[last(i, nu)], 0, 0)),
                  pl.BlockSpec((1, 1, de2), lambda i, be, nu: (be[last(i, nu)], 0, 0)),
                  pl.BlockSpec((1, de, d), lambda i, be, nu: (be[last(i, nu)], 0, 0)),
                  pl.BlockSpec((1, 1, d), lambda i, be, nu: (be[last(i, nu)], 0, 0))],
        out_specs=pl.BlockSpec(memory_space=pl.ANY),
        scratch_shapes=[pltpu.VMEM((tm * rs, LANES), U32) for _ in range(4)]
        + [pltpu.SemaphoreType.DMA((2,)), pltpu.SemaphoreType.DMA((2,))],
    )
    return pl.pallas_call(
        _expert_kernel,
        grid_spec=grid_spec,
        out_shape=jax.ShapeDtypeStruct((n_slots * rs, LANES), U32),
        compiler_params=_cparams(("arbitrary",), est),
    )(block_e, n_used, row_tok, step_rows, dst_last, hp, w1, b1.reshape(b1.shape[0], 1, de2), w2,
      b2.reshape(b2.shape[0], 1, d))


def _combine_kernel(y0_ref, y1_ref, y2_ref, y3_ref, g_ref, x_ref, mod_ref, o_ref):
    tm, d = x_ref.shape
    half = d // 2
    rs = ROW_SUBLANES
    rc = COMBINE_ROWS

    def chunk(c, carry):
        r0 = pl.multiple_of(c * rc, rc)
        rows = pl.ds(r0, rc)
        g = g_ref[rows, :]
        lo = [jnp.zeros((rc, LANES), F32) for _ in range(rs)]
        hi = [jnp.zeros((rc, LANES), F32) for _ in range(rs)]
        for k, y_ref in enumerate((y0_ref, y1_ref, y2_ref, y3_ref)):
            blk = y_ref[pl.ds(pl.multiple_of(r0 * rs, rc * rs), rc * rs), :]
            w = jnp.swapaxes(blk.reshape(rc // rs, rs, rs, LANES), 1, 2)
            gk = g[:, k:k + 1]
            for s in range(rs):
                u = w[:, s].reshape(rc, LANES)
                lo[s] = lo[s] + _unpack_lo(u) * gk
                hi[s] = hi[s] + _unpack_hi(u) * gk
        for s in range(rs):
            a, b = LANES * s, half + LANES * s
            o_ref[rows, a:a + LANES] = x_ref[rows, a:a + LANES] + mod_ref[0, 5:6, a:a + LANES] * lo[s]
            o_ref[rows, b:b + LANES] = x_ref[rows, b:b + LANES] + mod_ref[0, 5:6, b:b + LANES] * hi[s]
        return carry
    lax.fori_loop(0, tm // rc, chunk, 0)


def _combine(y4, gates, x2, mod, seq):
    n, d = x2.shape
    tm = 512
    per_b = seq // tm
    nt = n // tm
    rows = tm * ROW_SUBLANES
    est = 2 * TOP_K * rows * LANES * 4 + 4 * tm * d * 4 + 2 * tm * LANES * 4
    yspec = lambda k: pl.BlockSpec((rows, LANES), lambda i: (k * nt + i, 0))
    return pl.pallas_call(
        _combine_kernel,
        grid=(nt,),
        in_specs=[yspec(0), yspec(1), yspec(2), yspec(3),
                  pl.BlockSpec((tm, LANES), lambda i: (i, 0)),
                  pl.BlockSpec((tm, d), lambda i: (i, 0)),
                  pl.BlockSpec((1, 6, d), lambda i: (i // per_b, 0, 0))],
        out_specs=pl.BlockSpec((tm, d), lambda i: (i, 0)),
        out_shape=jax.ShapeDtypeStruct((n, d), F32),
        compiler_params=_cparams(("parallel",), est),
    )(y4, y4, y4, y4, gates, x2, mod)


def _final_norm_kernel(x_ref, g_ref, o_ref):
    o_ref[...] = _rms(x_ref[...], g_ref[...])


def _final_norm(x2, g):
    n, d = x2.shape
    tm = 512
    return pl.pallas_call(
        _final_norm_kernel,
        grid=(n // tm,),
        in_specs=[pl.BlockSpec((tm, d), lambda i: (i, 0)), pl.BlockSpec((1, d), lambda i: (0, 0))],
        out_specs=pl.BlockSpec((tm, d), lambda i: (i, 0)),
        out_shape=jax.ShapeDtypeStruct((n, d), F32),
        compiler_params=_cparams(("parallel",), 4 * tm * d * 4),
    )(x2, g.reshape(1, d))


def _swap_halves(w):
    half = w.shape[-1] // 2
    return jnp.concatenate([w[..., half:], w[..., :half]], axis=-1)


def _pack_w_in_kernel(w_ref, o_ref):
    w = w_ref[0]
    b0, c0 = A_COLS, A_COLS + B_COLS
    kr0 = MLA_Q_RANK + MLA_KV_RANK
    half = MLA_ROPE // 2
    pieces = [w[:, :MLA_Q_RANK],
              w[:, b0:b0 + GROUP_WIDTH + SSM_CONV_DIM],
              w[:, MLA_Q_RANK:kr0 + MLA_ROPE],
              w[:, kr0 + half:kr0 + MLA_ROPE], w[:, kr0:kr0 + half],
              w[:, c0 - SSM_HEADS:c0], jnp.zeros((w.shape[0], LANES - SSM_HEADS), w.dtype),
              w[:, c0:]]
    o_ref[0] = jnp.concatenate(pieces, axis=1).astype(BF16)


def _pack_w_in(w_in):
    nl, d, nc = w_in.shape
    tr = 256
    est = 2 * tr * nc * 4 + 2 * tr * P_COLS * 2 + 2 * tr * P_COLS * 4
    return pl.pallas_call(
        _pack_w_in_kernel,
        grid=(nl, d // tr),
        in_specs=[pl.BlockSpec((1, tr, nc), lambda l, i: (l, i, 0))],
        out_specs=pl.BlockSpec((1, tr, P_COLS), lambda l, i: (l, i, 0)),
        out_shape=jax.ShapeDtypeStruct((nl, d, P_COLS), BF16),
        compiler_params=_cparams(("parallel", "parallel"), est),
    )(w_in)


def _pack_wq(wq):
    r = wq.shape[0]
    w = wq.reshape(r, MLA_HEADS, MLA_NOPE + MLA_ROPE)
    rope = w[..., MLA_NOPE:]
    return jnp.concatenate([w[..., :MLA_NOPE], rope, _swap_halves(rope)], axis=-1).reshape(r, -1).astype(BF16)


def _pack_wkv(wkv):
    r = wkv.shape[0]
    w = wkv.reshape(r, MLA_HEADS, MLA_NOPE + MLA_V)
    return jnp.concatenate([w[..., :MLA_NOPE].reshape(r, -1), w[..., MLA_NOPE:].reshape(r, -1)], axis=-1).astype(BF16)


def _rope_tables(seq):
    inv = 1.0 / (ROPE_THETA ** (jnp.arange(0, MLA_ROPE, 2, dtype=F32) / MLA_ROPE))
    ang = jnp.arange(seq, dtype=F32)[:, None] * inv[None, :]
    cos, sin = jnp.cos(ang), jnp.sin(ang)
    zero = jnp.zeros((seq, LANES - MLA_ROPE), F32)
    return jnp.concatenate([cos, cos, zero], axis=1), jnp.concatenate([-sin, sin, zero], axis=1)


def _route(top_e, n_tok):
    n_assign = n_tok * TOP_K
    n_blocks = -(-(n_assign + N_EXPERTS * (MOE_BLOCK - 1)) // MOE_BLOCK)
    n_rows = n_blocks * MOE_BLOCK
    dest, meta = _route_dest(top_e)
    padded_end = meta[2, :N_EXPERTS]
    block_start = jnp.arange(n_blocks, dtype=jnp.int32) * MOE_BLOCK
    block_e = jnp.minimum(jnp.sum((padded_end[None, :] <= block_start[:, None]).astype(jnp.int32), axis=1),
                          N_EXPERTS - 1)
    n_used = (padded_end[N_EXPERTS - 1] // MOE_BLOCK).reshape(1)
    row_assign = _route_invert(meta, dest[:, :TOP_K].reshape(n_assign), n_rows)
    real = row_assign >= 0
    row = jnp.arange(n_rows, dtype=jnp.int32)
    row_tok = jnp.where(real, row_assign // TOP_K, 0)
    spare = n_assign + (row // MOE_BLOCK % 2) * MOE_BLOCK + row % MOE_BLOCK
    row_dst = jnp.where(real, row_assign % TOP_K * n_tok + row_assign // TOP_K, spare)
    shape = (n_blocks, 1, MOE_BLOCK)
    return row_tok.reshape(shape), row_dst.reshape(shape), block_e, n_used, n_assign + 2 * MOE_BLOCK


NORM_ROWS = 16
ROUTE_TM = 512
COMBINE_ROWS = 16


def _route_dest_kernel(e_ref, dest_ref, meta_ref, carry_ref, base_ref):
    p = pl.program_id(0)
    i = pl.program_id(1)
    tm = e_ref.shape[0]
    e = e_ref[...]
    lane = lax.broadcasted_iota(jnp.int32, (tm, LANES), 1)
    hit = [lane == e[:, k:k + 1] for k in range(TOP_K)]
    tot = sum(h.astype(F32) for h in hit)

    @pl.when(jnp.logical_and(p == 0, i == 0))
    def _():
        carry_ref[...] = jnp.zeros(carry_ref.shape, F32)
        meta_ref[...] = jnp.zeros(meta_ref.shape, jnp.int32)

    @pl.when(jnp.logical_and(p == 0, i == pl.num_programs(1) - 1))
    def _():
        counts = carry_ref[...] + jnp.sum(tot, axis=0, keepdims=True)
        padded = jnp.floor((counts + (MOE_BLOCK - 1)) * (1.0 / MOE_BLOCK)) * MOE_BLOCK
        upper = (lax.broadcasted_iota(jnp.int32, (LANES, LANES), 0)
                 <= lax.broadcasted_iota(jnp.int32, (LANES, LANES), 1)).astype(BF16)
        padded_end = _dot3_right01(jnp.broadcast_to(padded, (SUBLANES, LANES)), upper)[0:1]
        base_ref[...] = padded_end - padded
        meta_ref[0:1, :] = counts.astype(jnp.int32)
        meta_ref[1:2, :] = (padded_end - padded).astype(jnp.int32)
        meta_ref[2:3, :] = padded_end.astype(jnp.int32)
        carry_ref[...] = -jnp.sum(tot, axis=0, keepdims=True)

    @pl.when(p == 1)
    def _():
        earlier = (lax.broadcasted_iota(jnp.int32, (tm, tm), 1)
                   < lax.broadcasted_iota(jnp.int32, (tm, tm), 0)).astype(BF16)
        pos = _dot(earlier, tot.astype(BF16)) + carry_ref[...] + base_ref[...]
        dest = jnp.zeros((tm, LANES), jnp.int32)
        for k in range(TOP_K):
            dk = jnp.sum(jnp.where(hit[k], pos, 0.0), axis=-1, keepdims=True)
            dest = jnp.where(lane == k, dk.astype(jnp.int32), dest)
        dest_ref[...] = dest

    carry_ref[...] += jnp.sum(tot, axis=0, keepdims=True)


def _route_dest(top_e):
    n = top_e.shape[0]
    tm = ROUTE_TM
    return pl.pallas_call(
        _route_dest_kernel,
        grid=(2, n // tm),
        in_specs=[pl.BlockSpec((tm, LANES), lambda p, i: (i, 0))],
        out_specs=[pl.BlockSpec((tm, LANES), lambda p, i: (i * p, 0)),
                   pl.BlockSpec((SUBLANES, LANES), lambda p, i: (0, 0))],
        out_shape=[jax.ShapeDtypeStruct((n, LANES), jnp.int32), jax.ShapeDtypeStruct((SUBLANES, LANES), jnp.int32)],
        scratch_shapes=[pltpu.VMEM((1, LANES), F32), pltpu.VMEM((1, LANES), F32)],
        compiler_params=_cparams(("arbitrary", "arbitrary"), 8 * tm * tm * 4),
    )(top_e)


ROUTE_CHUNK = 2048


def _route_invert_kernel(meta_ref, dest_ref, out_ref):
    s = pl.program_id(0)
    ch = dest_ref.shape[2]
    per = SUBLANES

    def clear(r, carry):
        out_ref[r] = -1
        return carry

    @pl.when(s == 0)
    def _():
        def expert_padding(e, carry):
            return lax.fori_loop(meta_ref[1, e] + meta_ref[0, e], meta_ref[2, e], clear, carry)
        lax.fori_loop(0, N_EXPERTS, expert_padding, 0)
        lax.fori_loop(meta_ref[2, N_EXPERTS - 1], out_ref.shape[0], clear, 0)

    def place(j, carry):
        for u in range(per):
            a = j * per + u
            out_ref[dest_ref[0, 0, a]] = s * ch + a
        return carry
    lax.fori_loop(0, ch // per, place, 0)


def _route_invert(meta, dest_flat, n_rows):
    ch = ROUTE_CHUNK
    n_place = dest_flat.shape[0] // ch
    assert n_place * ch == dest_flat.shape[0]
    return pl.pallas_call(
        _route_invert_kernel,
        grid=(n_place,),
        in_specs=[pl.BlockSpec(memory_space=pltpu.SMEM),
                  pl.BlockSpec((1, 1, ch), lambda s: (s, 0, 0), memory_space=pltpu.SMEM)],
        out_specs=pl.BlockSpec(memory_space=pltpu.SMEM),
        out_shape=jax.ShapeDtypeStruct((n_rows,), jnp.int32),
        compiler_params=pltpu.CompilerParams(dimension_semantics=("arbitrary",)),
    )(meta, dest_flat.reshape(n_place, 1, ch))


def kernel(x, c, attn_norm, ffn_norm, mod_w, mod_b, w_in, mla_q_norm, mla_w_q_up, mla_kv_norm, mla_w_kv_up,
           ssm_conv_w, ssm_conv_b, ssm_dt_bias, ssm_a_log, ssm_d, ssm_norm, ca_rel_bias, mix_out_norm, w_out,
           router_w, router_b, moe_w1, moe_b1, moe_w2, moe_b2, final_norm):
    batch, seq, d = x.shape
    n = batch * seq
    depth = w_in.shape[0]
    x2 = x.reshape(n, d)
    mod_all = _mod_all(c, mod_w, mod_b).reshape(depth, batch, 6, d)
    tc, ts = _rope_tables(seq)
    w_in_packed = _pack_w_in(w_in)
    for l in range(depth):
        mod = mod_all[l]
        proj = _inproj(x2, attn_norm[l], mod, w_in_packed, l, seq)
        q, k, v = _mla_prep(proj, mla_q_norm[l], mla_kv_norm[l], _pack_wq(mla_w_q_up[l]),
                            _pack_wkv(mla_w_kv_up[l]), tc, ts, seq)
        out_a = _mla_flash(q, k, v, batch, seq)
        out_b = _ssd(proj, ssm_conv_w[l], ssm_conv_b[l], ssm_dt_bias[l], ssm_a_log[l], ssm_d[l], ssm_norm[l],
                     batch, seq)
        out_c = _sb_attention(proj, batch, seq)
        out_d = _ca_attention(proj, _ca_table(ca_rel_bias[l], CA_TQ), batch, seq)
        rw = jnp.zeros((d, LANES), F32).at[:, :N_EXPERTS].set(router_w[l])
        rb = jnp.full((1, LANES), NEG_BIG, F32).at[0, :N_EXPERTS].set(router_b[l])
        x2, hp, top_e, gates = _outproj(out_a, out_b, out_c, out_d, x2, mix_out_norm[l], w_out[l].astype(BF16),
                                        mod, ffn_norm[l], rw, rb, seq)
        row_tok, row_dst, block_e, n_used, n_slots = _route(top_e, n)
        y4 = _experts(hp, row_tok, row_dst, block_e, n_used, n_slots,
                      moe_w1.reshape(-1, *moe_w1.shape[2:]), moe_b1.reshape(-1, moe_b1.shape[2]),
                      moe_w2.reshape(-1, *moe_w2.shape[2:]), moe_b2.reshape(-1, moe_b2.shape[2]), layer=l)
        x2 = _combine(y4, gates, x2, mod, seq)
    return _final_norm(x2, final_norm).reshape(batch, seq, d)
```

```python
import functools
import math

import jax
import jax.numpy as jnp
from jax import lax
from jax.experimental import pallas as pl
from jax.experimental.pallas import tpu as pltpu

F32 = jnp.float32
BF16 = jnp.bfloat16
U32 = jnp.uint32

D_MODEL = 2048
DEPTH = 4
CHUNK = 64
EPS = 1e-6
GROUP_WIDTH = 512
MLA_HEADS = 4
MLA_NOPE = 128
MLA_ROPE = 64
MLA_V = 128
MLA_Q_RANK = 512
MLA_KV_RANK = 256
ROPE_THETA = 10000.0
SSM_HEAD_DIM = 64
SSM_HEADS = 8
SSM_GROUPS = 2
SSM_STATE = 128
SSM_CONV = 4
SSM_CONV_DIM = 1024
SB_HEADS = 4
CA_HEADS = 4
CA_PAST_CHUNKS = 8
CA_REL_PAST = 256
N_EXPERTS = 32
TOP_K = 4
D_EXPERT = 768
SWIGLU_ALPHA = 1.702
SWIGLU_LIMIT = 7.0

A_COLS = MLA_Q_RANK + MLA_KV_RANK + MLA_ROPE
B_COLS = GROUP_WIDTH + SSM_CONV_DIM + SSM_HEADS
C_COLS = 3 * GROUP_WIDTH

LANES = 128
SUBLANES = 8
VMEM_BYTES_V7X = 64 * 1024 * 1024
NEG_BIG = -1e30
LOG2_E = math.log2(math.e)

P_QLAT = 0
P_Z = 512
P_XBC = 1024
P_KVLAT = 2048
P_KROPE = 2304
P_DT = 2432
P_C = 2560
P_D = 4096
P_COLS = 5632

MOE_BLOCK = 256
EXPERT_COL_CHUNK = 256
ATT_TQ = 512
SB_SUB = 256
ROW_SUBLANES = 8
CA_TQ = 256
CA_WIN = CA_TQ + CA_PAST_CHUNKS * CHUNK
SSD_LC = 256


def _vmem_limit(nbytes):
    return int(min(max(2 * nbytes, 16 * 1024 * 1024), VMEM_BYTES_V7X - 8 * 1024 * 1024))


def _cparams(sem, nbytes):
    return pltpu.CompilerParams(dimension_semantics=sem, vmem_limit_bytes=_vmem_limit(nbytes))


def _rms(x, g):
    return x * lax.rsqrt(jnp.mean(x * x, axis=-1, keepdims=True) + EPS) * g


def _split3(x):
    hi = x.astype(BF16)
    r1 = x - hi.astype(F32)
    mid = r1.astype(BF16)
    lo = (r1 - mid.astype(F32)).astype(BF16)
    return hi, mid, lo


def _dot(a, b):
    return jnp.dot(a, b, preferred_element_type=F32)


def _dot_nt(a, b):
    return lax.dot_general(a, b, (((1,), (1,)), ((), ())), preferred_element_type=F32)


def _dot_tn(a, b):
    return lax.dot_general(a, b, (((0,), (0,)), ((), ())), preferred_element_type=F32)


def _dot3_left01(t01, x):
    hi, mid, lo = _split3(x)
    return _dot(t01, hi) + _dot(t01, mid) + _dot(t01, lo)


def _dot3_right01(x, t01):
    hi, mid, lo = _split3(x)
    return _dot(hi, t01) + _dot(mid, t01) + _dot(lo, t01)


def _bf16_bits(x):
    u = lax.bitcast_convert_type(x, U32)
    return (u + jnp.uint32(0x7FFF) + ((u >> 16) & jnp.uint32(1))) >> 16


def _store_packed_rows(ref, y):
    tm, d = y.shape
    half = d // 2
    rs = ROW_SUBLANES
    words = []
    for s in range(rs):
        lo = _bf16_bits(y[:, LANES * s:LANES * (s + 1)])
        hi = _bf16_bits(y[:, half + LANES * s:half + LANES * (s + 1)])
        words.append(((hi << 16) | lo).reshape(tm // rs, rs, LANES))
    ref[...] = jnp.swapaxes(jnp.stack(words, axis=1), 1, 2).reshape(tm * rs, LANES)


def _load_packed_rows(ref):
    rs = ROW_SUBLANES
    tm = ref.shape[0] // rs
    w = jnp.swapaxes(ref[...].reshape(tm // rs, rs, rs, LANES), 1, 2)
    return [w[:, s].reshape(tm, LANES) for s in range(rs)]


def _unpack_lo(u):
    return lax.bitcast_convert_type(u << 16, F32)


def _unpack_hi(u):
    return lax.bitcast_convert_type(u & jnp.uint32(0xFFFF0000), F32)


def _mod_kernel(ct_ref, w_ref, b_ref, o_ref):
    k = pl.program_id(2)
    nb = o_ref.shape[1]

    @pl.when(k == 0)
    def _():
        o_ref[0] = jnp.broadcast_to(b_ref[0], o_ref.shape[1:])

    ct = ct_ref[...]
    cond = ct * jax.nn.sigmoid(ct)
    w = w_ref[0]
    for b in range(nb):
        o_ref[0, b:b + 1, :] += jnp.sum(w * cond[:, b:b + 1], axis=0, keepdims=True)


def _mod_all(c, mod_w, mod_b):
    nb, d = c.shape
    nl, _, nout = mod_w.shape
    tk, tn = 512, 2048
    ct = c.T
    return pl.pallas_call(
        _mod_kernel,
        grid=(nl, nout // tn, d // tk),
        in_specs=[pl.BlockSpec((tk, nb), lambda l, j, k: (k, 0)),
                  pl.BlockSpec((1, tk, tn), lambda l, j, k: (l, k, j)),
                  pl.BlockSpec((1, 1, tn), lambda l, j, k: (l, 0, j))],
        out_specs=pl.BlockSpec((1, nb, tn), lambda l, j, k: (l, 0, j)),
        out_shape=jax.ShapeDtypeStruct((nl, nb, nout), F32),
        compiler_params=_cparams(("parallel", "parallel", "arbitrary"), 2 * tk * tn * 4),
    )(ct, mod_w, mod_b.reshape(nl, 1, nout))


def _inproj_kernel(x_ref, g_ref, mod_ref, w_ref, o_ref, h_ref):
    @pl.when(pl.program_id(1) == 0)
    def _():
        rc = NORM_ROWS
        gain = g_ref[...] * (1.0 + mod_ref[0, 1:2, :])
        shift = mod_ref[0, 0:1, :]

        def chunk(c, carry):
            rows = pl.ds(pl.multiple_of(c * rc, rc), rc)
            x = x_ref[rows, :]
            r = lax.rsqrt(jnp.mean(x * x, axis=-1, keepdims=True) + EPS)
            h_ref[rows, :] = (x * r * gain + shift).astype(BF16)
            return carry
        lax.fori_loop(0, x_ref.shape[0] // rc, chunk, 0, unroll=4)

    o_ref[...] = _dot(h_ref[...], w_ref[0]).astype(o_ref.dtype)


def _inproj(x2, g, mod, w, layer, seq):
    n, d = x2.shape
    ncol = w.shape[2]
    tm, tn = 512, ncol // 2
    per_b = seq // tm
    est = 2 * tm * d * 4 + tm * d * 2 + 2 * d * tn * 2 + 2 * tm * tn * 2 + tm * tn * 4
    return pl.pallas_call(
        _inproj_kernel,
        grid=(n // tm, ncol // tn),
        in_specs=[pl.BlockSpec((tm, d), lambda i, j: (i, 0)),
                  pl.BlockSpec((1, d), lambda i, j: (0, 0)),
                  pl.BlockSpec((1, 6, d), lambda i, j: (i // per_b, 0, 0)),
                  pl.BlockSpec((1, d, tn), lambda i, j: (layer, 0, j))],
        out_specs=pl.BlockSpec((tm, tn), lambda i, j: (i, j)),
        out_shape=jax.ShapeDtypeStruct((n, ncol), BF16),
        scratch_shapes=[pltpu.VMEM((tm, d), BF16)],
        compiler_params=_cparams(("parallel", "arbitrary"), est),
    )(x2, g.reshape(1, d), mod, w)


def _rope128(y, tc, ts):
    return y * tc + pltpu.roll(y, 64, 1) * ts


def _mla_prep_kernel(ql_ref, kvl_ref, kr_ref, gq_ref, gkv_ref, wq_ref, wkv_ref, tc_ref, ts_ref,
                     q_ref, k_ref, v_ref):
    scale = (MLA_NOPE + MLA_ROPE) ** -0.5
    tc = tc_ref[...]
    ts = ts_ref[...]
    qn = _rms(ql_ref[...].astype(F32), gq_ref[...]).astype(BF16)
    yq = _dot(qn, wq_ref[...])
    kvn = _rms(kvl_ref[...].astype(F32), gkv_ref[...]).astype(BF16)
    ykv = _dot(kvn, wkv_ref[...])
    k_roped = _rope128(kr_ref[...].astype(F32), tc, ts).astype(BF16)
    for h in range(MLA_HEADS):
        o = 2 * LANES * h
        q_ref[:, o:o + LANES] = (yq[:, o:o + LANES] * scale).astype(BF16)
        q_ref[:, o + LANES:o + 2 * LANES] = (_rope128(yq[:, o + LANES:o + 2 * LANES], tc, ts) * scale).astype(BF16)
        k_ref[:, o:o + LANES] = ykv[:, LANES * h:LANES * (h + 1)].astype(BF16)
        k_ref[:, o + LANES:o + 2 * LANES] = k_roped
    v_ref[...] = ykv[:, MLA_HEADS * LANES:].astype(BF16)


def _mla_prep(proj, gq, gkv, wq, wkv, tc, ts, seq):
    n = proj.shape[0]
    tm = 512
    per_b = seq // tm
    hq = MLA_HEADS * 2 * LANES
    return pl.pallas_call(
        _mla_prep_kernel,
        grid=(n // tm,),
        in_specs=[pl.BlockSpec((tm, MLA_Q_RANK), lambda i: (i, P_QLAT // MLA_Q_RANK)),
                  pl.BlockSpec((tm, MLA_KV_RANK), lambda i: (i, P_KVLAT // MLA_KV_RANK)),
                  pl.BlockSpec((tm, LANES), lambda i: (i, P_KROPE // LANES)),
                  pl.BlockSpec((1, MLA_Q_RANK), lambda i: (0, 0)),
                  pl.BlockSpec((1, MLA_KV_RANK), lambda i: (0, 0)),
                  pl.BlockSpec((MLA_Q_RANK, hq), lambda i: (0, 0)),
                  pl.BlockSpec((MLA_KV_RANK, 2 * GROUP_WIDTH), lambda i: (0, 0)),
                  pl.BlockSpec((tm, LANES), lambda i: (i % per_b, 0)),
                  pl.BlockSpec((tm, LANES), lambda i: (i % per_b, 0))],
        out_specs=[pl.BlockSpec((tm, hq), lambda i: (i, 0)),
                   pl.BlockSpec((tm, hq), lambda i: (i, 0)),
                   pl.BlockSpec((tm, GROUP_WIDTH), lambda i: (i, 0))],
        out_shape=[jax.ShapeDtypeStruct((n, hq), BF16),
                   jax.ShapeDtypeStruct((n, hq), BF16),
                   jax.ShapeDtypeStruct((n, GROUP_WIDTH), BF16)],
        compiler_params=_cparams(("parallel",), 16 * 1024 * 1024),
    )(proj, proj, proj, gq.reshape(1, -1), gkv.reshape(1, -1), wq, wkv, tc, ts)


def _mla_flash_kernel(q_ref, k_ref, v_ref, o_ref):
    i = pl.program_id(2)
    tq = q_ref.shape[0]
    tk = tq
    q = q_ref[...]

    def step(j, carry, masked):
        m, l, acc = carry
        k = k_ref[pl.ds(pl.multiple_of(j * tk, tk), tk), :]
        v = v_ref[pl.ds(pl.multiple_of(j * tk, tk), tk), :]
        s = _dot_nt(q, k)
        if masked:
            qc = lax.broadcasted_iota(jnp.int32, (tq, tk), 0) // CHUNK
            kc = lax.broadcasted_iota(jnp.int32, (tq, tk), 1) // CHUNK
            s = jnp.where(kc <= qc, s, NEG_BIG)
        m_new = jnp.maximum(m, jnp.max(s, axis=-1, keepdims=True))
        alpha = jnp.exp(m - m_new)
        p = jnp.exp(s - m_new)
        l = alpha * l + jnp.sum(p, axis=-1, keepdims=True)
        acc = alpha * acc + _dot(p.astype(BF16), v)
        return m_new, l, acc

    init = (jnp.full((tq, 1), NEG_BIG, F32), jnp.zeros((tq, 1), F32), jnp.zeros((tq, v_ref.shape[1]), F32))
    carry = lax.fori_loop(0, i // 2, lambda jj, c: step(2 * jj + 1, step(2 * jj, c, False), False), init)
    carry = lax.cond(lax.rem(i, 2) == 1, lambda c: step(i - 1, c, False), lambda c: c, carry)
    m, l, acc = step(i, carry, True)
    o_ref[...] = (acc / l).astype(o_ref.dtype)


def _mla_flash(q, k, v, batch, seq):
    n = q.shape[0]
    tq = ATT_TQ
    nq = seq // tq
    est = 2 * (seq * 2 * LANES * 2 + seq * LANES * 2) + 8 * tq * tq * 4
    return pl.pallas_call(
        _mla_flash_kernel,
        grid=(batch, MLA_HEADS, nq),
        in_specs=[pl.BlockSpec((tq, 2 * LANES), lambda b, h, i: (b * nq + i, h)),
                  pl.BlockSpec((seq, 2 * LANES), lambda b, h, i: (b, h)),
                  pl.BlockSpec((seq, LANES), lambda b, h, i: (b, h))],
        out_specs=pl.BlockSpec((tq, LANES), lambda b, h, i: (b * nq + i, h)),
        out_shape=jax.ShapeDtypeStruct((n, GROUP_WIDTH), BF16),
        compiler_params=_cparams(("parallel", "parallel", "arbitrary"), est),
    )(q, k, v)


def _sb_kernel(q_ref, k_ref, v_ref, o_ref):
    i = pl.program_id(2)
    tq = q_ref.shape[0]
    tk = tq
    sub = SB_SUB
    scale = q_ref.shape[1] ** -0.5
    q = q_ref[...]
    tri = (lax.broadcasted_iota(jnp.int32, (sub, sub), 0)
           > lax.broadcasted_iota(jnp.int32, (sub, sub), 1)).astype(BF16)

    def step(j, carry, masked):
        run, acc = carry
        k = k_ref[pl.ds(pl.multiple_of(j * tk, tk), tk), :]
        v = v_ref[pl.ds(pl.multiple_of(j * tk, tk), tk), :]
        u = _dot_nt(q, k) * (scale * LOG2_E)
        log_beta = jnp.minimum(u, 0.0) - jnp.log2(1.0 + jnp.exp2(-jnp.abs(u)))
        log_keep = log_beta - u
        if masked:
            vis = (lax.broadcasted_iota(jnp.int32, (tq, tk), 1)
                   < lax.broadcasted_iota(jnp.int32, (tq, tk), 0))
            log_keep = jnp.where(vis, log_keep, 0.0)
        keep_b = log_keep.astype(BF16)
        pieces = []
        for blk in range(tk // sub - 1, -1, -1):
            sl = slice(blk * sub, (blk + 1) * sub)
            pieces.append(_dot(keep_b[:, sl], tri) + run)
            run = run + jnp.sum(log_keep[:, sl], axis=-1, keepdims=True)
        later = jnp.concatenate(pieces[::-1], axis=1)
        att = jnp.exp2(log_beta + later)
        if masked:
            att = jnp.where(vis, att, 0.0)
        acc = acc + _dot(att.astype(BF16), v)
        return run, acc

    carry = step(i, (jnp.zeros((tq, 1), F32), jnp.zeros((tq, v_ref.shape[1]), F32)), True)
    carry = lax.fori_loop(0, i // 2, lambda jj, c: step(i - 2 - 2 * jj, step(i - 1 - 2 * jj, c, False), False), carry)
    run, acc = lax.cond(lax.rem(i, 2) == 1, lambda c: step(0, c, False), lambda c: c, carry)
    o_ref[...] = acc.astype(o_ref.dtype)


def _sb_attention(proj, batch, seq):
    n = proj.shape[0]
    tq = ATT_TQ
    nq = seq // tq
    cq, ck, cv = P_C // LANES, (P_C + GROUP_WIDTH) // LANES, (P_C + 2 * GROUP_WIDTH) // LANES
    est = 4 * seq * LANES * 2 + 10 * tq * tq * 4
    return pl.pallas_call(
        _sb_kernel,
        grid=(batch, SB_HEADS, nq),
        in_specs=[pl.BlockSpec((tq, LANES), lambda b, h, i: (b * nq + i, cq + h)),
                  pl.BlockSpec((seq, LANES), lambda b, h, i: (b, ck + h)),
                  pl.BlockSpec((seq, LANES), lambda b, h, i: (b, cv + h))],
        out_specs=pl.BlockSpec((tq, LANES), lambda b, h, i: (b * nq + i, h)),
        out_shape=jax.ShapeDtypeStruct((n, GROUP_WIDTH), BF16),
        compiler_params=_cparams(("parallel", "parallel", "arbitrary"), est),
    )(proj, proj, proj)


def _ca_kernel(q_ref, k_ref, v_ref, tab_ref, o_ref, kpad_ref, vpad_ref):
    i = pl.program_id(1)
    tq = q_ref.shape[0]
    past = CA_PAST_CHUNKS * CHUNK
    win = tq + past
    scale = LANES ** -0.5

    @pl.when(i == 0)
    def _():
        kpad_ref[0:past, :] = jnp.zeros((past, kpad_ref.shape[1]), kpad_ref.dtype)
        vpad_ref[0:past, :] = jnp.zeros((past, vpad_ref.shape[1]), vpad_ref.dtype)
        kpad_ref[past:, :] = k_ref[...]
        vpad_ref[past:, :] = v_ref[...]

    start = pl.multiple_of(i * tq, tq)
    for h in range(CA_HEADS):
        cols = slice(h * LANES, (h + 1) * LANES)
        kw = kpad_ref[pl.ds(start, win), cols]
        vw = vpad_ref[pl.ds(start, win), cols]
        s = _dot_nt(q_ref[:, cols], kw) * scale + tab_ref[0, h]
        m = jnp.max(s, axis=-1, keepdims=True)
        p = jnp.exp(s - m)
        l = jnp.sum(p, axis=-1, keepdims=True)
        o_ref[:, cols] = (_dot(p.astype(BF16), vw) / l).astype(o_ref.dtype)


def _ca_table(rel_bias, tq):
    past = CA_PAST_CHUNKS * CHUNK
    win = tq + past
    r = jnp.arange(tq)[:, None]
    c = jnp.arange(win)[None, :]
    ring = 1024
    assert ring >= tq + win - 1
    m = jnp.arange(ring)
    m = jnp.where(m >= win, m - ring, m)
    idx = jnp.clip(past - m, -(CHUNK - 1), CA_REL_PAST) + (CHUNK - 1)
    ringvals = rel_bias[:, idx].astype(F32)
    nh = rel_bias.shape[0]
    bias = jnp.tile(ringvals, (1, tq))[:, :tq * (ring - 1)].reshape(nh, tq, ring - 1)[:, :, :win]
    qc = r // CHUNK
    kc = c // CHUNK
    vis = (kc >= qc) & (kc <= qc + CA_PAST_CHUNKS)
    tabs = []
    n_var = past // tq + 1
    for v in range(n_var):
        first = (past - v * tq) // CHUNK if v < n_var - 1 else 0
        ok = vis & (kc >= first)
        tabs.append(jnp.where(ok[None], bias, NEG_BIG))
    return jnp.stack(tabs)


def _ca_attention(proj, table, batch, seq):
    n = proj.shape[0]
    tq = CA_TQ
    nq = seq // tq
    past = CA_PAST_CHUNKS * CHUNK
    win = tq + past
    n_var = table.shape[0]
    gw = GROUP_WIDTH
    cq, ck, cv = P_D // gw, (P_D + gw) // gw, (P_D + 2 * gw) // gw
    est = (4 * seq * gw * 2 + 2 * (seq + past) * gw * 2 + 2 * CA_HEADS * tq * win * 4 + 8 * tq * win * 4
           + 4 * tq * gw * 2)
    return pl.pallas_call(
        _ca_kernel,
        grid=(batch, nq),
        in_specs=[pl.BlockSpec((tq, gw), lambda b, i: (b * nq + i, cq)),
                  pl.BlockSpec((seq, gw), lambda b, i: (b, ck)),
                  pl.BlockSpec((seq, gw), lambda b, i: (b, cv)),
                  pl.BlockSpec((1, CA_HEADS, tq, win), lambda b, i: (jnp.minimum(i, n_var - 1), 0, 0, 0))],
        out_specs=pl.BlockSpec((tq, gw), lambda b, i: (b * nq + i, 0)),
        out_shape=jax.ShapeDtypeStruct((n, gw), BF16),
        scratch_shapes=[pltpu.VMEM((seq + past, gw), BF16), pltpu.VMEM((seq + past, gw), BF16)],
        compiler_params=_cparams(("parallel", "arbitrary"), est),
    )(proj, proj, proj, table)


def _ssd_kernel(z_ref, xbc_ref, dt_ref, cw_ref, cb_ref, dtb_ref, alog_ref, dskip_ref, ng_ref, ex_ref,
                o_ref, xbuf_ref, state_ref, y_ref):
    c = pl.program_id(1)
    lc = z_ref.shape[0]
    halo = SUBLANES
    gw = GROUP_WIDTH // SSM_GROUPS
    hpg = SSM_HEADS // SSM_GROUPS

    @pl.when(c == 0)
    def _():
        xbuf_ref[0:halo, :] = jnp.zeros((halo, xbuf_ref.shape[1]), F32)
        state_ref[...] = jnp.zeros(state_ref.shape, F32)

    xbuf_ref[halo:halo + lc, :] = xbc_ref[...].astype(F32)
    acc = jnp.broadcast_to(cb_ref[...], (lc, SSM_CONV_DIM))
    for k in range(SSM_CONV):
        acc = acc + cw_ref[k:k + 1, :] * xbuf_ref[pl.ds(halo - (SSM_CONV - 1) + k, lc), :]
    xbuf_ref[0:halo, :] = xbuf_ref[lc:lc + halo, :]
    u = acc * jax.nn.sigmoid(acc)
    xs = u[:, :GROUP_WIDTH]
    bm = u[:, GROUP_WIDTH:GROUP_WIDTH + SSM_GROUPS * SSM_STATE].astype(BF16)
    cm = u[:, GROUP_WIDTH + SSM_GROUPS * SSM_STATE:].astype(BF16)

    dt = jax.nn.softplus(dt_ref[...].astype(F32) + dtb_ref[...])
    a = -jnp.exp(alog_ref[...])
    da = dt * a
    row = lax.broadcasted_iota(jnp.int32, (lc, lc), 0)
    col = lax.broadcasted_iota(jnp.int32, (lc, lc), 1)
    lower = col <= row
    cs = _dot3_left01(lower.astype(BF16), da)
    cs_last = cs[lc - 1:lc, :]
    cs_t = cs.T
    dt_t = dt.T
    ex = ex_ref[...]
    ecs_x = _dot3_right01(jnp.exp(cs), ex)
    wdec_x = _dot3_right01(jnp.exp(cs_last - cs) * dt, ex)
    xs_b = xs.astype(BF16)

    for g in range(SSM_GROUPS):
        bg = bm[:, g * SSM_STATE:(g + 1) * SSM_STATE]
        cg = cm[:, g * SSM_STATE:(g + 1) * SSM_STATE]
        cb = _dot_nt(cg, bg)
        for r in range(hpg):
            h = g * hpg + r
            diff = cs[:, h:h + 1] - cs_t[h:h + 1, :]
            wgt = cb * jnp.exp(jnp.where(lower, diff, NEG_BIG)) * dt_t[h:h + 1, :]
            y_ref[:, h * SSM_HEAD_DIM:(h + 1) * SSM_HEAD_DIM] = _dot(
                wgt.astype(BF16), xs_b[:, h * SSM_HEAD_DIM:(h + 1) * SSM_HEAD_DIM])
        lo, hi = g * gw, (g + 1) * gw
        st = state_ref[g]
        y_off = _dot(cg, st.astype(BF16)) * ecs_x[:, lo:hi]
        y_ref[:, lo:hi] += y_off
        xw = (xs[:, lo:hi] * wdec_x[:, lo:hi]).astype(BF16)
        state_ref[g] = st * ecs_x[lc - 1:lc, lo:hi] + _dot_tn(bg, xw)

    zf = z_ref[...].astype(F32)
    y = (y_ref[...] + xs * dskip_ref[...]) * (zf * jax.nn.sigmoid(zf))
    for g in range(SSM_GROUPS):
        lo, hi = g * gw, (g + 1) * gw
        yg = y[:, lo:hi]
        yg = yg * lax.rsqrt(jnp.mean(yg * yg, axis=-1, keepdims=True) + EPS)
        o_ref[:, lo:hi] = (yg * ng_ref[:, lo:hi]).astype(o_ref.dtype)


def _ssd(proj, conv_w, conv_b, dt_bias, a_log, d_skip, norm_g, batch, seq):
    n = proj.shape[0]
    lc = SSD_LC
    nc = seq // lc
    dtb = jnp.zeros((1, LANES), F32).at[0, :SSM_HEADS].set(dt_bias)
    alog = jnp.full((1, LANES), NEG_BIG, F32).at[0, :SSM_HEADS].set(a_log)
    dskip = jnp.repeat(d_skip, SSM_HEAD_DIM).reshape(1, GROUP_WIDTH)
    ex = (jnp.arange(LANES)[:, None] == jnp.arange(GROUP_WIDTH)[None, :] // SSM_HEAD_DIM).astype(BF16)
    full = lambda shape: pl.BlockSpec(shape, lambda b, c: tuple(0 for _ in shape))
    return pl.pallas_call(
        _ssd_kernel,
        grid=(batch, nc),
        in_specs=[pl.BlockSpec((lc, GROUP_WIDTH), lambda b, c: (b * nc + c, P_Z // GROUP_WIDTH)),
                  pl.BlockSpec((lc, SSM_CONV_DIM), lambda b, c: (b * nc + c, P_XBC // SSM_CONV_DIM)),
                  pl.BlockSpec((lc, LANES), lambda b, c: (b * nc + c, P_DT // LANES)),
                  full((SSM_CONV, SSM_CONV_DIM)), full((1, SSM_CONV_DIM)), full((1, LANES)), full((1, LANES)),
                  full((1, GROUP_WIDTH)), full((1, GROUP_WIDTH)), full((LANES, GROUP_WIDTH))],
        out_specs=pl.BlockSpec((lc, GROUP_WIDTH), lambda b, c: (b * nc + c, 0)),
        out_shape=jax.ShapeDtypeStruct((n, GROUP_WIDTH), BF16),
        scratch_shapes=[pltpu.VMEM((lc + SUBLANES, SSM_CONV_DIM), F32),
                        pltpu.VMEM((SSM_GROUPS, SSM_STATE, GROUP_WIDTH // SSM_GROUPS), F32),
                        pltpu.VMEM((lc, GROUP_WIDTH), F32)],
        compiler_params=_cparams(("parallel", "arbitrary"), 16 * 1024 * 1024),
    )(proj, proj, proj, conv_w, conv_b.reshape(1, -1), dtb, alog, dskip, norm_g.reshape(1, -1), ex)


def _outproj_kernel(a_ref, b_ref, c_ref, d_ref, x_ref, gmix_ref, w_ref, mod_ref, gf_ref, rw_ref, rb_ref,
                    xo_ref, h_ref, e_ref, gt_ref):
    gw = GROUP_WIDTH
    a = _rms(a_ref[...].astype(F32), gmix_ref[0:1, :]).astype(BF16)
    c = _rms(c_ref[...].astype(F32), gmix_ref[1:2, :]).astype(BF16)
    d = _rms(d_ref[...].astype(F32), gmix_ref[2:3, :]).astype(BF16)
    mixed = (_dot(a, w_ref[0:gw, :]) + _dot(b_ref[...], w_ref[gw:2 * gw, :])
             + _dot(c, w_ref[2 * gw:3 * gw, :]) + _dot(d, w_ref[3 * gw:, :]))
    x = x_ref[...] + mod_ref[0, 2:3, :] * mixed
    xo_ref[...] = x
    h = _rms(x, gf_ref[...]) * (1.0 + mod_ref[0, 4:5, :]) + mod_ref[0, 3:4, :]
    _store_packed_rows(h_ref, h)

    hh = h.astype(BF16)
    hl = (h - hh.astype(F32)).astype(BF16)
    rw = rw_ref[...]
    wh = rw.astype(BF16)
    wl = (rw - wh.astype(F32)).astype(BF16)
    logits = _dot(hh, wh) + _dot(hh, wl) + _dot(hl, wh) + rb_ref[...]
    lane = lax.broadcasted_iota(jnp.int32, logits.shape, 1)
    vals, idxs = [], []
    for _ in range(TOP_K):
        m = jnp.max(logits, axis=-1, keepdims=True)
        idx = jnp.min(jnp.where(logits == m, lane, LANES), axis=-1, keepdims=True)
        vals.append(m)
        idxs.append(idx)
        logits = jnp.where(lane == idx, -jnp.inf, logits)
    ex = [jnp.exp(v - vals[0]) for v in vals]
    tot = ex[0] + ex[1] + ex[2] + ex[3]
    e_out = jnp.zeros(logits.shape, jnp.int32)
    g_out = jnp.zeros(logits.shape, F32)
    for k in range(TOP_K):
        e_out = jnp.where(lane == k, idxs[k], e_out)
        g_out = jnp.where(lane == k, ex[k] / tot, g_out)
    e_ref[...] = e_out
    gt_ref[...] = g_out


def _outproj(oa, ob, oc, od, x2, gmix, w_out, mod, gf, rw, rb, seq):
    n, d = x2.shape
    tm = 512
    per_b = seq // tm
    gw = GROUP_WIDTH
    row = lambda width: pl.BlockSpec((tm, width), lambda i: (i, 0))
    full = lambda shape: pl.BlockSpec(shape, lambda i: tuple(0 for _ in shape))
    est = 2 * d * d * 2 + 4 * tm * d * 4 + 2 * tm * d * 2 + 8 * tm * gw * 2 + 6 * tm * d * 4
    return pl.pallas_call(
        _outproj_kernel,
        grid=(n // tm,),
        in_specs=[row(gw), row(gw), row(gw), row(gw), row(d), full((3, gw)), full((d, d)),
                  pl.BlockSpec((1, 6, d), lambda i: (i // per_b, 0, 0)), full((1, d)),
                  full((d, LANES)), full((1, LANES))],
        out_specs=[row(d), pl.BlockSpec((tm * ROW_SUBLANES, LANES), lambda i: (i, 0)), row(LANES), row(LANES)],
        out_shape=[jax.ShapeDtypeStruct((n, d), F32), jax.ShapeDtypeStruct((n * ROW_SUBLANES, LANES), U32),
                   jax.ShapeDtypeStruct((n, LANES), jnp.int32), jax.ShapeDtypeStruct((n, LANES), F32)],
        compiler_params=_cparams(("parallel",), est),
    )(oa, ob, oc, od, x2, gmix, w_out, mod, gf.reshape(1, d), rw, rb)


def _row_copy(src_ref, src_row, dst_ref, dst_row, sem):
    rs = ROW_SUBLANES
    return pltpu.make_async_copy(src_ref.at[pl.ds(pl.multiple_of(src_row * rs, rs), rs)],
                                 dst_ref.at[pl.ds(pl.multiple_of(dst_row * rs, rs), rs)], sem)


def _expert_kernel(be_ref, nu_ref, tok_first_ref, step_ref, dst_last_ref, hp_ref, w1_ref, b1_ref, w2_ref,
                   b2_ref, y_ref, xa_ref, xb_ref, ya_ref, yb_ref, gsem, ssem):
    i = pl.program_id(0)
    nu = nu_ref[0]
    tm = MOE_BLOCK
    rs = ROW_SUBLANES

    def start_rows(copy_of_row):
        for r in range(tm):
            copy_of_row(r).start(priority=1)

    def wait_rows(src_ref, dst_ref, sem):
        for _ in range(tm):
            _row_copy(src_ref, 0, dst_ref, 0, sem).wait()

    @pl.when(i == 0)
    def _():
        start_rows(lambda r: _row_copy(hp_ref, tok_first_ref[0, 0, r], xa_ref, r, gsem.at[0]))
        yb_ref[...] = jnp.zeros(yb_ref.shape, U32)
        spare = pltpu.make_async_copy(yb_ref, y_ref.at[pl.ds(y_ref.shape[0] - 2 * tm * rs, tm * rs)], ssem.at[0])
        spare.start()
        spare.wait()

    def body(xcur, xnext, ycur, yprev, gcur, gnext, scur, sprev):
        wait_rows(hp_ref, xcur, gcur)
        words = _load_packed_rows(xcur)
        x = jnp.concatenate([_unpack_lo(u).astype(BF16) for u in words]
                            + [_unpack_hi(u).astype(BF16) for u in words], axis=1)
        start_rows(lambda r: _row_copy(hp_ref, step_ref[0, 0, r], xnext, r, gnext))
        start_rows(lambda r: _row_copy(yprev, r, y_ref, step_ref[0, 0, tm + r], sprev))
        cw = EXPERT_COL_CHUNK
        hu = jnp.concatenate([_dot(x, w1_ref[0, :, c:c + cw].astype(BF16)) for c in range(0, w1_ref.shape[2], cw)],
                             axis=1) + b1_ref[0]
        gate = jnp.minimum(hu[:, :D_EXPERT], SWIGLU_LIMIT)
        up = jnp.clip(hu[:, D_EXPERT:], -SWIGLU_LIMIT, SWIGLU_LIMIT)
        act = (gate * jax.nn.sigmoid(SWIGLU_ALPHA * gate) * (up + 1.0)).astype(BF16)
        y = jnp.concatenate([_dot(act, w2_ref[0, :, c:c + cw].astype(BF16)) for c in range(0, w2_ref.shape[2], cw)],
                            axis=1) + b2_ref[0]

        @pl.when(i > 0)
        def _():
            wait_rows(ycur, y_ref, scur)

        _store_packed_rows(ycur, y)

        @pl.when(i == nu - 1)
        def _():
            start_rows(lambda r: _row_copy(ycur, r, y_ref, dst_last_ref[0, 0, r], scur))
            wait_rows(yprev, y_ref, sprev)
            wait_rows(ycur, y_ref, scur)
            wait_rows(hp_ref, xnext, gnext)

    even = lax.rem(i, 2) == 0

    @pl.when(jnp.logical_and(i < nu, even))
    def _():
        body(xa_ref, xb_ref, ya_ref, yb_ref, gsem.at[0], gsem.at[1], ssem.at[0], ssem.at[1])

    @pl.when(jnp.logical_and(i < nu, jnp.logical_not(even)))
    def _():
        body(xb_ref, xa_ref, yb_ref, ya_ref, gsem.at[1], gsem.at[0], ssem.at[1], ssem.at[0])


def _experts(hp, row_tok, row_dst, block_e, n_used, n_slots, w1, b1, w2, b2, layer=0):
    tm = MOE_BLOCK
    nblk = row_tok.shape[0]
    d, de2 = w1.shape[1], w1.shape[2]
    de = w2.shape[1]
    rs = ROW_SUBLANES
    block_e = block_e + layer * N_EXPERTS
    last = lambda i, nu: jnp.minimum(i, nu[0] - 1)
    est = (2 * (d * de2 + de * d) * w1.dtype.itemsize + 4 * tm * rs * LANES * 4 + 4 * tm * d * 4
           + 3 * tm * de2 * 4)
    first_prev = (n_slots - tm + jnp.arange(tm, dtype=jnp.int32)).reshape(1, 1, tm)
    step_rows = jnp.concatenate([jnp.concatenate([row_tok[1:], row_tok[-1:]], axis=0),
                                 jnp.concatenate([first_prev, row_dst[:-1]], axis=0)], axis=2)
    dst_last = lax.dynamic_slice_in_dim(row_dst, n_used[0] - 1, 1, axis=0)
    smem = lambda width, f: pl.BlockSpec((1, 1, width), f, memory_space=pltpu.SMEM)
    grid_spec = pltpu.PrefetchScalarGridSpec(
        num_scalar_prefetch=2,
        grid=(nblk,),
        in_specs=[smem(tm, lambda i, be, nu: (0, 0, 0)),
                  smem(2 * tm, lambda i, be, nu: (last(i, nu), 0, 0)),
                  smem(tm, lambda i, be, nu: (0, 0, 0)),
                  pl.BlockSpec(memory_space=pl.ANY),
                  pl.BlockSpec((1, d, de2), lambda i, be, nu: (be[last(i, nu)], 0, 0)),
                  pl.BlockSpec((1, 1, de2), lambda i, be, nu: (be[last(i, nu)], 0, 0)),
                  pl.BlockSpec((1, de, d), lambda i, be, nu: (be[last(i, nu)], 0, 0)),
                  pl.BlockSpec((1, 1, d), lambda i, be, nu: (be[last(i, nu)], 0, 0))],
        out_specs=pl.BlockSpec(memory_space=pl.ANY),
        scratch_shapes=[pltpu.VMEM((tm * rs, LANES), U32) for _ in range(4)]
        + [pltpu.SemaphoreType.DMA((2,)), pltpu.SemaphoreType.DMA((2,))],
    )
    return pl.pallas_call(
        _expert_kernel,
        grid_spec=grid_spec,
        out_shape=jax.ShapeDtypeStruct((n_slots * rs, LANES), U32),
        compiler_params=_cparams(("arbitrary",), est),
    )(block_e, n_used, row_tok, step_rows, dst_last, hp, w1, b1.reshape(b1.shape[0], 1, de2), w2,
      b2.reshape(b2.shape[0], 1, d))


def _combine_kernel(y0_ref, y1_ref, y2_ref, y3_ref, g_ref, x_ref, mod_ref, o_ref):
    tm, d = x_ref.shape
    half = d // 2
    rs = ROW_SUBLANES
    rc = COMBINE_ROWS

    def chunk(c, carry):
        r0 = pl.multiple_of(c * rc, rc)
        rows = pl.ds(r0, rc)
        g = g_ref[rows, :]
        lo = [jnp.zeros((rc, LANES), F32) for _ in range(rs)]
        hi = [jnp.zeros((rc, LANES), F32) for _ in range(rs)]
        for k, y_ref in enumerate((y0_ref, y1_ref, y2_ref, y3_ref)):
            blk = y_ref[pl.ds(pl.multiple_of(r0 * rs, rc * rs), rc * rs), :]
            w = jnp.swapaxes(blk.reshape(rc // rs, rs, rs, LANES), 1, 2)
            gk = g[:, k:k + 1]
            for s in range(rs):
                u = w[:, s].reshape(rc, LANES)
                lo[s] = lo[s] + _unpack_lo(u) * gk
                hi[s] = hi[s] + _unpack_hi(u) * gk
        for s in range(rs):
            a, b = LANES * s, half + LANES * s
            o_ref[rows, a:a + LANES] = x_ref[rows, a:a + LANES] + mod_ref[0, 5:6, a:a + LANES] * lo[s]
            o_ref[rows, b:b + LANES] = x_ref[rows, b:b + LANES] + mod_ref[0, 5:6, b:b + LANES] * hi[s]
        return carry
    lax.fori_loop(0, tm // rc, chunk, 0)


def _combine(y4, gates, x2, mod, seq):
    n, d = x2.shape
    tm = 512
    per_b = seq // tm
    nt = n // tm
    rows = tm * ROW_SUBLANES
    est = 2 * TOP_K * rows * LANES * 4 + 4 * tm * d * 4 + 2 * tm * LANES * 4
    yspec = lambda k: pl.BlockSpec((rows, LANES), lambda i: (k * nt + i, 0))
    return pl.pallas_call(
        _combine_kernel,
        grid=(nt,),
        in_specs=[yspec(0), yspec(1), yspec(2), yspec(3),
                  pl.BlockSpec((tm, LANES), lambda i: (i, 0)),
                  pl.BlockSpec((tm, d), lambda i: (i, 0)),
                  pl.BlockSpec((1, 6, d), lambda i: (i // per_b, 0, 0))],
        out_specs=pl.BlockSpec((tm, d), lambda i: (i, 0)),
        out_shape=jax.ShapeDtypeStruct((n, d), F32),
        compiler_params=_cparams(("parallel",), est),
    )(y4, y4, y4, y4, gates, x2, mod)


def _final_norm_kernel(x_ref, g_ref, o_ref):
    o_ref[...] = _rms(x_ref[...], g_ref[...])


def _final_norm(x2, g):
    n, d = x2.shape
    tm = 512
    return pl.pallas_call(
        _final_norm_kernel,
        grid=(n // tm,),
        in_specs=[pl.BlockSpec((tm, d), lambda i: (i, 0)), pl.BlockSpec((1, d), lambda i: (0, 0))],
        out_specs=pl.BlockSpec((tm, d), lambda i: (i, 0)),
        out_shape=jax.ShapeDtypeStruct((n, d), F32),
        compiler_params=_cparams(("parallel",), 4 * tm * d * 4),
    )(x2, g.reshape(1, d))


def _swap_halves(w):
    half = w.shape[-1] // 2
    return jnp.concatenate([w[..., half:], w[..., :half]], axis=-1)


def _pack_w_in_kernel(w_ref, o_ref):
    w = w_ref[0]
    b0, c0 = A_COLS, A_COLS + B_COLS
    kr0 = MLA_Q_RANK + MLA_KV_RANK
    half = MLA_ROPE // 2
    pieces = [w[:, :MLA_Q_RANK],
              w[:, b0:b0 + GROUP_WIDTH + SSM_CONV_DIM],
              w[:, MLA_Q_RANK:kr0 + MLA_ROPE],
              w[:, kr0 + half:kr0 + MLA_ROPE], w[:, kr0:kr0 + half],
              w[:, c0 - SSM_HEADS:c0], jnp.zeros((w.shape[0], LANES - SSM_HEADS), w.dtype),
              w[:, c0:]]
    o_ref[0] = jnp.concatenate(pieces, axis=1).astype(BF16)


def _pack_w_in(w_in):
    nl, d, nc = w_in.shape
    tr = 256
    est = 2 * tr * nc * 4 + 2 * tr * P_COLS * 2 + 2 * tr * P_COLS * 4
    return pl.pallas_call(
        _pack_w_in_kernel,
        grid=(nl, d // tr),
        in_specs=[pl.BlockSpec((1, tr, nc), lambda l, i: (l, i, 0))],
        out_specs=pl.BlockSpec((1, tr, P_COLS), lambda l, i: (l, i, 0)),
        out_shape=jax.ShapeDtypeStruct((nl, d, P_COLS), BF16),
        compiler_params=_cparams(("parallel", "parallel"), est),
    )(w_in)


def _pack_wq(wq):
    r = wq.shape[0]
    w = wq.reshape(r, MLA_HEADS, MLA_NOPE + MLA_ROPE)
    rope = w[..., MLA_NOPE:]
    return jnp.concatenate([w[..., :MLA_NOPE], rope, _swap_halves(rope)], axis=-1).reshape(r, -1).astype(BF16)


def _pack_wkv(wkv):
    r = wkv.shape[0]
    w = wkv.reshape(r, MLA_HEADS, MLA_NOPE + MLA_V)
    return jnp.concatenate([w[..., :MLA_NOPE].reshape(r, -1), w[..., MLA_NOPE:].reshape(r, -1)], axis=-1).astype(BF16)


def _rope_tables(seq):
    inv = 1.0 / (ROPE_THETA ** (jnp.arange(0, MLA_ROPE, 2, dtype=F32) / MLA_ROPE))
    ang = jnp.arange(seq, dtype=F32)[:, None] * inv[None, :]
    cos, sin = jnp.cos(ang), jnp.sin(ang)
    zero = jnp.zeros((seq, LANES - MLA_ROPE), F32)
    return jnp.concatenate([cos, cos, zero], axis=1), jnp.concatenate([-sin, sin, zero], axis=1)


def _route(top_e, n_tok):
    n_assign = n_tok * TOP_K
    n_blocks = -(-(n_assign + N_EXPERTS * (MOE_BLOCK - 1)) // MOE_BLOCK)
    n_rows = n_blocks * MOE_BLOCK
    dest, meta = _route_dest(top_e)
    padded_end = meta[2, :N_EXPERTS]
    block_start = jnp.arange(n_blocks, dtype=jnp.int32) * MOE_BLOCK
    block_e = jnp.minimum(jnp.sum((padded_end[None, :] <= block_start[:, None]).astype(jnp.int32), axis=1),
                          N_EXPERTS - 1)
    n_used = (padded_end[N_EXPERTS - 1] // MOE_BLOCK).reshape(1)
    row_assign = _route_invert(meta, dest[:, :TOP_K].reshape(n_assign), n_rows)
    real = row_assign >= 0
    row = jnp.arange(n_rows, dtype=jnp.int32)
    row_tok = jnp.where(real, row_assign // TOP_K, 0)
    spare = n_assign + (row // MOE_BLOCK % 2) * MOE_BLOCK + row % MOE_BLOCK
    row_dst = jnp.where(real, row_assign % TOP_K * n_tok + row_assign // TOP_K, spare)
    shape = (n_blocks, 1, MOE_BLOCK)
    return row_tok.reshape(shape), row_dst.reshape(shape), block_e, n_used, n_assign + 2 * MOE_BLOCK


NORM_ROWS = 16
ROUTE_TM = 512
COMBINE_ROWS = 16


def _route_dest_kernel(e_ref, dest_ref, meta_ref, carry_ref, base_ref):
    p = pl.program_id(0)
    i = pl.program_id(1)
    tm = e_ref.shape[0]
    e = e_ref[...]
    lane = lax.broadcasted_iota(jnp.int32, (tm, LANES), 1)
    hit = [lane == e[:, k:k + 1] for k in range(TOP_K)]
    tot = sum(h.astype(F32) for h in hit)

    @pl.when(jnp.logical_and(p == 0, i == 0))
    def _():
        carry_ref[...] = jnp.zeros(carry_ref.shape, F32)
        meta_ref[...] = jnp.zeros(meta_ref.shape, jnp.int32)

    @pl.when(jnp.logical_and(p == 0, i == pl.num_programs(1) - 1))
    def _():
        counts = carry_ref[...] + jnp.sum(tot, axis=0, keepdims=True)
        padded = jnp.floor((counts + (MOE_BLOCK - 1)) * (1.0 / MOE_BLOCK)) * MOE_BLOCK
        upper = (lax.broadcasted_iota(jnp.int32, (LANES, LANES), 0)
                 <= lax.broadcasted_iota(jnp.int32, (LANES, LANES), 1)).astype(BF16)
        padded_end = _dot3_right01(jnp.broadcast_to(padded, (SUBLANES, LANES)), upper)[0:1]
        base_ref[...] = padded_end - padded
        meta_ref[0:1, :] = counts.astype(jnp.int32)
        meta_ref[1:2, :] = (padded_end - padded).astype(jnp.int32)
        meta_ref[2:3, :] = padded_end.astype(jnp.int32)
        carry_ref[...] = -jnp.sum(tot, axis=0, keepdims=True)

    @pl.when(p == 1)
    def _():
        earlier = (lax.broadcasted_iota(jnp.int32, (tm, tm), 1)
                   < lax.broadcasted_iota(jnp.int32, (tm, tm), 0)).astype(BF16)
        pos = _dot(earlier, tot.astype(BF16)) + carry_ref[...] + base_ref[...]
        dest = jnp.zeros((tm, LANES), jnp.int32)
        for k in range(TOP_K):
            dk = jnp.sum(jnp.where(hit[k], pos, 0.0), axis=-1, keepdims=True)
            dest = jnp.where(lane == k, dk.astype(jnp.int32), dest)
        dest_ref[...] = dest

    carry_ref[...] += jnp.sum(tot, axis=0, keepdims=True)


def _route_dest(top_e):
    n = top_e.shape[0]
    tm = ROUTE_TM
    return pl.pallas_call(
        _route_dest_kernel,
        grid=(2, n // tm),
        in_specs=[pl.BlockSpec((tm, LANES), lambda p, i: (i, 0))],
        out_specs=[pl.BlockSpec((tm, LANES), lambda p, i: (i * p, 0)),
                   pl.BlockSpec((SUBLANES, LANES), lambda p, i: (0, 0))],
        out_shape=[jax.ShapeDtypeStruct((n, LANES), jnp.int32), jax.ShapeDtypeStruct((SUBLANES, LANES), jnp.int32)],
        scratch_shapes=[pltpu.VMEM((1, LANES), F32), pltpu.VMEM((1, LANES), F32)],
        compiler_params=_cparams(("arbitrary", "arbitrary"), 8 * tm * tm * 4),
    )(top_e)


ROUTE_CHUNK = 2048


def _route_invert_kernel(meta_ref, dest_ref, out_ref):
    s = pl.program_id(0)
    ch = dest_ref.shape[2]
    per = SUBLANES

    def clear(r, carry):
        out_ref[r] = -1
        return carry

    @pl.when(s == 0)
    def _():
        def expert_padding(e, carry):
            return lax.fori_loop(meta_ref[1, e] + meta_ref[0, e], meta_ref[2, e], clear, carry)
        lax.fori_loop(0, N_EXPERTS, expert_padding, 0)
        lax.fori_loop(meta_ref[2, N_EXPERTS - 1], out_ref.shape[0], clear, 0)

    def place(j, carry):
        for u in range(per):
            a = j * per + u
            out_ref[dest_ref[0, 0, a]] = s * ch + a
        return carry
    lax.fori_loop(0, ch // per, place, 0)


def _route_invert(meta, dest_flat, n_rows):
    ch = ROUTE_CHUNK
    n_place = dest_flat.shape[0] // ch
    assert n_place * ch == dest_flat.shape[0]
    return pl.pallas_call(
        _route_invert_kernel,
        grid=(n_place,),
        in_specs=[pl.BlockSpec(memory_space=pltpu.SMEM),
                  pl.BlockSpec((1, 1, ch), lambda s: (s, 0, 0), memory_space=pltpu.SMEM)],
        out_specs=pl.BlockSpec(memory_space=pltpu.SMEM),
        out_shape=jax.ShapeDtypeStruct((n_rows,), jnp.int32),
        compiler_params=pltpu.CompilerParams(dimension_semantics=("arbitrary",)),
    )(meta, dest_flat.reshape(n_place, 1, ch))


def kernel(x, c, attn_norm, ffn_norm, mod_w, mod_b, w_in, mla_q_norm, mla_w_q_up, mla_kv_norm, mla_w_kv_up,
           ssm_conv_w, ssm_conv_b, ssm_dt_bias, ssm_a_log, ssm_d, ssm_norm, ca_rel_bias, mix_out_norm, w_out,
           router_w, router_b, moe_w1, moe_b1, moe_w2, moe_b2, final_norm):
    batch, seq, d = x.shape
    n = batch * seq
    depth = w_in.shape[0]
    x2 = x.reshape(n, d)
    mod_all = _mod_all(c, mod_w, mod_b).reshape(depth, batch, 6, d)
    tc, ts = _rope_tables(seq)
    w_in_packed = _pack_w_in(w_in)
    for l in range(depth):
        mod = mod_all[l]
        proj = _inproj(x2, attn_norm[l], mod, w_in_packed, l, seq)
        q, k, v = _mla_prep(proj, mla_q_norm[l], mla_kv_norm[l], _pack_wq(mla_w_q_up[l]),
                            _pack_wkv(mla_w_kv_up[l]), tc, ts, seq)
        out_a = _mla_flash(q, k, v, batch, seq)
        out_b = _ssd(proj, ssm_conv_w[l], ssm_conv_b[l], ssm_dt_bias[l], ssm_a_log[l], ssm_d[l], ssm_norm[l],
                     batch, seq)
        out_c = _sb_attention(proj, batch, seq)
        out_d = _ca_attention(proj, _ca_table(ca_rel_bias[l], CA_TQ), batch, seq)
        rw = jnp.zeros((d, LANES), F32).at[:, :N_EXPERTS].set(router_w[l])
        rb = jnp.full((1, LANES), NEG_BIG, F32).at[0, :N_EXPERTS].set(router_b[l])
        x2, hp, top_e, gates = _outproj(out_a, out_b, out_c, out_d, x2, mix_out_norm[l], w_out[l].astype(BF16),
                                        mod, ffn_norm[l], rw, rb, seq)
        row_tok, row_dst, block_e, n_used, n_slots = _route(top_e, n)
        y4 = _experts(hp, row_tok, row_dst, block_e, n_used, n_slots,
                      moe_w1.reshape(-1, *moe_w1.shape[2:]), moe_b1.reshape(-1, moe_b1.shape[2]),
                      moe_w2.reshape(-1, *moe_w2.shape[2:]), moe_b2.reshape(-1, moe_b2.shape[2]), layer=l)
        x2 = _combine(y4, gates, x2, mod, seq)
    return _final_norm(x2, final_norm).reshape(batch, seq, d)
```
